```python
import math
import jax, jax.numpy as jnp
from jax import lax
import numpy as np

D_MODEL = 1024
BATCH = 2
SEQ = 8192
DEPTH = 4
DEC_BATCH = 128
DEC_SEQ = 1
PAST_LEN = 8192
PAGE_SIZE = 128

D_MIX = D_MODEL
HEAD_DIM = 64
A_WIDTH = D_MIX // 4
A_HEADS = A_WIDTH // HEAD_DIM
CHUNK = 128
B_WIDTH = D_MIX // 4
ATT_HEADS = B_WIDTH // HEAD_DIM
KV_HEADS = 2
WINDOW = 128
C_WIDTH = D_MIX // 2
SSM_HEADS = C_WIDTH // HEAD_DIM
SSM_HEAD_DIM = HEAD_DIM
SSM_GROUPS = 2
D_STATE = 128
CONV_W = 4
SSD_CHUNK = 128
CONV_DIM = C_WIDTH + 2 * SSM_GROUPS * D_STATE
D_FF = 4 * D_MODEL
D_IN = 2 * A_WIDTH + B_WIDTH + 2 * KV_HEADS * HEAD_DIM + C_WIDTH + CONV_DIM + SSM_HEADS
ALPHA = (2 * DEPTH) ** 0.25
BETA = (8 * DEPTH) ** -0.25
LN_EPS = 1e-5
RMS_EPS = 1e-6

kernel_name = 'hybrid_sgu_swa_ssd_decoder_step'


def _layer_norm(x, g, b):
    xf = x.astype(jnp.float32)
    mu = jnp.mean(xf, axis=-1, keepdims=True)
    var = jnp.mean(jnp.square(xf - mu), axis=-1, keepdims=True)
    return ((xf - mu) * lax.rsqrt(var + LN_EPS) * g.astype(jnp.float32) + b.astype(jnp.float32)).astype(x.dtype)


def _split_proj(proj):
    sizes = (A_WIDTH, A_WIDTH, B_WIDTH, KV_HEADS * HEAD_DIM, KV_HEADS * HEAD_DIM, C_WIDTH, CONV_DIM, SSM_HEADS)
    bounds, acc = [], 0
    for s in sizes[:-1]:
        acc += s
        bounds.append(acc)
    return jnp.split(proj, bounds, axis=-1)


def _spatial_gate(u, vn, w_s, b_s):
    bn, L, _ = u.shape
    lc = min(L, CHUNK)
    nc = L // lc
    causal = jnp.tril(jnp.ones((lc, lc), dtype=bool))
    w = jnp.where(causal, w_s[:, :lc, :lc], 0.0)
    vc = vn.reshape(bn, nc, lc, A_HEADS, HEAD_DIM)
    mix = jnp.einsum('hts,bcshd->bcthd', w, vc) + b_s[:, :lc].T[:, :, None]
    return u * mix.reshape(bn, L, A_WIDTH)


def _sink_attend(qb, kb, vb, mask, sinks):
    g = ATT_HEADS // KV_HEADS
    scores = jnp.einsum('bntkgd,bnskd->bnkgts', qb, kb).astype(jnp.float32) * (HEAD_DIM ** -0.5)
    scores = jnp.where(mask[None, :, None, None], scores, -jnp.inf)
    sink = jnp.broadcast_to(sinks.astype(jnp.float32).reshape(1, 1, KV_HEADS, g, 1, 1), scores.shape[:-1] + (1,))
    probs = jax.nn.softmax(jnp.concatenate([scores, sink], axis=-1), axis=-1)[..., :-1]
    return jnp.einsum('bnkgts,bnskd->bntkgd', probs.astype(vb.dtype), vb)


def _window_attn_prompt(q, k, v, sinks):
    bn, L = q.shape[:2]
    nb = L // WINDOW
    g = ATT_HEADS // KV_HEADS
    qb = q.reshape(bn, nb, WINDOW, KV_HEADS, g, HEAD_DIM)

    def band(t):
        cur = t.reshape(bn, nb, WINDOW, KV_HEADS, HEAD_DIM)
        prev = jnp.concatenate([jnp.zeros_like(cur[:, :1]), cur[:, :-1]], axis=1)
        return jnp.concatenate([prev, cur], axis=2)

    t = jnp.arange(WINDOW)[:, None]
    s = jnp.arange(2 * WINDOW)[None, :]
    blk = jnp.arange(nb)[:, None, None]
    mask = (s > t) & (s <= t + WINDOW) & ((blk > 0) | (s >= WINDOW))
    out = _sink_attend(qb, band(k), band(v), mask, sinks)
    return out.reshape(bn, L, B_WIDTH)


def _window_attn_sample(q, k, v, k_buf, v_buf, sinks):
    bn, L = q.shape[:2]
    g = ATT_HEADS // KV_HEADS
    qb = q.reshape(bn, 1, L, KV_HEADS, g, HEAD_DIM)
    kb = jnp.concatenate([k_buf.astype(k.dtype), k], axis=1)[:, None]
    vb = jnp.concatenate([v_buf.astype(v.dtype), v], axis=1)[:, None]
    t = jnp.arange(L)[:, None]
    s = jnp.arange(WINDOW + L)[None, :]
    mask = ((s > t) & (s <= t + WINDOW))[None]
    out = _sink_attend(qb, kb, vb, mask, sinks)
    return out.reshape(bn, L, B_WIDTH)


def _causal_conv(xbc, buf, conv_w, conv_b):
    L = xbc.shape[1]
    full = jnp.concatenate([buf.astype(xbc.dtype), xbc], axis=1)
    acc = conv_b
    for i in range(CONV_W):
        acc = acc + full[:, i:i + L] * conv_w[i]
    return jax.nn.silu(acc), full[:, L:]


def _ssd(x, dt, a, bm, cm, h0):
    bn, L = x.shape[:2]
    hg = SSM_HEADS // SSM_GROUPS
    lc = min(L, SSD_CHUNK)
    nc = L // lc

    def chunks(t):
        return jnp.moveaxis(t.reshape((bn, nc, lc) + t.shape[2:]), 1, 0)

    xs = chunks(x.reshape(bn, L, SSM_GROUPS, hg, SSM_HEAD_DIM))
    dts = chunks(dt.reshape(bn, L, SSM_GROUPS, hg))
    bs = chunks(bm)
    cs = chunks(cm)
    a_g = a.reshape(SSM_GROUPS, hg)
    causal = jnp.tril(jnp.ones((lc, lc), dtype=bool))[None, :, :, None, None]

    def step(h, inp):
        xc, dtc, bc, cc = inp
        cum = jnp.cumsum(dtc * a_g, axis=1)
        diff = cum[:, :, None] - cum[:, None, :]
        decay = jnp.where(causal, jnp.exp(jnp.where(causal, diff, 0.0)), 0.0)
        cb = jnp.einsum('btgn,bsgn->btsg', cc, bc)
        y = jnp.einsum('btsgh,bsghp->btghp', cb[..., None] * decay * dtc[:, None], xc)
        y = y + jnp.einsum('btgn,bghpn->btghp', cc, h) * jnp.exp(cum)[..., None]
        to_end = jnp.exp(cum[:, -1:] - cum) * dtc
        h = h * jnp.exp(cum[:, -1])[..., None, None] + jnp.einsum('bsgh,bsghp,bsgn->bghpn', to_end, xc, bc)
        return h, y

    h, ys = lax.scan(step, h0.reshape(bn, SSM_GROUPS, hg, SSM_HEAD_DIM, D_STATE), (xs, dts, bs, cs))
    y = jnp.moveaxis(ys, 0, 1).reshape(bn, L, SSM_HEADS, SSM_HEAD_DIM)
    return y, h.reshape(bn, SSM_HEADS, SSM_HEAD_DIM, D_STATE)


def _hybrid_layer(x, k_buf, v_buf, conv_buf, ssm_h, w_in, ln_v_g, ln_v_b, w_s, b_s, sinks,
                  conv_w, conv_b, dt_bias, a_log, d_skip, gn_w, w_out,
                  ln1_g, ln1_b, w1, w2, ln2_g, ln2_b):
    bn, L, _ = x.shape
    f32 = jnp.float32
    proj = jnp.einsum('bld,de->ble', x, w_in)
    a_u, a_v, q, k, v, z, xbc, dt_raw = _split_proj(proj)

    vn = _layer_norm(jax.nn.gelu(a_v), ln_v_g, ln_v_b)
    out_a = _spatial_gate(jax.nn.gelu(a_u), vn, w_s, b_s)

    q = q.reshape(bn, L, ATT_HEADS, HEAD_DIM)
    k = k.reshape(bn, L, KV_HEADS, HEAD_DIM)
    v = v.reshape(bn, L, KV_HEADS, HEAD_DIM)
    if k_buf is None:
        out_b = _window_attn_prompt(q, k, v, sinks)
        k_new, v_new = k[:, -WINDOW:], v[:, -WINDOW:]
    else:
        out_b = _window_attn_sample(q, k, v, k_buf, v_buf, sinks)
        k_new, v_new = k, v

    xbc_act, conv_new = _causal_conv(xbc, conv_buf, conv_w, conv_b)
    xs, bm, cm = jnp.split(xbc_act, [C_WIDTH, C_WIDTH + SSM_GROUPS * D_STATE], axis=-1)
    xs_h = xs.reshape(bn, L, SSM_HEADS, SSM_HEAD_DIM).astype(f32)
    dt = jax.nn.softplus(dt_raw.astype(f32) + dt_bias.astype(f32))
    a = -jnp.exp(a_log.astype(f32))
    y, ssm_new = _ssd(xs_h, dt, a,
                      bm.reshape(bn, L, SSM_GROUPS, D_STATE).astype(f32),
                      cm.reshape(bn, L, SSM_GROUPS, D_STATE).astype(f32),
                      ssm_h.astype(f32))
    y = (y + d_skip.astype(f32)[:, None] * xs_h).reshape(bn, L, C_WIDTH) * jax.nn.silu(z.astype(f32))
    out_c = (y * lax.rsqrt(jnp.mean(jnp.square(y), axis=-1, keepdims=True) + RMS_EPS) * gn_w.astype(f32)).astype(x.dtype)

    mix = jnp.einsum('ble,ed->bld', jnp.concatenate([out_a, out_b, out_c], axis=-1), w_out)
    x = _layer_norm(ALPHA * x + mix, ln1_g, ln1_b)
    hid = jnp.square(jax.nn.relu(jnp.einsum('bld,df->blf', x, w1)))
    x = _layer_norm(ALPHA * x + jnp.einsum('blf,fd->bld', hid, w2), ln2_g, ln2_b)
    return x, k_new, v_new, conv_new, ssm_new.astype(conv_buf.dtype), vn


def setup_inputs(seed: int = 0) -> dict:
    key = jax.random.key(seed)
    ks = jax.random.split(key, 32)
    nrm = jax.random.normal
    dt0 = jnp.exp(jax.random.uniform(ks[13], (DEPTH, SSM_HEADS), minval=math.log(1e-3), maxval=math.log(1e-1)))
    return {
        'x_prompt': nrm(ks[0], (BATCH, SEQ, D_MODEL), jnp.float32),
        'x_sample': nrm(ks[1], (DEC_BATCH, DEC_SEQ, D_MODEL), jnp.float32),
        'state_attn_k': nrm(ks[2], (DEPTH, DEC_BATCH, WINDOW, KV_HEADS, HEAD_DIM), jnp.float32),
        'state_attn_v': nrm(ks[3], (DEPTH, DEC_BATCH, WINDOW, KV_HEADS, HEAD_DIM), jnp.float32),
        'state_conv': nrm(ks[4], (DEPTH, DEC_BATCH, CONV_W - 1, CONV_DIM), jnp.float32),
        'state_ssm': 0.3 * nrm(ks[5], (DEPTH, DEC_BATCH, SSM_HEADS, SSM_HEAD_DIM, D_STATE), jnp.float32),
        'w_in': nrm(ks[6], (DEPTH, D_MODEL, D_IN), jnp.float32) * D_MODEL ** -0.5,
        'ln_v_g': 1.0 + 0.05 * nrm(ks[7], (DEPTH, A_WIDTH), jnp.float32),
        'ln_v_b': 0.02 * nrm(ks[8], (DEPTH, A_WIDTH), jnp.float32),
        'w_s': nrm(ks[9], (DEPTH, A_HEADS, CHUNK, CHUNK), jnp.float32) * CHUNK ** -0.5,
        'b_s': 1.0 + 0.05 * nrm(ks[10], (DEPTH, A_HEADS, CHUNK), jnp.float32),
        'sinks': 0.5 * nrm(ks[11], (DEPTH, ATT_HEADS), jnp.float32),
        'conv_w': nrm(ks[12], (DEPTH, CONV_W, CONV_DIM), jnp.float32) * CONV_W ** -0.5,
        'conv_b': 0.02 * nrm(ks[14], (DEPTH, CONV_DIM), jnp.float32),
        'dt_bias': dt0 + jnp.log(-jnp.expm1(-dt0)),
        'a_log': jnp.log(jax.random.uniform(ks[15], (DEPTH, SSM_HEADS), minval=1.0, maxval=16.0)),
        'd_skip': 1.0 + 0.05 * nrm(ks[16], (DEPTH, SSM_HEADS), jnp.float32),
        'gn_w': 1.0 + 0.05 * nrm(ks[17], (DEPTH, C_WIDTH), jnp.float32),
        'w_out': nrm(ks[18], (DEPTH, D_MIX, D_MODEL), jnp.float32) * (D_MIX ** -0.5) * BETA,
        'ln1_g': 1.0 + 0.05 * nrm(ks[19], (DEPTH, D_MODEL), jnp.float32),
        'ln1_b': 0.02 * nrm(ks[20], (DEPTH, D_MODEL), jnp.float32),
        'w1': nrm(ks[21], (DEPTH, D_MODEL, D_FF), jnp.float32) * D_MODEL ** -0.5,
        'w2': nrm(ks[22], (DEPTH, D_FF, D_MODEL), jnp.float32) * (D_FF ** -0.5) * BETA,
        'ln2_g': 1.0 + 0.05 * nrm(ks[23], (DEPTH, D_MODEL), jnp.float32),
        'ln2_b': 0.02 * nrm(ks[24], (DEPTH, D_MODEL), jnp.float32),
    }


def reference(x_prompt, x_sample, state_attn_k, state_attn_v, state_conv, state_ssm,
              w_in, ln_v_g, ln_v_b, w_s, b_s, sinks, conv_w, conv_b, dt_bias, a_log, d_skip,
              gn_w, w_out, ln1_g, ln1_b, w1, w2, ln2_g, ln2_b):
    yp, ys = x_prompt, x_sample
    kp, vp, cp, hp, ksm, vsm, csm, hsm, vns = [], [], [], [], [], [], [], [], []
    conv0 = jnp.zeros((x_prompt.shape[0], CONV_W - 1, CONV_DIM), x_prompt.dtype)
    ssm0 = jnp.zeros((x_prompt.shape[0], SSM_HEADS, SSM_HEAD_DIM, D_STATE), state_ssm.dtype)
    for l in range(DEPTH):
        wts = (w_in[l], ln_v_g[l], ln_v_b[l], w_s[l], b_s[l], sinks[l], conv_w[l], conv_b[l],
               dt_bias[l], a_log[l], d_skip[l], gn_w[l], w_out[l],
               ln1_g[l], ln1_b[l], w1[l], w2[l], ln2_g[l], ln2_b[l])
        yp, k_new, v_new, c_new, h_new, _ = _hybrid_layer(yp, None, None, conv0, ssm0, *wts)
        kp.append(k_new); vp.append(v_new); cp.append(c_new); hp.append(h_new)
        ys, k_new, v_new, c_new, h_new, vn_new = _hybrid_layer(
            ys, state_attn_k[l], state_attn_v[l], state_conv[l], state_ssm[l], *wts)
        ksm.append(k_new); vsm.append(v_new); csm.append(c_new); hsm.append(h_new); vns.append(vn_new)
    return (yp, ys,
            jnp.stack(kp), jnp.stack(vp), jnp.stack(cp), jnp.stack(hp),
            jnp.stack(ksm), jnp.stack(vsm), jnp.stack(csm), jnp.stack(hsm), jnp.stack(vns))
```

```python
import functools

import jax
import jax.numpy as jnp
from jax import lax
from jax.experimental import pallas as pl
from jax.experimental.pallas import tpu as pltpu

F32 = jnp.float32
BF16 = jnp.bfloat16

D_MODEL = 1024
DEPTH = 4
HEAD_DIM = 64
A_WIDTH = 256
B_WIDTH = 256
KV_WIDTH = 128
C_WIDTH = 512
CONV_DIM = 1024
CONV_W = 4
SSM_HEADS = 8
D_STATE = 128
D_FF = 4096
CHUNK = 128
ALPHA = (2 * DEPTH) ** 0.25
LN_EPS = 1e-5
RMS_EPS = 1e-6
ATT_SCALE = HEAD_DIM ** -0.5
NEG = -1e30

OFF_AU, OFF_AV, OFF_Q, OFF_K, OFF_V, OFF_Z = 0, 256, 512, 768, 896, 1024
OFF_XBC = 1536
OFF_DT = 2560
D_IN_PAD = 2688
LANES = 128
SUBLANES = 8
VMEM_LIMIT = 56 * 1024 * 1024

NT_DIMS = (((1,), (1,)), ((), ()))


def _dot(a, b):
    return jnp.dot(a, b, preferred_element_type=F32)


def _dot_nt(a, b):
    return lax.dot_general(a, b, NT_DIMS, preferred_element_type=F32)


def _layer_norm(x, g, b):
    mu = jnp.mean(x, axis=-1, keepdims=True)
    xc = x - mu
    var = jnp.mean(xc * xc, axis=-1, keepdims=True)
    return xc * lax.rsqrt(var + LN_EPS) * g + b


def _silu(x):
    return x * jax.nn.sigmoid(x)


def _split3(x):
    hi = x.astype(BF16)
    r = x - hi.astype(F32)
    mid = r.astype(BF16)
    lo = (r - mid.astype(F32)).astype(BF16)
    return hi, mid, lo


def _dot_exact_lhs(m_bf16, x):
    hi, mid, lo = _split3(x)
    return _dot(m_bf16, hi) + _dot(m_bf16, mid) + _dot(m_bf16, lo)


def _dot_exact_rhs(x, m_bf16):
    hi, mid, lo = _split3(x)
    return _dot(hi, m_bf16) + _dot(mid, m_bf16) + _dot(lo, m_bf16)


def _gated_rmsnorm(y, xs, z, dskip, gnw):
    y2 = (y + dskip * xs) * _silu(z)
    ms = jnp.mean(y2 * y2, axis=-1, keepdims=True)
    return y2 * lax.rsqrt(ms + RMS_EPS) * gnw


def _mixer_kernel(sinks_ref, x_ref, w_in_ref, lnvg_ref, lnvb_ref, ws_ref, bs_ref,
                  convw_ref, convb_ref, dtb_ref, alog_ref, dskip_ref, gnw_ref,
                  w_out_ref, ln1g_ref, ln1b_ref,
                  x1_ref, klast_ref, vlast_ref, convnew_ref, ssm_ref,
                  proj_ref, xbc_ref, act_ref, dtr_ref, mix_ref, kprev_ref, vprev_ref,
                  hT_ref, wcat_ref, *, tl):
    t = pl.program_id(1)
    nt = pl.num_programs(1)
    n_chunks = tl // CHUNK

    row_i = lax.broadcasted_iota(jnp.int32, (CHUNK, CHUNK), 0)
    col_i = lax.broadcasted_iota(jnp.int32, (CHUNK, CHUNK), 1)
    causal = col_i <= row_i
    lane_lo = col_i < HEAD_DIM
    lane_lo_row = lax.broadcasted_iota(jnp.int32, (1, LANES), 1) < HEAD_DIM

    @pl.when(t == 0)
    def _init():
        kprev_ref[...] = jnp.zeros_like(kprev_ref)
        vprev_ref[...] = jnp.zeros_like(vprev_ref)
        hT_ref[...] = jnp.zeros_like(hT_ref)
        xbc_ref[0:SUBLANES, :] = jnp.zeros((SUBLANES, CONV_DIM), F32)
        for h in range(4):
            wcat_ref[:, h * CHUNK:(h + 1) * CHUNK] = jnp.where(causal, ws_ref[h], 0.0).astype(BF16)

    xb = x_ref[0].astype(BF16)
    proj_ref[...] = _dot(xb, w_in_ref[:, 0:OFF_XBC])
    xbc_ref[SUBLANES:SUBLANES + tl, :] = _dot(xb, w_in_ref[:, OFF_XBC:OFF_DT])
    dtr_ref[...] = _dot(xb, w_in_ref[:, OFF_DT:D_IN_PAD])

    acc = convb_ref[...]
    for i in range(CONV_W):
        start = SUBLANES - (CONV_W - 1) + i
        acc = acc + xbc_ref[start:start + tl, :] * convw_ref[i:i + 1, :]
    act_ref[...] = _silu(acc)

    tril_b = jnp.where(causal, 1.0, 0.0).astype(BF16)
    a_row = -jnp.exp(alog_ref[...])
    lane256 = lax.broadcasted_iota(jnp.int32, (CHUNK, A_WIDTH), 1)
    r256 = lax.broadcasted_iota(jnp.int32, (2 * CHUNK, 2 * CHUNK), 0)
    c256 = lax.broadcasted_iota(jnp.int32, (2 * CHUNK, 2 * CHUNK), 1)
    t256 = jnp.bitwise_and(r256, CHUNK - 1)
    win_mask = (c256 > t256) & (c256 <= t256 + CHUNK)
    rcol = lax.broadcasted_iota(jnp.int32, (2 * CHUNK, 1), 0)
    sink_a = jnp.where(rcol < CHUNK, sinks_ref[0], sinks_ref[3])
    sink_b = jnp.where(rcol < CHUNK, sinks_ref[1], sinks_ref[2])

    def chunk(c, carry):
        r0 = pl.multiple_of(c * CHUNK, CHUNK)
        rows = pl.ds(r0, CHUNK)

        u = jax.nn.gelu(proj_ref[rows, OFF_AU:OFF_AU + A_WIDTH])
        vn = _layer_norm(jax.nn.gelu(proj_ref[rows, OFF_AV:OFF_AV + A_WIDTH]),
                         lnvg_ref[...], lnvb_ref[...])
        vblk = jnp.concatenate(
            [jnp.where((lane256 >= h * HEAD_DIM) & (lane256 < (h + 1) * HEAD_DIM), vn, 0.0)
             for h in range(4)], axis=0).astype(BF16)
        mix_a = _dot(wcat_ref[...], vblk) + bs_ref[...]
        mix_ref[rows, 0:A_WIDTH] = (u * mix_a).astype(BF16)

        q01 = proj_ref[rows, OFF_Q:OFF_Q + LANES]
        q23 = proj_ref[rows, OFF_Q + LANES:OFF_Q + 2 * LANES]
        kc = proj_ref[rows, OFF_K:OFF_K + KV_WIDTH]
        vc = proj_ref[rows, OFF_V:OFF_V + KV_WIDTH]
        k2 = jnp.concatenate([kprev_ref[...], kc], axis=0)
        v2 = jnp.concatenate([vprev_ref[...], vc], axis=0)
        k2b = k2.astype(BF16)
        v2b = v2.astype(BF16)
        k2rb = pltpu.roll(k2, HEAD_DIM, 1).astype(BF16)
        v2rb = pltpu.roll(v2, HEAD_DIM, 1).astype(BF16)
        q_a = jnp.concatenate([jnp.where(lane_lo, q01, 0.0), jnp.where(lane_lo, 0.0, q23)], axis=0)
        q_b = jnp.concatenate([jnp.where(lane_lo, 0.0, q01), jnp.where(lane_lo, q23, 0.0)], axis=0)
        first = jnp.logical_and(t == 0, c == 0)
        vis = win_mask & (c256 >= jnp.where(first, CHUNK, 0))

        def attend(qm, kk, vv, sink):
            s = jnp.where(vis, _dot_nt(qm.astype(BF16), kk) * ATT_SCALE, NEG)
            m = jnp.maximum(jnp.max(s, axis=-1, keepdims=True), sink)
            p = jnp.exp(s - m)
            den = jnp.sum(p, axis=-1, keepdims=True) + jnp.exp(sink - m)
            return _dot((p * (1.0 / den)).astype(BF16), vv)

        o_a = attend(q_a, k2b, v2b, sink_a)
        o_b = attend(q_b, k2rb, v2rb, sink_b)
        mix_ref[rows, A_WIDTH:A_WIDTH + LANES] = jnp.where(
            lane_lo, o_a[0:CHUNK], o_b[0:CHUNK]).astype(BF16)
        mix_ref[rows, A_WIDTH + LANES:A_WIDTH + 2 * LANES] = jnp.where(
            lane_lo, o_b[CHUNK:], o_a[CHUNK:]).astype(BF16)
        kprev_ref[...] = kc
        vprev_ref[...] = vc

        xs = act_ref[rows, 0:C_WIDTH]
        bm = act_ref[rows, C_WIDTH:C_WIDTH + 2 * D_STATE]
        cm = act_ref[rows, C_WIDTH + 2 * D_STATE:CONV_DIM]
        z = proj_ref[rows, OFF_Z:OFF_Z + C_WIDTH]
        dt = jax.nn.softplus(dtr_ref[rows, :] + dtb_ref[...])
        cum = _dot_exact_lhs(tril_b, dt * a_row)
        cum_row = cum.T[0:SSM_HEADS]
        dt_row = dt.T[0:SSM_HEADS]
        toend_row = jnp.exp(cum_row[:, CHUNK - 1:CHUNK] - cum_row) * dt_row
        ecl = jnp.exp(cum[CHUNK - 1:CHUNK, :])
        ys = []
        for g in range(2):
            bg = bm[:, g * D_STATE:(g + 1) * D_STATE]
            cg = cm[:, g * D_STATE:(g + 1) * D_STATE]
            cb = _dot_nt(cg.astype(BF16), bg.astype(BF16))
            bgt = bg.T
            for pr in range(2):
                pair = g * 2 + pr
                psl = slice(pair * LANES, (pair + 1) * LANES)
                lhs, btw = [], []
                for hh in range(2):
                    h = pair * 2 + hh
                    cc = jnp.broadcast_to(cum[:, h:h + 1], (CHUNK, CHUNK))
                    dec = jnp.exp(jnp.where(causal, cc - cum_row[h:h + 1, :], NEG))
                    m_h = cb * dec * dt_row[h:h + 1, :]
                    ce_h = cg * jnp.exp(cc)
                    lhs.append(jnp.concatenate([m_h, ce_h], axis=1).astype(BF16))
                    btw.append((bgt * toend_row[h:h + 1, :]).astype(BF16))
                xs_p = xs[:, psl]
                ht_p = hT_ref[:, psl]
                rhs = jnp.concatenate([xs_p, ht_p], axis=0).astype(BF16)
                yy = _dot(jnp.concatenate(lhs, axis=0), rhs)
                ys.append(jnp.where(lane_lo, yy[0:CHUNK], yy[CHUNK:]))
                st = _dot(jnp.concatenate(btw, axis=0), xs_p.astype(BF16))
                ecl_p = jnp.where(lane_lo_row,
                                  jnp.broadcast_to(ecl[:, 2 * pair:2 * pair + 1], (1, LANES)),
                                  jnp.broadcast_to(ecl[:, 2 * pair + 1:2 * pair + 2], (1, LANES)))
                hT_ref[:, psl] = ht_p * ecl_p + jnp.where(lane_lo, st[0:CHUNK], st[CHUNK:])
        y = jnp.concatenate(ys, axis=1)
        out_c = _gated_rmsnorm(y, xs, z, dskip_ref[...], gnw_ref[...])
        mix_ref[rows, A_WIDTH + B_WIDTH:D_MODEL] = out_c.astype(BF16)
        return carry

    lax.fori_loop(0, n_chunks, chunk, 0)

    mix_o = _dot(mix_ref[...], w_out_ref[...])
    x1_ref[0] = _layer_norm(ALPHA * x_ref[0] + mix_o, ln1g_ref[...], ln1b_ref[...])

    tail = xbc_ref[SUBLANES + tl - (CONV_W - 1):SUBLANES + tl, :]
    xbc_ref[SUBLANES - (CONV_W - 1):SUBLANES, :] = tail

    @pl.when(t == nt - 1)
    def _final():
        klast_ref[0] = kprev_ref[...]
        vlast_ref[0] = vprev_ref[...]
        convnew_ref[0] = tail
        ssm_ref[0] = hT_ref[...].T


def _full(shape):
    n = len(shape)
    return pl.BlockSpec(shape, lambda *_: (0,) * n)


def _mixer_call(x, sinks, w_in, lnvg, lnvb, ws, bs_exp, convw, convb, dtb, alog, dskip, gnw,
                w_out, ln1g, ln1b, *, tl):
    bsz, seq, _ = x.shape
    nt = seq // tl
    kern = functools.partial(_mixer_kernel, tl=tl)
    in_specs = [
        pl.BlockSpec(memory_space=pltpu.SMEM),
        pl.BlockSpec((1, tl, D_MODEL), lambda b, t: (b, t, 0)),
        _full((D_MODEL, D_IN_PAD)),
        _full((1, A_WIDTH)), _full((1, A_WIDTH)),
        _full((4, CHUNK, CHUNK)), _full((CHUNK, A_WIDTH)),
        _full((CONV_W, CONV_DIM)), _full((1, CONV_DIM)),
        _full((1, LANES)), _full((1, LANES)),
        _full((1, C_WIDTH)), _full((1, C_WIDTH)),
        _full((D_MODEL, D_MODEL)),
        _full((1, D_MODEL)), _full((1, D_MODEL)),
    ]
    out_shape = (
        jax.ShapeDtypeStruct((bsz, seq, D_MODEL), F32),
        jax.ShapeDtypeStruct((bsz, CHUNK, KV_WIDTH), F32),
        jax.ShapeDtypeStruct((bsz, CHUNK, KV_WIDTH), F32),
        jax.ShapeDtypeStruct((bsz, CONV_W - 1, CONV_DIM), F32),
        jax.ShapeDtypeStruct((bsz, C_WIDTH, D_STATE), F32),
    )
    out_specs = (
        pl.BlockSpec((1, tl, D_MODEL), lambda b, t: (b, t, 0)),
        pl.BlockSpec((1, CHUNK, KV_WIDTH), lambda b, t: (b, 0, 0)),
        pl.BlockSpec((1, CHUNK, KV_WIDTH), lambda b, t: (b, 0, 0)),
        pl.BlockSpec((1, CONV_W - 1, CONV_DIM), lambda b, t: (b, 0, 0)),
        pl.BlockSpec((1, C_WIDTH, D_STATE), lambda b, t: (b, 0, 0)),
    )
    scratch = [
        pltpu.VMEM((tl, OFF_XBC), F32),
        pltpu.VMEM((SUBLANES + tl, CONV_DIM), F32),
        pltpu.VMEM((tl, CONV_DIM), F32),
        pltpu.VMEM((tl, LANES), F32),
        pltpu.VMEM((tl, D_MODEL), BF16),
        pltpu.VMEM((CHUNK, KV_WIDTH), F32),
        pltpu.VMEM((CHUNK, KV_WIDTH), F32),
        pltpu.VMEM((D_STATE, C_WIDTH), F32),
        pltpu.VMEM((CHUNK, 4 * CHUNK), BF16),
    ]
    return pl.pallas_call(
        kern, out_shape=out_shape, grid=(bsz, nt), in_specs=in_specs, out_specs=out_specs,
        scratch_shapes=scratch, name="prompt_mixer",
        compiler_params=pltpu.CompilerParams(
            dimension_semantics=("arbitrary", "arbitrary"), vmem_limit_bytes=VMEM_LIMIT),
    )(sinks, x, w_in, lnvg, lnvb, ws, bs_exp, convw, convb, dtb, alog, dskip, gnw, w_out, ln1g, ln1b)


FF_CHUNK = 1024


def _ffn_kernel(x_ref, w1_ref, w2_ref, g_ref, b_ref, o_ref):
    x = x_ref[...]
    xb = x.astype(BF16)
    acc = jnp.zeros(x.shape, F32)
    for c in range(D_FF // FF_CHUNK):
        sl = slice(c * FF_CHUNK, (c + 1) * FF_CHUNK)
        h = jnp.maximum(_dot(xb, w1_ref[:, sl]), 0.0)
        acc = acc + _dot((h * h).astype(BF16), w2_ref[sl, :])
    o_ref[...] = _layer_norm(ALPHA * x + acc, g_ref[...], b_ref[...])


def _ffn_call(x2d, w1, w2, g, b, *, tm):
    rows = x2d.shape[0]
    return pl.pallas_call(
        _ffn_kernel, out_shape=jax.ShapeDtypeStruct((rows, D_MODEL), F32),
        grid=(rows // tm,),
        in_specs=[pl.BlockSpec((tm, D_MODEL), lambda i: (i, 0)),
                  _full((D_MODEL, D_FF)), _full((D_FF, D_MODEL)),
                  _full((1, D_MODEL)), _full((1, D_MODEL))],
        out_specs=pl.BlockSpec((tm, D_MODEL), lambda i: (i, 0)),
        name="ffn",
        compiler_params=pltpu.CompilerParams(
            dimension_semantics=("arbitrary",), vmem_limit_bytes=VMEM_LIMIT),
    )(x2d, w1, w2, g, b)


def _s1_kernel(x_ref, w_in_ref, lnvg_ref, lnvb_ref, ws0_ref, bs0_ref, convw_ref, convb_ref,
               cst_ref, dtb_ref, alog_ref, e_ref,
               vn_ref, outa_ref, q_ref, k_ref, v_ref, z_ref, cnew_ref, xs_ref, bm_ref,
               xT_ref, cT_ref, dec_ref):
    xb = x_ref[...].astype(BF16)
    proj = _dot(xb, w_in_ref[...])
    u = jax.nn.gelu(proj[:, OFF_AU:OFF_AU + A_WIDTH])
    vn = _layer_norm(jax.nn.gelu(proj[:, OFF_AV:OFF_AV + A_WIDTH]), lnvg_ref[...], lnvb_ref[...])
    vn_ref[...] = vn
    outa_ref[...] = u * (ws0_ref[...] * vn + bs0_ref[...])
    q_ref[...] = proj[:, OFF_Q:OFF_Q + B_WIDTH]
    k_ref[...] = proj[:, OFF_K:OFF_K + KV_WIDTH]
    v_ref[...] = proj[:, OFF_V:OFF_V + KV_WIDTH]
    z_ref[...] = proj[:, OFF_Z:OFF_Z + C_WIDTH]
    xbc = proj[:, OFF_XBC:OFF_DT]
    acc = convb_ref[...] + xbc * convw_ref[CONV_W - 1:CONV_W, :]
    for i in range(CONV_W - 1):
        acc = acc + cst_ref[i] * convw_ref[i:i + 1, :]
    cnew_ref[0] = cst_ref[1]
    cnew_ref[1] = cst_ref[2]
    cnew_ref[2] = xbc
    act = _silu(acc)
    xs = act[:, 0:C_WIDTH]
    xs_ref[...] = xs
    bm_ref[...] = act[:, C_WIDTH:C_WIDTH + 2 * D_STATE]
    cT_ref[...] = act[:, C_WIDTH + 2 * D_STATE:CONV_DIM].T
    dt = jax.nn.softplus(proj[:, OFF_DT:D_IN_PAD] + dtb_ref[...])
    dec_ref[...] = jnp.exp(dt * (-jnp.exp(alog_ref[...])))
    dt_exp = _dot_exact_rhs(dt, e_ref[...])
    xT_ref[...] = (xs * dt_exp).T.astype(BF16)


def _s1_call(x, w_in, lnvg, lnvb, ws0, bs0, convw, convb, cst, dtb, alog, emat):
    n = x.shape[0]
    out_shape = (
        jax.ShapeDtypeStruct((n, A_WIDTH), F32), jax.ShapeDtypeStruct((n, A_WIDTH), F32),
        jax.ShapeDtypeStruct((n, B_WIDTH), F32), jax.ShapeDtypeStruct((n, KV_WIDTH), F32),
        jax.ShapeDtypeStruct((n, KV_WIDTH), F32), jax.ShapeDtypeStruct((n, C_WIDTH), F32),
        jax.ShapeDtypeStruct((CONV_W - 1, n, CONV_DIM), F32),
        jax.ShapeDtypeStruct((n, C_WIDTH), F32), jax.ShapeDtypeStruct((n, 2 * D_STATE), F32),
        jax.ShapeDtypeStruct((C_WIDTH, n), BF16), jax.ShapeDtypeStruct((2 * D_STATE, n), F32),
        jax.ShapeDtypeStruct((n, LANES), F32),
    )
    return pl.pallas_call(
        _s1_kernel, out_shape=out_shape, name="sample_pre",
        compiler_params=pltpu.CompilerParams(vmem_limit_bytes=VMEM_LIMIT),
    )(x, w_in, lnvg, lnvb, ws0, bs0, convw, convb, cst, dtb, alog, emat)


S2_TB = SUBLANES


def _s2_kernel(sinks_ref, q_ref, kn_ref, vn_ref, kst_ref, vst_ref, h_ref, xT_ref, bm_ref,
               cT_ref, dec_ref, ob_ref, hnew_ref, yT_ref):
    i = pl.program_id(0)
    tb = S2_TB
    lane = lax.broadcasted_iota(jnp.int32, (tb, LANES), 1)
    lo = lane < HEAD_DIM

    q01 = q_ref[:, 0:LANES]
    q23 = q_ref[:, LANES:2 * LANES]
    qh = jnp.concatenate([
        jnp.where(lo, q01, 0.0),
        jnp.where(lo, pltpu.roll(q01, HEAD_DIM, 1), 0.0),
        jnp.where(lo, 0.0, pltpu.roll(q23, HEAD_DIM, 1)),
        jnp.where(lo, 0.0, q23)], axis=0)
    qhb = qh.astype(BF16)
    rowi = lax.broadcasted_iota(jnp.int32, (4 * tb, LANES), 0)
    coli = lax.broadcasted_iota(jnp.int32, (4 * tb, LANES), 1)
    tok = jnp.bitwise_and(rowi, tb - 1)
    s = jnp.zeros((4 * tb, LANES), F32)
    for j in range(tb):
        sj = _dot_nt(qhb, kst_ref[j].astype(BF16))
        s = jnp.where(tok == j, sj, s)
    s = jnp.where(coli >= 1, s * ATT_SCALE, NEG)
    kn4 = jnp.concatenate([kn_ref[...]] * 4, axis=0)
    vn4 = jnp.concatenate([vn_ref[...]] * 4, axis=0)
    s_new = jnp.sum(qh * kn4, axis=-1, keepdims=True) * ATT_SCALE
    head = jnp.right_shift(rowi[:, 0:1], 3)
    sink = jnp.where(head == 0, sinks_ref[0],
                     jnp.where(head == 1, sinks_ref[1],
                               jnp.where(head == 2, sinks_ref[2], sinks_ref[3])))
    m = jnp.maximum(jnp.maximum(jnp.max(s, axis=-1, keepdims=True), s_new), sink)
    p = jnp.exp(s - m)
    p_new = jnp.exp(s_new - m)
    inv = 1.0 / (jnp.sum(p, axis=-1, keepdims=True) + p_new + jnp.exp(sink - m))
    p = p * inv
    o = (p_new * inv) * vn4
    for j in range(tb):
        pj = jnp.where(tok == j, p, 0.0).astype(BF16)
        o = o + _dot(pj, vst_ref[j].astype(BF16))
    ob_ref[:, 0:LANES] = jnp.where(lo, o[0:tb], pltpu.roll(o[tb:2 * tb], HEAD_DIM, 1))
    ob_ref[:, LANES:2 * LANES] = jnp.where(lo, pltpu.roll(o[2 * tb:3 * tb], HEAD_DIM, 1), o[3 * tb:])

    @pl.when(i == 0)
    def _():
        yT_ref[...] = jnp.zeros_like(yT_ref)

    r128 = lax.broadcasted_iota(jnp.int32, (LANES, LANES), 0)
    c128 = lax.broadcasted_iota(jnp.int32, (LANES, LANES), 1)
    hp_g = C_WIDTH // 2
    for j in range(tb):
        b = i * tb + j
        brow = bm_ref[pl.ds(b, 1), :]
        drow = dec_ref[pl.ds(b, 1), :]
        for g in range(2):
            dm = jnp.where(r128 == b,
                           jnp.broadcast_to(brow[:, g * D_STATE:(g + 1) * D_STATE], (LANES, D_STATE)),
                           0.0).astype(BF16)
            upd = _dot(xT_ref[g * hp_g:(g + 1) * hp_g, :], dm)
            parts = []
            for hh in range(4):
                h = g * 4 + hh
                rs = slice(h * HEAD_DIM, (h + 1) * HEAD_DIM)
                hn = (h_ref[j, rs, :] * jnp.broadcast_to(drow[:, h:h + 1], (HEAD_DIM, D_STATE))
                      + upd[hh * HEAD_DIM:(hh + 1) * HEAD_DIM])
                hnew_ref[j, rs, :] = hn
                parts.append(hn)
            cmm = jnp.where(c128 == b, cT_ref[g * D_STATE:(g + 1) * D_STATE, :], 0.0).astype(BF16)
            yT_ref[g * hp_g:(g + 1) * hp_g, :] += _dot(jnp.concatenate(parts, axis=0).astype(BF16), cmm)


def _s2_call(sinks, q, kn, vn, kst, vst, hst, xT, bm, cT, dec):
    n = q.shape[0]
    tb = S2_TB
    in_specs = [
        pl.BlockSpec(memory_space=pltpu.SMEM),
        pl.BlockSpec((tb, B_WIDTH), lambda i: (i, 0)),
        pl.BlockSpec((tb, KV_WIDTH), lambda i: (i, 0)),
        pl.BlockSpec((tb, KV_WIDTH), lambda i: (i, 0)),
        pl.BlockSpec((tb, CHUNK, KV_WIDTH), lambda i: (i, 0, 0)),
        pl.BlockSpec((tb, CHUNK, KV_WIDTH), lambda i: (i, 0, 0)),
        pl.BlockSpec((tb, C_WIDTH, D_STATE), lambda i: (i, 0, 0)),
        _full((C_WIDTH, n)), _full((n, 2 * D_STATE)), _full((2 * D_STATE, n)), _full((n, LANES)),
    ]
    out_shape = (
        jax.ShapeDtypeStruct((n, B_WIDTH), F32),
        jax.ShapeDtypeStruct((n, C_WIDTH, D_STATE), F32),
        jax.ShapeDtypeStruct((C_WIDTH, n), F32),
    )
    out_specs = (
        pl.BlockSpec((tb, B_WIDTH), lambda i: (i, 0)),
        pl.BlockSpec((tb, C_WIDTH, D_STATE), lambda i: (i, 0, 0)),
        _full((C_WIDTH, n)),
    )
    return pl.pallas_call(
        _s2_kernel, out_shape=out_shape, grid=(n // tb,), in_specs=in_specs, out_specs=out_specs,
        name="sample_state",
        compiler_params=pltpu.CompilerParams(
            dimension_semantics=("arbitrary",), vmem_limit_bytes=VMEM_LIMIT),
    )(sinks, q, kn, vn, kst, vst, hst, xT, bm, cT, dec)


def _s3_kernel(yT_ref, xs_ref, z_ref, dskip_ref, gnw_ref, outa_ref, outb_ref, x_ref, w_out_ref,
               g_ref, b_ref, x1_ref):
    out_c = _gated_rmsnorm(yT_ref[...].T, xs_ref[...], z_ref[...], dskip_ref[...], gnw_ref[...])
    mix = jnp.concatenate([outa_ref[...], outb_ref[...], out_c], axis=1).astype(BF16)
    x1_ref[...] = _layer_norm(ALPHA * x_ref[...] + _dot(mix, w_out_ref[...]), g_ref[...], b_ref[...])


def _s3_call(yT, xs, z, dskip, gnw, outa, outb, x, w_out, g, b):
    return pl.pallas_call(
        _s3_kernel, out_shape=jax.ShapeDtypeStruct(x.shape, F32), name="sample_post",
        compiler_params=pltpu.CompilerParams(vmem_limit_bytes=VMEM_LIMIT),
    )(yT, xs, z, dskip, gnw, outa, outb, x, w_out, g, b)


PROMPT_TL = 512
PROMPT_TM = 512


def kernel(x_prompt, x_sample, state_attn_k, state_attn_v, state_conv, state_ssm, w_in, ln_v_g,
           ln_v_b, w_s, b_s, sinks, conv_w, conv_b, dt_bias, a_log, d_skip, gn_w, w_out, ln1_g,
           ln1_b, w1, w2, ln2_g, ln2_b):
    bsz, seq, _ = x_prompt.shape
    n_s = x_sample.shape[0]
    d_in = w_in.shape[-1]

    w_in_b = jnp.pad(w_in, ((0, 0), (0, 0), (0, D_IN_PAD - d_in))).astype(BF16)
    w_out_b = w_out.astype(BF16)
    w1_b = w1.astype(BF16)
    w2_b = w2.astype(BF16)
    pad_h = ((0, 0), (0, LANES - SSM_HEADS))
    dtb_p = jnp.pad(dt_bias, pad_h)[:, None, :]
    alog_p = jnp.pad(a_log, pad_h)[:, None, :]
    dskip_e = jnp.repeat(d_skip, HEAD_DIM, axis=-1)[:, None, :]
    bs_e = jnp.repeat(jnp.swapaxes(b_s, 1, 2), HEAD_DIM, axis=-1)
    ws0_e = jnp.repeat(w_s[:, :, 0, 0], HEAD_DIM, axis=-1)[:, None, :]
    bs0_e = jnp.repeat(b_s[:, :, 0], HEAD_DIM, axis=-1)[:, None, :]
    emat = (lax.broadcasted_iota(jnp.int32, (LANES, C_WIDTH), 0)
            == lax.broadcasted_iota(jnp.int32, (LANES, C_WIDTH), 1) // HEAD_DIM).astype(BF16)
    row = lambda a: a[:, None, :]
    lnvg, lnvb, convb, gnw = row(ln_v_g), row(ln_v_b), row(conv_b), row(gn_w)
    ln1g, ln1b, ln2g, ln2b = row(ln1_g), row(ln1_b), row(ln2_g), row(ln2_b)
    kst = state_attn_k.reshape(DEPTH, n_s, CHUNK, KV_WIDTH)
    vst = state_attn_v.reshape(DEPTH, n_s, CHUNK, KV_WIDTH)
    hst = state_ssm.reshape(DEPTH, n_s, C_WIDTH, D_STATE)
    cst = jnp.swapaxes(state_conv, 1, 2)

    yp = x_prompt
    ys = x_sample.reshape(n_s, D_MODEL)
    kp, vp, cp, hp, ksm, vsm, csm, hsm, vns = [], [], [], [], [], [], [], [], []
    for l in range(DEPTH):
        x1, k_l, v_l, c_l, h_l = _mixer_call(
            yp, sinks[l], w_in_b[l], lnvg[l], lnvb[l], w_s[l], bs_e[l], conv_w[l], convb[l],
            dtb_p[l], alog_p[l], dskip_e[l], gnw[l], w_out_b[l], ln1g[l], ln1b[l], tl=PROMPT_TL)
        yp = _ffn_call(x1.reshape(bsz * seq, D_MODEL), w1_b[l], w2_b[l], ln2g[l], ln2b[l],
                       tm=PROMPT_TM).reshape(bsz, seq, D_MODEL)
        kp.append(k_l); vp.append(v_l); cp.append(c_l); hp.append(h_l)

        (vn_s, outa, q_s, k_s, v_s, z_s, cnew, xs_s, bm_s, xT, cT, dec) = _s1_call(
            ys, w_in_b[l], lnvg[l], lnvb[l], ws0_e[l], bs0_e[l], conv_w[l], convb[l], cst[l],
            dtb_p[l], alog_p[l], emat)
        outb, hnew, yT = _s2_call(sinks[l], q_s, k_s, v_s, kst[l], vst[l], hst[l], xT, bm_s, cT, dec)
        x1s = _s3_call(yT, xs_s, z_s, dskip_e[l], gnw[l], outa, outb, ys, w_out_b[l], ln1g[l], ln1b[l])
        ys = _ffn_call(x1s, w1_b[l], w2_b[l], ln2g[l], ln2b[l], tm=n_s)
        ksm.append(k_s); vsm.append(v_s); csm.append(cnew); hsm.append(hnew); vns.append(vn_s)

    kv_p = (DEPTH, bsz, CHUNK, 2, HEAD_DIM)
    kv_s = (DEPTH, n_s, 1, 2, HEAD_DIM)
    ssm_shape = (SSM_HEADS, HEAD_DIM, D_STATE)
    return (yp, ys.reshape(n_s, 1, D_MODEL),
            jnp.stack(kp).reshape(kv_p), jnp.stack(vp).reshape(kv_p),
            jnp.stack(cp), jnp.stack(hp).reshape((DEPTH, bsz) + ssm_shape),
            jnp.stack(ksm).reshape(kv_s), jnp.stack(vsm).reshape(kv_s),
            jnp.swapaxes(jnp.stack(csm), 1, 2),
            jnp.stack(hsm).reshape((DEPTH, n_s) + ssm_shape),
            jnp.stack(vns).reshape(DEPTH, n_s, 1, A_WIDTH))
```

```python
import functools

import jax
import jax.numpy as jnp
from jax import lax
from jax.experimental import pallas as pl
from jax.experimental.pallas import tpu as pltpu

F32 = jnp.float32
BF16 = jnp.bfloat16

D_MODEL = 1024
DEPTH = 4
HEAD_DIM = 64
A_WIDTH = 256
B_WIDTH = 256
KV_WIDTH = 128
C_WIDTH = 512
CONV_DIM = 1024
CONV_W = 4
SSM_HEADS = 8
D_STATE = 128
D_FF = 4096
CHUNK = 128
ALPHA = (2 * DEPTH) ** 0.25
LN_EPS = 1e-5
RMS_EPS = 1e-6
ATT_SCALE = HEAD_DIM ** -0.5
NEG = -1e30

OFF_AU, OFF_AV, OFF_Q, OFF_K, OFF_V, OFF_Z = 0, 256, 512, 768, 896, 1024
OFF_XBC = 1536
OFF_DT = 2560
D_IN_PAD = 2688
LANES = 128
SUBLANES = 8
VMEM_LIMIT = 56 * 1024 * 1024

NT_DIMS = (((1,), (1,)), ((), ()))


def _dot(a, b):
    return jnp.dot(a, b, preferred_element_type=F32)


def _dot_nt(a, b):
    return lax.dot_general(a, b, NT_DIMS, preferred_element_type=F32)


def _layer_norm(x, g, b):
    mu = jnp.mean(x, axis=-1, keepdims=True)
    xc = x - mu
    var = jnp.mean(xc * xc, axis=-1, keepdims=True)
    return xc * lax.rsqrt(var + LN_EPS) * g + b


def _silu(x):
    return x * jax.nn.sigmoid(x)


def _split3(x):
    hi = x.astype(BF16)
    r = x - hi.astype(F32)
    mid = r.astype(BF16)
    lo = (r - mid.astype(F32)).astype(BF16)
    return hi, mid, lo


def _dot_exact_lhs(m_bf16, x):
    hi, mid, lo = _split3(x)
    return _dot(m_bf16, hi) + _dot(m_bf16, mid) + _dot(m_bf16, lo)


def _dot_exact_rhs(x, m_bf16):
    hi, mid, lo = _split3(x)
    return _dot(hi, m_bf16) + _dot(mid, m_bf16) + _dot(lo, m_bf16)


def _gated_rmsnorm(y, xs, z, dskip, gnw):
    y2 = (y + dskip * xs) * _silu(z)
    ms = jnp.mean(y2 * y2, axis=-1, keepdims=True)
    return y2 * lax.rsqrt(ms + RMS_EPS) * gnw


def _mixer_kernel(sinks_ref, x_ref, w_in_ref, lnvg_ref, lnvb_ref, ws_ref, bs_ref,
                  convw_ref, convb_ref, dtb_ref, alog_ref, dskip_ref, gnw_ref,
                  w_out_ref, ln1g_ref, ln1b_ref,
                  x1_ref, klast_ref, vlast_ref, convnew_ref, ssm_ref,
                  proj_ref, xbc_ref, act_ref, dtr_ref, mix_ref, kprev_ref, vprev_ref,
                  hT_ref, wcat_ref, *, tl, layer):
    t = pl.program_id(1)
    nt = pl.num_programs(1)
    n_chunks = tl // CHUNK

    row_i = lax.broadcasted_iota(jnp.int32, (CHUNK, CHUNK), 0)
    col_i = lax.broadcasted_iota(jnp.int32, (CHUNK, CHUNK), 1)
    causal = col_i <= row_i
    lane_lo = col_i < HEAD_DIM
    lane_lo_row = lax.broadcasted_iota(jnp.int32, (1, LANES), 1) < HEAD_DIM

    @pl.when(t == 0)
    def _init():
        kprev_ref[...] = jnp.zeros_like(kprev_ref)
        vprev_ref[...] = jnp.zeros_like(vprev_ref)
        hT_ref[...] = jnp.zeros_like(hT_ref)
        xbc_ref[0:SUBLANES, :] = jnp.zeros((SUBLANES, CONV_DIM), F32)
        for h in range(4):
            wcat_ref[:, h * CHUNK:(h + 1) * CHUNK] = jnp.where(causal, ws_ref[h], 0.0).astype(BF16)

    xb = x_ref[0].astype(BF16)
    proj_ref[...] = _dot(xb, w_in_ref[:, 0:OFF_XBC])
    xbc_ref[SUBLANES:SUBLANES + tl, :] = _dot(xb, w_in_ref[:, OFF_XBC:OFF_DT])
    dtr_ref[...] = _dot(xb, w_in_ref[:, OFF_DT:D_IN_PAD])

    acc = convb_ref[...]
    for i in range(CONV_W):
        start = SUBLANES - (CONV_W - 1) + i
        acc = acc + xbc_ref[start:start + tl, :] * convw_ref[i:i + 1, :]
    act_ref[...] = _silu(acc)

    tril_b = jnp.where(causal, 1.0, 0.0).astype(BF16)
    a_row = -jnp.exp(alog_ref[...])
    lane256 = lax.broadcasted_iota(jnp.int32, (CHUNK, A_WIDTH), 1)
    r256 = lax.broadcasted_iota(jnp.int32, (2 * CHUNK, 2 * CHUNK), 0)
    c256 = lax.broadcasted_iota(jnp.int32, (2 * CHUNK, 2 * CHUNK), 1)
    t256 = jnp.bitwise_and(r256, CHUNK - 1)
    win_mask = (c256 > t256) & (c256 <= t256 + CHUNK)
    rcol = lax.broadcasted_iota(jnp.int32, (2 * CHUNK, 1), 0)
    sink_a = jnp.where(rcol < CHUNK, sinks_ref[layer, 0], sinks_ref[layer, 3])
    sink_b = jnp.where(rcol < CHUNK, sinks_ref[layer, 1], sinks_ref[layer, 2])

    def chunk(c, carry):
        r0 = pl.multiple_of(c * CHUNK, CHUNK)
        rows = pl.ds(r0, CHUNK)

        u = jax.nn.gelu(proj_ref[rows, OFF_AU:OFF_AU + A_WIDTH])
        vn = _layer_norm(jax.nn.gelu(proj_ref[rows, OFF_AV:OFF_AV + A_WIDTH]),
                         lnvg_ref[...], lnvb_ref[...])
        vblk = jnp.concatenate(
            [jnp.where((lane256 >= h * HEAD_DIM) & (lane256 < (h + 1) * HEAD_DIM), vn, 0.0)
             for h in range(4)], axis=0).astype(BF16)
        mix_a = _dot(wcat_ref[...], vblk) + bs_ref[...]
        mix_ref[rows, 0:A_WIDTH] = (u * mix_a).astype(BF16)

        q01 = proj_ref[rows, OFF_Q:OFF_Q + LANES]
        q23 = proj_ref[rows, OFF_Q + LANES:OFF_Q + 2 * LANES]
        kc = proj_ref[rows, OFF_K:OFF_K + KV_WIDTH]
        vc = proj_ref[rows, OFF_V:OFF_V + KV_WIDTH]
        k2 = jnp.concatenate([kprev_ref[...], kc], axis=0)
        v2 = jnp.concatenate([vprev_ref[...], vc], axis=0)
        k2b = k2.astype(BF16)
        v2b = v2.astype(BF16)
        k2rb = pltpu.roll(k2, HEAD_DIM, 1).astype(BF16)
        v2rb = pltpu.roll(v2, HEAD_DIM, 1).astype(BF16)
        q_a = jnp.concatenate([jnp.where(lane_lo, q01, 0.0), jnp.where(lane_lo, 0.0, q23)], axis=0)
        q_b = jnp.concatenate([jnp.where(lane_lo, 0.0, q01), jnp.where(lane_lo, q23, 0.0)], axis=0)
        first = jnp.logical_and(t == 0, c == 0)
        vis = win_mask & (c256 >= jnp.where(first, CHUNK, 0))

        def attend(qm, kk, vv, sink):
            s = jnp.where(vis, _dot_nt(qm.astype(BF16), kk) * ATT_SCALE, NEG)
            m = jnp.maximum(jnp.max(s, axis=-1, keepdims=True), sink)
            p = jnp.exp(s - m)
            den = jnp.sum(p, axis=-1, keepdims=True) + jnp.exp(sink - m)
            return _dot((p * (1.0 / den)).astype(BF16), vv)

        o_a = attend(q_a, k2b, v2b, sink_a)
        o_b = attend(q_b, k2rb, v2rb, sink_b)
        mix_ref[rows, A_WIDTH:A_WIDTH + LANES] = jnp.where(
            lane_lo, o_a[0:CHUNK], o_b[0:CHUNK]).astype(BF16)
        mix_ref[rows, A_WIDTH + LANES:A_WIDTH + 2 * LANES] = jnp.where(
            lane_lo, o_b[CHUNK:], o_a[CHUNK:]).astype(BF16)
        kprev_ref[...] = kc
        vprev_ref[...] = vc

        xs = act_ref[rows, 0:C_WIDTH]
        bm = act_ref[rows, C_WIDTH:C_WIDTH + 2 * D_STATE]
        cm = act_ref[rows, C_WIDTH + 2 * D_STATE:CONV_DIM]
        z = proj_ref[rows, OFF_Z:OFF_Z + C_WIDTH]
        dt = jax.nn.softplus(dtr_ref[rows, :] + dtb_ref[...])
        cum = _dot_exact_lhs(tril_b, dt * a_row)
        cum_row = cum.T[0:SSM_HEADS]
        dt_row = dt.T[0:SSM_HEADS]
        toend_row = jnp.exp(cum_row[:, CHUNK - 1:CHUNK] - cum_row) * dt_row
        ecl = jnp.exp(cum[CHUNK - 1:CHUNK, :])
        ys = []
        for g in range(2):
            bg = bm[:, g * D_STATE:(g + 1) * D_STATE]
            cg = cm[:, g * D_STATE:(g + 1) * D_STATE]
            cb = _dot_nt(cg.astype(BF16), bg.astype(BF16))
            bgt = bg.T
            for pr in range(2):
                pair = g * 2 + pr
                psl = slice(pair * LANES, (pair + 1) * LANES)
                lhs, btw = [], []
                for hh in range(2):
                    h = pair * 2 + hh
                    cc = jnp.broadcast_to(cum[:, h:h + 1], (CHUNK, CHUNK))
                    dec = jnp.exp(jnp.where(causal, cc - cum_row[h:h + 1, :], NEG))
                    m_h = cb * dec * dt_row[h:h + 1, :]
                    ce_h = cg * jnp.exp(cc)
                    lhs.append(jnp.concatenate([m_h, ce_h], axis=1).astype(BF16))
                    btw.append((bgt * toend_row[h:h + 1, :]).astype(BF16))
                xs_p = xs[:, psl]
                ht_p = hT_ref[:, psl]
                rhs = jnp.concatenate([xs_p, ht_p], axis=0).astype(BF16)
                yy = _dot(jnp.concatenate(lhs, axis=0), rhs)
                ys.append(jnp.where(lane_lo, yy[0:CHUNK], yy[CHUNK:]))
                st = _dot(jnp.concatenate(btw, axis=0), xs_p.astype(BF16))
                ecl_p = jnp.where(lane_lo_row,
                                  jnp.broadcast_to(ecl[:, 2 * pair:2 * pair + 1], (1, LANES)),
                                  jnp.broadcast_to(ecl[:, 2 * pair + 1:2 * pair + 2], (1, LANES)))
                hT_ref[:, psl] = ht_p * ecl_p + jnp.where(lane_lo, st[0:CHUNK], st[CHUNK:])
        y = jnp.concatenate(ys, axis=1)
        out_c = _gated_rmsnorm(y, xs, z, dskip_ref[...], gnw_ref[...])
        mix_ref[rows, A_WIDTH + B_WIDTH:D_MODEL] = out_c.astype(BF16)
        return carry

    lax.fori_loop(0, n_chunks, chunk, 0)

    mix_o = _dot(mix_ref[...], w_out_ref[...])
    x1_ref[0] = _layer_norm(ALPHA * x_ref[0] + mix_o, ln1g_ref[...], ln1b_ref[...])

    tail = xbc_ref[SUBLANES + tl - (CONV_W - 1):SUBLANES + tl, :]
    xbc_ref[SUBLANES - (CONV_W - 1):SUBLANES, :] = tail

    @pl.when(t == nt - 1)
    def _final():
        klast_ref[0] = kprev_ref[...]
        vlast_ref[0] = vprev_ref[...]
        convnew_ref[0] = tail
        ssm_ref[0] = hT_ref[...].T


def _full(shape):
    n = len(shape)
    return pl.BlockSpec(shape, lambda *_: (0,) * n)


def _layer(shape, l, single_buffer=False):
    n = len(shape)
    mode = pl.Buffered(1) if single_buffer else None
    return pl.BlockSpec((None,) + tuple(shape), lambda *_: (l,) + (0,) * n, pipeline_mode=mode)


def _mixer_call(l, x, sinks, w_in, lnvg, lnvb, ws, bs_exp, convw, convb, dtb, alog, dskip, gnw,
                w_out, ln1g, ln1b, *, tl):
    bsz, seq, _ = x.shape
    nt = seq // tl
    kern = functools.partial(_mixer_kernel, tl=tl, layer=l)
    in_specs = [
        pl.BlockSpec(memory_space=pltpu.SMEM),
        pl.BlockSpec((1, tl, D_MODEL), lambda b, t: (b, t, 0)),
        _layer((D_MODEL, D_IN_PAD), l, single_buffer=True),
        _layer((1, A_WIDTH), l), _layer((1, A_WIDTH), l),
        _layer((4, CHUNK, CHUNK), l), _layer((CHUNK, A_WIDTH), l),
        _layer((CONV_W, CONV_DIM), l), _layer((1, CONV_DIM), l),
        _layer((1, LANES), l), _layer((1, LANES), l),
        _layer((1, C_WIDTH), l), _layer((1, C_WIDTH), l),
        _layer((D_MODEL, D_MODEL), l, single_buffer=True),
        _layer((1, D_MODEL), l), _layer((1, D_MODEL), l),
    ]
    out_shape = (
        jax.ShapeDtypeStruct((bsz, seq, D_MODEL), F32),
        jax.ShapeDtypeStruct((bsz, CHUNK, KV_WIDTH), F32),
        jax.ShapeDtypeStruct((bsz, CHUNK, KV_WIDTH), F32),
        jax.ShapeDtypeStruct((bsz, CONV_W - 1, CONV_DIM), F32),
        jax.ShapeDtypeStruct((bsz, C_WIDTH, D_STATE), F32),
    )
    out_specs = (
        pl.BlockSpec((1, tl, D_MODEL), lambda b, t: (b, t, 0)),
        pl.BlockSpec((1, CHUNK, KV_WIDTH), lambda b, t: (b, 0, 0)),
        pl.BlockSpec((1, CHUNK, KV_WIDTH), lambda b, t: (b, 0, 0)),
        pl.BlockSpec((1, CONV_W - 1, CONV_DIM), lambda b, t: (b, 0, 0)),
        pl.BlockSpec((1, C_WIDTH, D_STATE), lambda b, t: (b, 0, 0)),
    )
    scratch = [
        pltpu.VMEM((tl, OFF_XBC), F32),
        pltpu.VMEM((SUBLANES + tl, CONV_DIM), F32),
        pltpu.VMEM((tl, CONV_DIM), F32),
        pltpu.VMEM((tl, LANES), F32),
        pltpu.VMEM((tl, D_MODEL), BF16),
        pltpu.VMEM((CHUNK, KV_WIDTH), F32),
        pltpu.VMEM((CHUNK, KV_WIDTH), F32),
        pltpu.VMEM((D_STATE, C_WIDTH), F32),
        pltpu.VMEM((CHUNK, 4 * CHUNK), BF16),
    ]
    return pl.pallas_call(
        kern, out_shape=out_shape, grid=(bsz, nt), in_specs=in_specs, out_specs=out_specs,
        scratch_shapes=scratch, name="prompt_mixer",
        compiler_params=pltpu.CompilerParams(
            dimension_semantics=("arbitrary", "arbitrary"), vmem_limit_bytes=VMEM_LIMIT),
    )(sinks, x, w_in, lnvg, lnvb, ws, bs_exp, convw, convb, dtb, alog, dskip, gnw, w_out, ln1g, ln1b)


FF_CHUNK = 1024


def _ffn_kernel(x_ref, w1_ref, w2_ref, g_ref, b_ref, o_ref):
    x = x_ref[...]
    xb = x.astype(BF16)
    acc = jnp.zeros(x.shape, F32)
    for c in range(D_FF // FF_CHUNK):
        sl = slice(c * FF_CHUNK, (c + 1) * FF_CHUNK)
        h = jnp.maximum(_dot(xb, w1_ref[:, sl]), 0.0)
        acc = acc + _dot((h * h).astype(BF16), w2_ref[sl, :])
    o_ref[...] = _layer_norm(ALPHA * x + acc, g_ref[...], b_ref[...])


def _ffn_call(l, x2d, w1, w2, g, b, *, tm):
    rows = x2d.shape[0]
    return pl.pallas_call(
        _ffn_kernel, out_shape=jax.ShapeDtypeStruct((rows, D_MODEL), F32),
        grid=(rows // tm,),
        in_specs=[pl.BlockSpec((tm, D_MODEL), lambda i: (i, 0)),
                  _layer((D_MODEL, D_FF), l, single_buffer=True),
                  _layer((D_FF, D_MODEL), l, single_buffer=True),
                  _layer((1, D_MODEL), l), _layer((1, D_MODEL), l)],
        out_specs=pl.BlockSpec((tm, D_MODEL), lambda i: (i, 0)),
        name="ffn",
        compiler_params=pltpu.CompilerParams(
            dimension_semantics=("arbitrary",), vmem_limit_bytes=VMEM_LIMIT),
    )(x2d, w1, w2, g, b)


def _s1_kernel(x_ref, w_in_ref, lnvg_ref, lnvb_ref, ws0_ref, bs0_ref, convw_ref, convb_ref,
               cst_ref, dtb_ref, alog_ref, e_ref,
               vn_ref, outa_ref, q_ref, k_ref, v_ref, z_ref, cnew_ref, xs_ref, bm_ref,
               xT_ref, cT_ref, dec_ref):
    xb = x_ref[...].astype(BF16)
    proj = _dot(xb, w_in_ref[...])
    u = jax.nn.gelu(proj[:, OFF_AU:OFF_AU + A_WIDTH])
    vn = _layer_norm(jax.nn.gelu(proj[:, OFF_AV:OFF_AV + A_WIDTH]), lnvg_ref[...], lnvb_ref[...])
    vn_ref[...] = vn
    outa_ref[...] = u * (ws0_ref[...] * vn + bs0_ref[...])
    q_ref[...] = proj[:, OFF_Q:OFF_Q + B_WIDTH]
    k_ref[...] = proj[:, OFF_K:OFF_K + KV_WIDTH]
    v_ref[...] = proj[:, OFF_V:OFF_V + KV_WIDTH]
    z_ref[...] = proj[:, OFF_Z:OFF_Z + C_WIDTH]
    xbc = proj[:, OFF_XBC:OFF_DT]
    acc = convb_ref[...] + xbc * convw_ref[CONV_W - 1:CONV_W, :]
    for i in range(CONV_W - 1):
        acc = acc + cst_ref[i] * convw_ref[i:i + 1, :]
    cnew_ref[0] = cst_ref[1]
    cnew_ref[1] = cst_ref[2]
    cnew_ref[2] = xbc
    act = _silu(acc)
    xs = act[:, 0:C_WIDTH]
    xs_ref[...] = xs
    bm_ref[...] = act[:, C_WIDTH:C_WIDTH + 2 * D_STATE]
    cT_ref[...] = act[:, C_WIDTH + 2 * D_STATE:CONV_DIM].T
    dt = jax.nn.softplus(proj[:, OFF_DT:D_IN_PAD] + dtb_ref[...])
    dec_ref[...] = jnp.exp(dt * (-jnp.exp(alog_ref[...])))
    dt_exp = _dot_exact_rhs(dt, e_ref[...])
    xT_ref[...] = (xs * dt_exp).T.astype(BF16)


def _s1_call(l, x, w_in, lnvg, lnvb, ws0, bs0, convw, convb, cst, dtb, alog, emat):
    n = x.shape[0]
    in_specs = [
        _full((n, D_MODEL)), _layer((D_MODEL, D_IN_PAD), l),
        _layer((1, A_WIDTH), l), _layer((1, A_WIDTH), l), _layer((1, A_WIDTH), l), _layer((1, A_WIDTH), l),
        _layer((CONV_W, CONV_DIM), l), _layer((1, CONV_DIM), l), _layer((CONV_W - 1, n, CONV_DIM), l),
        _layer((1, LANES), l), _layer((1, LANES), l), _full((LANES, C_WIDTH)),
    ]
    out_shape = (
        jax.ShapeDtypeStruct((n, A_WIDTH), F32), jax.ShapeDtypeStruct((n, A_WIDTH), F32),
        jax.ShapeDtypeStruct((n, B_WIDTH), F32), jax.ShapeDtypeStruct((n, KV_WIDTH), F32),
        jax.ShapeDtypeStruct((n, KV_WIDTH), F32), jax.ShapeDtypeStruct((n, C_WIDTH), F32),
        jax.ShapeDtypeStruct((CONV_W - 1, n, CONV_DIM), F32),
        jax.ShapeDtypeStruct((n, C_WIDTH), F32), jax.ShapeDtypeStruct((n, 2 * D_STATE), F32),
        jax.ShapeDtypeStruct((C_WIDTH, n), BF16), jax.ShapeDtypeStruct((2 * D_STATE, n), F32),
        jax.ShapeDtypeStruct((n, LANES), F32),
    )
    return pl.pallas_call(
        _s1_kernel, out_shape=out_shape, grid=(1,), in_specs=in_specs,
        out_specs=tuple(_full(o.shape) for o in out_shape), name="sample_pre",
        compiler_params=pltpu.CompilerParams(
            dimension_semantics=("arbitrary",), vmem_limit_bytes=VMEM_LIMIT),
    )(x, w_in, lnvg, lnvb, ws0, bs0, convw, convb, cst, dtb, alog, emat)


S2_TB = SUBLANES


def _s2_kernel(sinks_ref, q_ref, kn_ref, vn_ref, kst_ref, vst_ref, h_ref, xT_ref, bm_ref,
               cT_ref, dec_ref, ob_ref, hnew_ref, yT_ref, *, layer):
    i = pl.program_id(0)
    tb = S2_TB
    lane = lax.broadcasted_iota(jnp.int32, (tb, LANES), 1)
    lo = lane < HEAD_DIM

    q01 = q_ref[:, 0:LANES]
    q23 = q_ref[:, LANES:2 * LANES]
    qh = jnp.concatenate([
        jnp.where(lo, q01, 0.0),
        jnp.where(lo, pltpu.roll(q01, HEAD_DIM, 1), 0.0),
        jnp.where(lo, 0.0, pltpu.roll(q23, HEAD_DIM, 1)),
        jnp.where(lo, 0.0, q23)], axis=0)
    qhb = qh.astype(BF16)
    rowi = lax.broadcasted_iota(jnp.int32, (4 * tb, LANES), 0)
    coli = lax.broadcasted_iota(jnp.int32, (4 * tb, LANES), 1)
    tok = jnp.bitwise_and(rowi, tb - 1)
    s = jnp.zeros((4 * tb, LANES), F32)
    for j in range(tb):
        sj = _dot_nt(qhb, kst_ref[j].astype(BF16))
        s = jnp.where(tok == j, sj, s)
    s = jnp.where(coli >= 1, s * ATT_SCALE, NEG)
    kn4 = jnp.concatenate([kn_ref[...]] * 4, axis=0)
    vn4 = jnp.concatenate([vn_ref[...]] * 4, axis=0)
    s_new = jnp.sum(qh * kn4, axis=-1, keepdims=True) * ATT_SCALE
    head = jnp.right_shift(rowi[:, 0:1], 3)
    sink = jnp.where(head == 0, sinks_ref[layer, 0],
                     jnp.where(head == 1, sinks_ref[layer, 1],
                               jnp.where(head == 2, sinks_ref[layer, 2], sinks_ref[layer, 3])))
    m = jnp.maximum(jnp.maximum(jnp.max(s, axis=-1, keepdims=True), s_new), sink)
    p = jnp.exp(s - m)
    p_new = jnp.exp(s_new - m)
    inv = 1.0 / (jnp.sum(p, axis=-1, keepdims=True) + p_new + jnp.exp(sink - m))
    p = p * inv
    o = (p_new * inv) * vn4
    for j in range(tb):
        pj = jnp.where(tok == j, p, 0.0).astype(BF16)
        o = o + _dot(pj, vst_ref[j].astype(BF16))
    ob_ref[:, 0:LANES] = jnp.where(lo, o[0:tb], pltpu.roll(o[tb:2 * tb], HEAD_DIM, 1))
    ob_ref[:, LANES:2 * LANES] = jnp.where(lo, pltpu.roll(o[2 * tb:3 * tb], HEAD_DIM, 1), o[3 * tb:])

    @pl.when(i == 0)
    def _():
        yT_ref[...] = jnp.zeros_like(yT_ref)

    r128 = lax.broadcasted_iota(jnp.int32, (LANES, LANES), 0)
    c128 = lax.broadcasted_iota(jnp.int32, (LANES, LANES), 1)
    hp_g = C_WIDTH // 2
    for j in range(tb):
        b = i * tb + j
        brow = bm_ref[pl.ds(b, 1), :]
        drow = dec_ref[pl.ds(b, 1), :]
        for g in range(2):
            dm = jnp.where(r128 == b,
                           jnp.broadcast_to(brow[:, g * D_STATE:(g + 1) * D_STATE], (LANES, D_STATE)),
                           0.0).astype(BF16)
            upd = _dot(xT_ref[g * hp_g:(g + 1) * hp_g, :], dm)
            parts = []
            for hh in range(4):
                h = g * 4 + hh
                rs = slice(h * HEAD_DIM, (h + 1) * HEAD_DIM)
                hn = (h_ref[j, rs, :] * jnp.broadcast_to(drow[:, h:h + 1], (HEAD_DIM, D_STATE))
                      + upd[hh * HEAD_DIM:(hh + 1) * HEAD_DIM])
                hnew_ref[j, rs, :] = hn
                parts.append(hn)
            cmm = jnp.where(c128 == b, cT_ref[g * D_STATE:(g + 1) * D_STATE, :], 0.0).astype(BF16)
            yT_ref[g * hp_g:(g + 1) * hp_g, :] += _dot(jnp.concatenate(parts, axis=0).astype(BF16), cmm)


def _s2_call(l, sinks, q, kn, vn, kst, vst, hst, xT, bm, cT, dec):
    n = q.shape[0]
    tb = S2_TB
    in_specs = [
        pl.BlockSpec(memory_space=pltpu.SMEM),
        pl.BlockSpec((tb, B_WIDTH), lambda i: (i, 0)),
        pl.BlockSpec((tb, KV_WIDTH), lambda i: (i, 0)),
        pl.BlockSpec((tb, KV_WIDTH), lambda i: (i, 0)),
        pl.BlockSpec((None, tb, CHUNK, KV_WIDTH), lambda i: (l, i, 0, 0)),
        pl.BlockSpec((None, tb, CHUNK, KV_WIDTH), lambda i: (l, i, 0, 0)),
        pl.BlockSpec((None, tb, C_WIDTH, D_STATE), lambda i: (l, i, 0, 0)),
        _full((C_WIDTH, n)), _full((n, 2 * D_STATE)), _full((2 * D_STATE, n)), _full((n, LANES)),
    ]
    out_shape = (
        jax.ShapeDtypeStruct((n, B_WIDTH), F32),
        jax.ShapeDtypeStruct((n, C_WIDTH, D_STATE), F32),
        jax.ShapeDtypeStruct((C_WIDTH, n), F32),
    )
    out_specs = (
        pl.BlockSpec((tb, B_WIDTH), lambda i: (i, 0)),
        pl.BlockSpec((tb, C_WIDTH, D_STATE), lambda i: (i, 0, 0)),
        _full((C_WIDTH, n)),
    )
    return pl.pallas_call(
        functools.partial(_s2_kernel, layer=l), out_shape=out_shape, grid=(n // tb,),
        in_specs=in_specs, out_specs=out_specs,
        name="sample_state",
        compiler_params=pltpu.CompilerParams(
            dimension_semantics=("arbitrary",), vmem_limit_bytes=VMEM_LIMIT),
    )(sinks, q, kn, vn, kst, vst, hst, xT, bm, cT, dec)


def _s3_kernel(yT_ref, xs_ref, z_ref, dskip_ref, gnw_ref, outa_ref, outb_ref, x_ref, w_out_ref,
               g_ref, b_ref, x1_ref):
    out_c = _gated_rmsnorm(yT_ref[...].T, xs_ref[...], z_ref[...], dskip_ref[...], gnw_ref[...])
    mix = jnp.concatenate([outa_ref[...], outb_ref[...], out_c], axis=1).astype(BF16)
    x1_ref[...] = _layer_norm(ALPHA * x_ref[...] + _dot(mix, w_out_ref[...]), g_ref[...], b_ref[...])


def _s3_call(l, yT, xs, z, dskip, gnw, outa, outb, x, w_out, g, b):
    n = x.shape[0]
    in_specs = [
        _full((C_WIDTH, n)), _full((n, C_WIDTH)), _full((n, C_WIDTH)),
        _layer((1, C_WIDTH), l), _layer((1, C_WIDTH), l),
        _full((n, A_WIDTH)), _full((n, B_WIDTH)), _full((n, D_MODEL)),
        _layer((D_MODEL, D_MODEL), l), _layer((1, D_MODEL), l), _layer((1, D_MODEL), l),
    ]
    return pl.pallas_call(
        _s3_kernel, out_shape=jax.ShapeDtypeStruct(x.shape, F32), grid=(1,), in_specs=in_specs,
        out_specs=_full((n, D_MODEL)), name="sample_post",
        compiler_params=pltpu.CompilerParams(
            dimension_semantics=("arbitrary",), vmem_limit_bytes=VMEM_LIMIT),
    )(yT, xs, z, dskip, gnw, outa, outb, x, w_out, g, b)


PROMPT_TL = 512
PROMPT_TM = 1024


def kernel(x_prompt, x_sample, state_attn_k, state_attn_v, state_conv, state_ssm, w_in, ln_v_g,
           ln_v_b, w_s, b_s, sinks, conv_w, conv_b, dt_bias, a_log, d_skip, gn_w, w_out, ln1_g,
           ln1_b, w1, w2, ln2_g, ln2_b):
    bsz, seq, _ = x_prompt.shape
    n_s = x_sample.shape[0]
    d_in = w_in.shape[-1]

    w_in_b = jnp.pad(w_in, ((0, 0), (0, 0), (0, D_IN_PAD - d_in))).astype(BF16)
    w_out_b = w_out.astype(BF16)
    w1_b = w1.astype(BF16)
    w2_b = w2.astype(BF16)
    pad_h = ((0, 0), (0, LANES - SSM_HEADS))
    dtb_p = jnp.pad(dt_bias, pad_h)[:, None, :]
    alog_p = jnp.pad(a_log, pad_h)[:, None, :]
    dskip_e = jnp.repeat(d_skip, HEAD_DIM, axis=-1)[:, None, :]
    bs_e = jnp.repeat(jnp.swapaxes(b_s, 1, 2), HEAD_DIM, axis=-1)
    ws0_e = jnp.repeat(w_s[:, :, 0, 0], HEAD_DIM, axis=-1)[:, None, :]
    bs0_e = jnp.repeat(b_s[:, :, 0], HEAD_DIM, axis=-1)[:, None, :]
    emat = (lax.broadcasted_iota(jnp.int32, (LANES, C_WIDTH), 0)
            == lax.broadcasted_iota(jnp.int32, (LANES, C_WIDTH), 1) // HEAD_DIM).astype(BF16)
    row = lambda a: a[:, None, :]
    lnvg, lnvb, convb, gnw = row(ln_v_g), row(ln_v_b), row(conv_b), row(gn_w)
    ln1g, ln1b, ln2g, ln2b = row(ln1_g), row(ln1_b), row(ln2_g), row(ln2_b)
    kst = state_attn_k.reshape(DEPTH, n_s, CHUNK, KV_WIDTH)
    vst = state_attn_v.reshape(DEPTH, n_s, CHUNK, KV_WIDTH)
    hst = state_ssm.reshape(DEPTH, n_s, C_WIDTH, D_STATE)
    cst = jnp.swapaxes(state_conv, 1, 2)

    yp = x_prompt
    ys = x_sample.reshape(n_s, D_MODEL)
    kp, vp, cp, hp, ksm, vsm, csm, hsm, vns = [], [], [], [], [], [], [], [], []
    for l in range(DEPTH):
        x1, k_l, v_l, c_l, h_l = _mixer_call(
            l, yp, sinks, w_in_b, lnvg, lnvb, w_s, bs_e, conv_w, convb,
            dtb_p, alog_p, dskip_e, gnw, w_out_b, ln1g, ln1b, tl=PROMPT_TL)
        yp = _ffn_call(l, x1.reshape(bsz * seq, D_MODEL), w1_b, w2_b, ln2g, ln2b,
                       tm=PROMPT_TM).reshape(bsz, seq, D_MODEL)
        kp.append(k_l); vp.append(v_l); cp.append(c_l); hp.append(h_l)

        (vn_s, outa, q_s, k_s, v_s, z_s, cnew, xs_s, bm_s, xT, cT, dec) = _s1_call(
            l, ys, w_in_b, lnvg, lnvb, ws0_e, bs0_e, conv_w, convb, cst, dtb_p, alog_p, emat)
        outb, hnew, yT = _s2_call(l, sinks, q_s, k_s, v_s, kst, vst, hst, xT, bm_s, cT, dec)
        x1s = _s3_call(l, yT, xs_s, z_s, dskip_e, gnw, outa, outb, ys, w_out_b, ln1g, ln1b)
        ys = _ffn_call(l, x1s, w1_b, w2_b, ln2g, ln2b, tm=n_s)
        ksm.append(k_s); vsm.append(v_s); csm.append(cnew); hsm.append(hnew); vns.append(vn_s)

    kv_p = (DEPTH, bsz, CHUNK, 2, HEAD_DIM)
    kv_s = (DEPTH, n_s, 1, 2, HEAD_DIM)
    ssm_shape = (SSM_HEADS, HEAD_DIM, D_STATE)
    return (yp, ys.reshape(n_s, 1, D_MODEL),
            jnp.stack(kp).reshape(kv_p), jnp.stack(vp).reshape(kv_p),
            jnp.stack(cp), jnp.stack(hp).reshape((DEPTH, bsz) + ssm_shape),
            jnp.stack(ksm).reshape(kv_s), jnp.stack(vsm).reshape(kv_s),
            jnp.swapaxes(jnp.stack(csm), 1, 2),
            jnp.stack(hsm).reshape((DEPTH, n_s) + ssm_shape),
            jnp.stack(vns).reshape(DEPTH, n_s, 1, A_WIDTH))
```

```python
import functools

import jax
import jax.numpy as jnp
from jax import lax
from jax.experimental import pallas as pl
from jax.experimental.pallas import tpu as pltpu

F32 = jnp.float32
BF16 = jnp.bfloat16

D_MODEL = 1024
DEPTH = 4
HEAD_DIM = 64
A_WIDTH = 256
B_WIDTH = 256
KV_WIDTH = 128
C_WIDTH = 512
CONV_DIM = 1024
CONV_W = 4
SSM_HEADS = 8
D_STATE = 128
D_FF = 4096
CHUNK = 128
ALPHA = (2 * DEPTH) ** 0.25
LN_EPS = 1e-5
RMS_EPS = 1e-6
ATT_SCALE = HEAD_DIM ** -0.5
LOG2E = 1.4426950408889634
Q_SCALE = ATT_SCALE * LOG2E
NEG = -1e30

OFF_AU, OFF_AV, OFF_Q, OFF_K, OFF_V, OFF_Z = 0, 256, 512, 768, 896, 1024
OFF_XBC = 1536
OFF_DT = 2560
D_IN_PAD = 2688
LANES = 128
SUBLANES = 8
VMEM_LIMIT = 56 * 1024 * 1024

NT_DIMS = (((1,), (1,)), ((), ()))


def _dot(a, b):
    return jnp.dot(a, b, preferred_element_type=F32)


def _dot_nt(a, b):
    return lax.dot_general(a, b, NT_DIMS, preferred_element_type=F32)


def _layer_norm(x, g, b):
    mu = jnp.mean(x, axis=-1, keepdims=True)
    xc = x - mu
    var = jnp.mean(xc * xc, axis=-1, keepdims=True)
    return xc * lax.rsqrt(var + LN_EPS) * g + b


def _silu(x):
    return x * jax.nn.sigmoid(x)


def _split3(x):
    hi = x.astype(BF16)
    r = x - hi.astype(F32)
    mid = r.astype(BF16)
    lo = (r - mid.astype(F32)).astype(BF16)
    return hi, mid, lo


def _dot_exact_rhs(x, m_bf16):
    hi, mid, lo = _split3(x)
    return _dot(hi, m_bf16) + _dot(mid, m_bf16) + _dot(lo, m_bf16)


def _gated_rmsnorm(y, xs, z, dskip, gnw):
    y2 = (y + dskip * xs) * _silu(z)
    ms = jnp.mean(y2 * y2, axis=-1, keepdims=True)
    return y2 * lax.rsqrt(ms + RMS_EPS) * gnw


def _mixer_kernel(sinks_ref, x_ref, w_in_ref, lnvg_ref, lnvb_ref, ws_ref, bs_ref,
                  convw_ref, convb_ref, dtb_ref, alog_ref, dskip_ref, gnw_ref,
                  w_out_ref, ln1g_ref, ln1b_ref,
                  x1_ref, klast_ref, vlast_ref, convnew_ref, ssm_ref,
                  proj_ref, xbc_ref, act_ref, dtr_ref, mix_ref, kprev_ref, vprev_ref,
                  hT_ref, wstk_ref, *, tl, layer):
    t = pl.program_id(1)
    nt = pl.num_programs(1)
    n_chunks = tl // CHUNK

    row_i = lax.broadcasted_iota(jnp.int32, (CHUNK, CHUNK), 0)
    col_i = lax.broadcasted_iota(jnp.int32, (CHUNK, CHUNK), 1)
    causal = col_i <= row_i
    lane_lo = col_i < HEAD_DIM
    lane_lo_row = lax.broadcasted_iota(jnp.int32, (1, LANES), 1) < HEAD_DIM

    @pl.when(t == 0)
    def _init():
        kprev_ref[...] = jnp.zeros_like(kprev_ref)
        vprev_ref[...] = jnp.zeros_like(vprev_ref)
        hT_ref[...] = jnp.zeros_like(hT_ref)
        xbc_ref[0:SUBLANES, :] = jnp.zeros((SUBLANES, CONV_DIM), F32)
        for h in range(4):
            wstk_ref[h * CHUNK:(h + 1) * CHUNK, :] = jnp.where(causal, ws_ref[h], 0.0).astype(BF16)

    xb = x_ref[0].astype(BF16)
    proj_ref[...] = _dot(xb, w_in_ref[:, 0:OFF_XBC])
    xbc_ref[SUBLANES:SUBLANES + tl, :] = _dot(xb, w_in_ref[:, OFF_XBC:OFF_DT])
    dtr_ref[...] = _dot(xb, w_in_ref[:, OFF_DT:D_IN_PAD])

    xh = xbc_ref[...]
    acc = xh * convw_ref[0:1, :]
    for i in range(1, CONV_W):
        acc = pltpu.roll(acc, 1, 0) + xh * convw_ref[i:i + 1, :]
    act_ref[...] = _silu(acc[SUBLANES:] + convb_ref[...])

    triu_b = jnp.where(row_i <= col_i, 1.0, 0.0).astype(BF16)
    a_col = -jnp.exp(alog_ref[...])
    zpad = jnp.zeros((CHUNK - SSM_HEADS, CHUNK), F32)
    lane256 = lax.broadcasted_iota(jnp.int32, (CHUNK, A_WIDTH), 1)
    r256 = lax.broadcasted_iota(jnp.int32, (2 * CHUNK, 2 * CHUNK), 0)
    c256 = lax.broadcasted_iota(jnp.int32, (2 * CHUNK, 2 * CHUNK), 1)
    t256 = jnp.bitwise_and(r256, CHUNK - 1)
    win_mask = (c256 > t256) & (c256 <= t256 + CHUNK)
    rcol = lax.broadcasted_iota(jnp.int32, (2 * CHUNK, 1), 0)
    sink_a = jnp.where(rcol < CHUNK, sinks_ref[layer, 0], sinks_ref[layer, 3]) * LOG2E
    sink_b = jnp.where(rcol < CHUNK, sinks_ref[layer, 1], sinks_ref[layer, 2]) * LOG2E

    for c in range(n_chunks):
        rows = slice(c * CHUNK, (c + 1) * CHUNK)

        u = jax.nn.gelu(proj_ref[rows, OFF_AU:OFF_AU + A_WIDTH])
        vn = _layer_norm(jax.nn.gelu(proj_ref[rows, OFF_AV:OFF_AV + A_WIDTH]),
                         lnvg_ref[...], lnvb_ref[...])
        pa = _dot(wstk_ref[...], vn.astype(BF16))
        mix_a = jnp.where(
            lane256 < HEAD_DIM, pa[0:CHUNK],
            jnp.where(lane256 < 2 * HEAD_DIM, pa[CHUNK:2 * CHUNK],
                      jnp.where(lane256 < 3 * HEAD_DIM, pa[2 * CHUNK:3 * CHUNK], pa[3 * CHUNK:])))
        mix_ref[rows, 0:A_WIDTH] = (u * (mix_a + bs_ref[...])).astype(BF16)

        q01 = proj_ref[rows, OFF_Q:OFF_Q + LANES] * Q_SCALE
        q23 = proj_ref[rows, OFF_Q + LANES:OFF_Q + 2 * LANES] * Q_SCALE
        kc = proj_ref[rows, OFF_K:OFF_K + KV_WIDTH]
        vc = proj_ref[rows, OFF_V:OFF_V + KV_WIDTH]
        if c == 0:
            kp, vp = kprev_ref[...], vprev_ref[...]
        else:
            prev = slice((c - 1) * CHUNK, c * CHUNK)
            kp = proj_ref[prev, OFF_K:OFF_K + KV_WIDTH]
            vp = proj_ref[prev, OFF_V:OFF_V + KV_WIDTH]
        k2 = jnp.concatenate([kp, kc], axis=0)
        v2 = jnp.concatenate([vp, vc], axis=0)
        k2b = k2.astype(BF16)
        v2b = v2.astype(BF16)
        k2rb = pltpu.roll(k2, HEAD_DIM, 1).astype(BF16)
        v2rb = pltpu.roll(v2, HEAD_DIM, 1).astype(BF16)
        q_a = jnp.concatenate([jnp.where(lane_lo, q01, 0.0), jnp.where(lane_lo, 0.0, q23)], axis=0)
        q_b = jnp.concatenate([jnp.where(lane_lo, 0.0, q01), jnp.where(lane_lo, q23, 0.0)], axis=0)
        vis = win_mask & (c256 >= jnp.where(t == 0, CHUNK, 0)) if c == 0 else win_mask

        def attend(qm, kk, vv, sink2):
            s = jnp.where(vis, _dot_nt(qm.astype(BF16), kk), NEG)
            m = jnp.maximum(jnp.max(s, axis=-1, keepdims=True), sink2)
            p = jnp.exp2(s - m)
            den = jnp.sum(p, axis=-1, keepdims=True) + jnp.exp2(sink2 - m)
            return _dot(p.astype(BF16), vv) * (1.0 / den)

        o_a = attend(q_a, k2b, v2b, sink_a)
        o_b = attend(q_b, k2rb, v2rb, sink_b)
        mix_ref[rows, A_WIDTH:A_WIDTH + LANES] = jnp.where(
            lane_lo, o_a[0:CHUNK], o_b[0:CHUNK]).astype(BF16)
        mix_ref[rows, A_WIDTH + LANES:A_WIDTH + 2 * LANES] = jnp.where(
            lane_lo, o_b[CHUNK:], o_a[CHUNK:]).astype(BF16)

        xs = act_ref[rows, 0:C_WIDTH]
        bm = act_ref[rows, C_WIDTH:C_WIDTH + 2 * D_STATE]
        cm = act_ref[rows, C_WIDTH + 2 * D_STATE:CONV_DIM]
        z = proj_ref[rows, OFF_Z:OFF_Z + C_WIDTH]
        dt_row = jax.nn.softplus(dtr_ref[rows, :].T[0:SSM_HEADS] + dtb_ref[...])
        cum_row = _dot_exact_rhs(dt_row * a_col, triu_b) * LOG2E
        cum = jnp.concatenate([cum_row, zpad], axis=0).T
        toend_row = jnp.exp2(cum_row[:, CHUNK - 1:CHUNK] - cum_row) * dt_row
        ecl = jnp.exp2(cum[CHUNK - 1:CHUNK, :])
        ys = []
        for g in range(2):
            bg = bm[:, g * D_STATE:(g + 1) * D_STATE]
            cg = cm[:, g * D_STATE:(g + 1) * D_STATE]
            cb = _dot_nt(cg.astype(BF16), bg.astype(BF16))
            bgt = bg.T
            for pr in range(2):
                pair = g * 2 + pr
                psl = slice(pair * LANES, (pair + 1) * LANES)
                lhs, btw = [], []
                for hh in range(2):
                    h = pair * 2 + hh
                    cc = jnp.broadcast_to(cum[:, h:h + 1], (CHUNK, CHUNK))
                    dec = jnp.exp2(jnp.where(causal, cc - cum_row[h:h + 1, :], NEG))
                    m_h = cb * dec * dt_row[h:h + 1, :]
                    ce_h = cg * jnp.exp2(cc)
                    lhs.append(jnp.concatenate([m_h, ce_h], axis=1).astype(BF16))
                    btw.append((bgt * toend_row[h:h + 1, :]).astype(BF16))
                xs_p = xs[:, psl]
                ht_p = hT_ref[:, psl]
                rhs = jnp.concatenate([xs_p, ht_p], axis=0).astype(BF16)
                yy = _dot(jnp.concatenate(lhs, axis=0), rhs)
                ys.append(jnp.where(lane_lo, yy[0:CHUNK], yy[CHUNK:]))
                st = _dot(jnp.concatenate(btw, axis=0), xs_p.astype(BF16))
                ecl_p = jnp.where(lane_lo_row,
                                  jnp.broadcast_to(ecl[:, 2 * pair:2 * pair + 1], (1, LANES)),
                                  jnp.broadcast_to(ecl[:, 2 * pair + 1:2 * pair + 2], (1, LANES)))
                hT_ref[:, psl] = ht_p * ecl_p + jnp.where(lane_lo, st[0:CHUNK], st[CHUNK:])
        y = jnp.concatenate(ys, axis=1)
        out_c = _gated_rmsnorm(y, xs, z, dskip_ref[...], gnw_ref[...])
        mix_ref[rows, A_WIDTH + B_WIDTH:D_MODEL] = out_c.astype(BF16)

    last = slice(tl - CHUNK, tl)
    kprev_ref[...] = proj_ref[last, OFF_K:OFF_K + KV_WIDTH]
    vprev_ref[...] = proj_ref[last, OFF_V:OFF_V + KV_WIDTH]

    mix_o = _dot(mix_ref[...], w_out_ref[...])
    x1_ref[0] = _layer_norm(ALPHA * x_ref[0] + mix_o, ln1g_ref[...], ln1b_ref[...])

    tail = xbc_ref[SUBLANES + tl - (CONV_W - 1):SUBLANES + tl, :]
    xbc_ref[SUBLANES - (CONV_W - 1):SUBLANES, :] = tail

    @pl.when(t == nt - 1)
    def _final():
        klast_ref[0] = kprev_ref[...]
        vlast_ref[0] = vprev_ref[...]
        convnew_ref[0] = tail
        ssm_ref[0] = hT_ref[...].T


def _full(shape):
    n = len(shape)
    return pl.BlockSpec(shape, lambda *_: (0,) * n)


def _layer(shape, l, single_buffer=False):
    n = len(shape)
    mode = pl.Buffered(1) if single_buffer else None
    return pl.BlockSpec((None,) + tuple(shape), lambda *_: (l,) + (0,) * n, pipeline_mode=mode)


def _mixer_call(l, x, sinks, w_in, lnvg, lnvb, ws, bs_exp, convw, convb, dtb, alog, dskip, gnw,
                w_out, ln1g, ln1b, *, tl):
    bsz, seq, _ = x.shape
    nt = seq // tl
    kern = functools.partial(_mixer_kernel, tl=tl, layer=l)
    in_specs = [
        pl.BlockSpec(memory_space=pltpu.SMEM),
        pl.BlockSpec((1, tl, D_MODEL), lambda b, t: (b, t, 0)),
        _layer((D_MODEL, D_IN_PAD), l, single_buffer=True),
        _layer((1, A_WIDTH), l), _layer((1, A_WIDTH), l),
        _layer((4, CHUNK, CHUNK), l), _layer((CHUNK, A_WIDTH), l),
        _layer((CONV_W, CONV_DIM), l), _layer((1, CONV_DIM), l),
        _layer((SSM_HEADS, 1), l), _layer((SSM_HEADS, 1), l),
        _layer((1, C_WIDTH), l), _layer((1, C_WIDTH), l),
        _layer((D_MODEL, D_MODEL), l, single_buffer=True),
        _layer((1, D_MODEL), l), _layer((1, D_MODEL), l),
    ]
    out_shape = (
        jax.ShapeDtypeStruct((bsz, seq, D_MODEL), F32),
        jax.ShapeDtypeStruct((bsz, CHUNK, KV_WIDTH), F32),
        jax.ShapeDtypeStruct((bsz, CHUNK, KV_WIDTH), F32),
        jax.ShapeDtypeStruct((bsz, CONV_W - 1, CONV_DIM), F32),
        jax.ShapeDtypeStruct((bsz, C_WIDTH, D_STATE), F32),
    )
    out_specs = (
        pl.BlockSpec((1, tl, D_MODEL), lambda b, t: (b, t, 0)),
        pl.BlockSpec((1, CHUNK, KV_WIDTH), lambda b, t: (b, 0, 0)),
        pl.BlockSpec((1, CHUNK, KV_WIDTH), lambda b, t: (b, 0, 0)),
        pl.BlockSpec((1, CONV_W - 1, CONV_DIM), lambda b, t: (b, 0, 0)),
        pl.BlockSpec((1, C_WIDTH, D_STATE), lambda b, t: (b, 0, 0)),
    )
    scratch = [
        pltpu.VMEM((tl, OFF_XBC), F32),
        pltpu.VMEM((SUBLANES + tl, CONV_DIM), F32),
        pltpu.VMEM((tl, CONV_DIM), F32),
        pltpu.VMEM((tl, LANES), F32),
        pltpu.VMEM((tl, D_MODEL), BF16),
        pltpu.VMEM((CHUNK, KV_WIDTH), F32),
        pltpu.VMEM((CHUNK, KV_WIDTH), F32),
        pltpu.VMEM((D_STATE, C_WIDTH), F32),
        pltpu.VMEM((4 * CHUNK, CHUNK), BF16),
    ]
    return pl.pallas_call(
        kern, out_shape=out_shape, grid=(bsz, nt), in_specs=in_specs, out_specs=out_specs,
        scratch_shapes=scratch, name="prompt_mixer",
        compiler_params=pltpu.CompilerParams(
            dimension_semantics=("arbitrary", "arbitrary"), vmem_limit_bytes=VMEM_LIMIT),
    )(sinks, x, w_in, lnvg, lnvb, ws, bs_exp, convw, convb, dtb, alog, dskip, gnw, w_out, ln1g, ln1b)


FF_CHUNK = 1024


def _ffn_kernel(x_ref, w1_ref, w2_ref, g_ref, b_ref, o_ref):
    x = x_ref[...]
    xb = x.astype(BF16)
    acc = jnp.zeros(x.shape, F32)
    for c in range(D_FF // FF_CHUNK):
        sl = slice(c * FF_CHUNK, (c + 1) * FF_CHUNK)
        h = jnp.maximum(_dot(xb, w1_ref[:, sl]), 0.0)
        acc = acc + _dot((h * h).astype(BF16), w2_ref[sl, :])
    o_ref[...] = _layer_norm(ALPHA * x + acc, g_ref[...], b_ref[...])


def _ffn_call(l, x2d, w1, w2, g, b, *, tm):
    rows = x2d.shape[0]
    return pl.pallas_call(
        _ffn_kernel, out_shape=jax.ShapeDtypeStruct((rows, D_MODEL), F32),
        grid=(rows // tm,),
        in_specs=[pl.BlockSpec((tm, D_MODEL), lambda i: (i, 0)),
                  _layer((D_MODEL, D_FF), l, single_buffer=True),
                  _layer((D_FF, D_MODEL), l, single_buffer=True),
                  _layer((1, D_MODEL), l), _layer((1, D_MODEL), l)],
        out_specs=pl.BlockSpec((tm, D_MODEL), lambda i: (i, 0)),
        name="ffn",
        compiler_params=pltpu.CompilerParams(
            dimension_semantics=("arbitrary",), vmem_limit_bytes=VMEM_LIMIT),
    )(x2d, w1, w2, g, b)


def _s1_kernel(x_ref, w_in_ref, lnvg_ref, lnvb_ref, ws0_ref, bs0_ref, convw_ref, convb_ref,
               cst_ref, dtb_ref, alog_ref, e_ref,
               vn_ref, outa_ref, q_ref, k_ref, v_ref, z_ref, cnew_ref, xs_ref, bm_ref,
               xT_ref, cT_ref, dec_ref):
    xb = x_ref[...].astype(BF16)
    proj = _dot(xb, w_in_ref[...])
    u = jax.nn.gelu(proj[:, OFF_AU:OFF_AU + A_WIDTH])
    vn = _layer_norm(jax.nn.gelu(proj[:, OFF_AV:OFF_AV + A_WIDTH]), lnvg_ref[...], lnvb_ref[...])
    vn_ref[...] = vn
    outa_ref[...] = u * (ws0_ref[...] * vn + bs0_ref[...])
    q_ref[...] = proj[:, OFF_Q:OFF_Q + B_WIDTH]
    k_ref[...] = proj[:, OFF_K:OFF_K + KV_WIDTH]
    v_ref[...] = proj[:, OFF_V:OFF_V + KV_WIDTH]
    z_ref[...] = proj[:, OFF_Z:OFF_Z + C_WIDTH]
    xbc = proj[:, OFF_XBC:OFF_DT]
    acc = convb_ref[...] + xbc * convw_ref[CONV_W - 1:CONV_W, :]
    for i in range(CONV_W - 1):
        acc = acc + cst_ref[i] * convw_ref[i:i + 1, :]
    cnew_ref[0] = cst_ref[1]
    cnew_ref[1] = cst_ref[2]
    cnew_ref[2] = xbc
    act = _silu(acc)
    xs = act[:, 0:C_WIDTH]
    xs_ref[...] = xs
    bm_ref[...] = act[:, C_WIDTH:C_WIDTH + 2 * D_STATE]
    cT_ref[...] = act[:, C_WIDTH + 2 * D_STATE:CONV_DIM].T
    dt = jax.nn.softplus(proj[:, OFF_DT:D_IN_PAD] + dtb_ref[...])
    dec_ref[...] = jnp.exp(dt * (-jnp.exp(alog_ref[...])))
    dt_exp = _dot_exact_rhs(dt, e_ref[...])
    xT_ref[...] = (xs * dt_exp).T.astype(BF16)


def _s1_call(l, x, w_in, lnvg, lnvb, ws0, bs0, convw, convb, cst, dtb, alog, emat):
    n = x.shape[0]
    in_specs = [
        _full((n, D_MODEL)), _layer((D_MODEL, D_IN_PAD), l),
        _layer((1, A_WIDTH), l), _layer((1, A_WIDTH), l), _layer((1, A_WIDTH), l), _layer((1, A_WIDTH), l),
        _layer((CONV_W, CONV_DIM), l), _layer((1, CONV_DIM), l), _layer((CONV_W - 1, n, CONV_DIM), l),
        _layer((1, LANES), l), _layer((1, LANES), l), _full((LANES, C_WIDTH)),
    ]
    out_shape = (
        jax.ShapeDtypeStruct((n, A_WIDTH), F32), jax.ShapeDtypeStruct((n, A_WIDTH), F32),
        jax.ShapeDtypeStruct((n, B_WIDTH), F32), jax.ShapeDtypeStruct((n, KV_WIDTH), F32),
        jax.ShapeDtypeStruct((n, KV_WIDTH), F32), jax.ShapeDtypeStruct((n, C_WIDTH), F32),
        jax.ShapeDtypeStruct((CONV_W - 1, n, CONV_DIM), F32),
        jax.ShapeDtypeStruct((n, C_WIDTH), F32), jax.ShapeDtypeStruct((n, 2 * D_STATE), F32),
        jax.ShapeDtypeStruct((C_WIDTH, n), BF16), jax.ShapeDtypeStruct((2 * D_STATE, n), F32),
        jax.ShapeDtypeStruct((n, LANES), F32),
    )
    return pl.pallas_call(
        _s1_kernel, out_shape=out_shape, grid=(1,), in_specs=in_specs,
        out_specs=tuple(_full(o.shape) for o in out_shape), name="sample_pre",
        compiler_params=pltpu.CompilerParams(
            dimension_semantics=("arbitrary",), vmem_limit_bytes=VMEM_LIMIT),
    )(x, w_in, lnvg, lnvb, ws0, bs0, convw, convb, cst, dtb, alog, emat)


S2_TB = SUBLANES


def _s2_kernel(sinks_ref, q_ref, kn_ref, vn_ref, kst_ref, vst_ref, h_ref, xT_ref, bm_ref,
               cT_ref, dec_ref, ob_ref, hnew_ref, yT_ref, *, layer):
    i = pl.program_id(0)
    tb = S2_TB
    lane = lax.broadcasted_iota(jnp.int32, (tb, LANES), 1)
    lo = lane < HEAD_DIM

    q01 = q_ref[:, 0:LANES]
    q23 = q_ref[:, LANES:2 * LANES]
    qh = jnp.concatenate([
        jnp.where(lo, q01, 0.0),
        jnp.where(lo, pltpu.roll(q01, HEAD_DIM, 1), 0.0),
        jnp.where(lo, 0.0, pltpu.roll(q23, HEAD_DIM, 1)),
        jnp.where(lo, 0.0, q23)], axis=0)
    qhb = qh.astype(BF16)
    rowi = lax.broadcasted_iota(jnp.int32, (4 * tb, LANES), 0)
    coli = lax.broadcasted_iota(jnp.int32, (4 * tb, LANES), 1)
    tok = jnp.bitwise_and(rowi, tb - 1)
    s = jnp.zeros((4 * tb, LANES), F32)
    for j in range(tb):
        sj = _dot_nt(qhb, kst_ref[j].astype(BF16))
        s = jnp.where(tok == j, sj, s)
    s = jnp.where(coli >= 1, s * ATT_SCALE, NEG)
    kn4 = jnp.concatenate([kn_ref[...]] * 4, axis=0)
    vn4 = jnp.concatenate([vn_ref[...]] * 4, axis=0)
    s_new = jnp.sum(qh * kn4, axis=-1, keepdims=True) * ATT_SCALE
    head = jnp.right_shift(rowi[:, 0:1], 3)
    sink = jnp.where(head == 0, sinks_ref[layer, 0],
                     jnp.where(head == 1, sinks_ref[layer, 1],
                               jnp.where(head == 2, sinks_ref[layer, 2], sinks_ref[layer, 3])))
    m = jnp.maximum(jnp.maximum(jnp.max(s, axis=-1, keepdims=True), s_new), sink)
    p = jnp.exp(s - m)
    p_new = jnp.exp(s_new - m)
    inv = 1.0 / (jnp.sum(p, axis=-1, keepdims=True) + p_new + jnp.exp(sink - m))
    p = p * inv
    o = (p_new * inv) * vn4
    for j in range(tb):
        pj = jnp.where(tok == j, p, 0.0).astype(BF16)
        o = o + _dot(pj, vst_ref[j].astype(BF16))
    ob_ref[:, 0:LANES] = jnp.where(lo, o[0:tb], pltpu.roll(o[tb:2 * tb], HEAD_DIM, 1))
    ob_ref[:, LANES:2 * LANES] = jnp.where(lo, pltpu.roll(o[2 * tb:3 * tb], HEAD_DIM, 1), o[3 * tb:])

    @pl.when(i == 0)
    def _():
        yT_ref[...] = jnp.zeros_like(yT_ref)

    r128 = lax.broadcasted_iota(jnp.int32, (LANES, LANES), 0)
    c128 = lax.broadcasted_iota(jnp.int32, (LANES, LANES), 1)
    hp_g = C_WIDTH // 2
    for j in range(tb):
        b = i * tb + j
        brow = bm_ref[pl.ds(b, 1), :]
        drow = dec_ref[pl.ds(b, 1), :]
        for g in range(2):
            dm = jnp.where(r128 == b,
                           jnp.broadcast_to(brow[:, g * D_STATE:(g + 1) * D_STATE], (LANES, D_STATE)),
                           0.0).astype(BF16)
            upd = _dot(xT_ref[g * hp_g:(g + 1) * hp_g, :], dm)
            parts = []
            for hh in range(4):
                h = g * 4 + hh
                rs = slice(h * HEAD_DIM, (h + 1) * HEAD_DIM)
                hn = (h_ref[j, rs, :] * jnp.broadcast_to(drow[:, h:h + 1], (HEAD_DIM, D_STATE))
                      + upd[hh * HEAD_DIM:(hh + 1) * HEAD_DIM])
                hnew_ref[j, rs, :] = hn
                parts.append(hn)
            cmm = jnp.where(c128 == b, cT_ref[g * D_STATE:(g + 1) * D_STATE, :], 0.0).astype(BF16)
            yT_ref[g * hp_g:(g + 1) * hp_g, :] += _dot(jnp.concatenate(parts, axis=0).astype(BF16), cmm)


def _s2_call(l, sinks, q, kn, vn, kst, vst, hst, xT, bm, cT, dec):
    n = q.shape[0]
    tb = S2_TB
    in_specs = [
        pl.BlockSpec(memory_space=pltpu.SMEM),
        pl.BlockSpec((tb, B_WIDTH), lambda i: (i, 0)),
        pl.BlockSpec((tb, KV_WIDTH), lambda i: (i, 0)),
        pl.BlockSpec((tb, KV_WIDTH), lambda i: (i, 0)),
        pl.BlockSpec((None, tb, CHUNK, KV_WIDTH), lambda i: (l, i, 0, 0)),
        pl.BlockSpec((None, tb, CHUNK, KV_WIDTH), lambda i: (l, i, 0, 0)),
        pl.BlockSpec((None, tb, C_WIDTH, D_STATE), lambda i: (l, i, 0, 0)),
        _full((C_WIDTH, n)), _full((n, 2 * D_STATE)), _full((2 * D_STATE, n)), _full((n, LANES)),
    ]
    out_shape = (
        jax.ShapeDtypeStruct((n, B_WIDTH), F32),
        jax.ShapeDtypeStruct((n, C_WIDTH, D_STATE), F32),
        jax.ShapeDtypeStruct((C_WIDTH, n), F32),
    )
    out_specs = (
        pl.BlockSpec((tb, B_WIDTH), lambda i: (i, 0)),
        pl.BlockSpec((tb, C_WIDTH, D_STATE), lambda i: (i, 0, 0)),
        _full((C_WIDTH, n)),
    )
    return pl.pallas_call(
        functools.partial(_s2_kernel, layer=l), out_shape=out_shape, grid=(n // tb,),
        in_specs=in_specs, out_specs=out_specs,
        name="sample_state",
        compiler_params=pltpu.CompilerParams(
            dimension_semantics=("arbitrary",), vmem_limit_bytes=VMEM_LIMIT),
    )(sinks, q, kn, vn, kst, vst, hst, xT, bm, cT, dec)


def _s3_kernel(yT_ref, xs_ref, z_ref, dskip_ref, gnw_ref, outa_ref, outb_ref, x_ref, w_out_ref,
               g_ref, b_ref, x1_ref):
    out_c = _gated_rmsnorm(yT_ref[...].T, xs_ref[...], z_ref[...], dskip_ref[...], gnw_ref[...])
    mix = jnp.concatenate([outa_ref[...], outb_ref[...], out_c], axis=1).astype(BF16)
    x1_ref[...] = _layer_norm(ALPHA * x_ref[...] + _dot(mix, w_out_ref[...]), g_ref[...], b_ref[...])


def _s3_call(l, yT, xs, z, dskip, gnw, outa, outb, x, w_out, g, b):
    n = x.shape[0]
    in_specs = [
        _full((C_WIDTH, n)), _full((n, C_WIDTH)), _full((n, C_WIDTH)),
        _layer((1, C_WIDTH), l), _layer((1, C_WIDTH), l),
        _full((n, A_WIDTH)), _full((n, B_WIDTH)), _full((n, D_MODEL)),
        _layer((D_MODEL, D_MODEL), l), _layer((1, D_MODEL), l), _layer((1, D_MODEL), l),
    ]
    return pl.pallas_call(
        _s3_kernel, out_shape=jax.ShapeDtypeStruct(x.shape, F32), grid=(1,), in_specs=in_specs,
        out_specs=_full((n, D_MODEL)), name="sample_post",
        compiler_params=pltpu.CompilerParams(
            dimension_semantics=("arbitrary",), vmem_limit_bytes=VMEM_LIMIT),
    )(yT, xs, z, dskip, gnw, outa, outb, x, w_out, g, b)


PROMPT_TL = 512
PROMPT_TM = 1024


def kernel(x_prompt, x_sample, state_attn_k, state_attn_v, state_conv, state_ssm, w_in, ln_v_g,
           ln_v_b, w_s, b_s, sinks, conv_w, conv_b, dt_bias, a_log, d_skip, gn_w, w_out, ln1_g,
           ln1_b, w1, w2, ln2_g, ln2_b):
    bsz, seq, _ = x_prompt.shape
    n_s = x_sample.shape[0]
    d_in = w_in.shape[-1]

    w_in_b = jnp.pad(w_in, ((0, 0), (0, 0), (0, D_IN_PAD - d_in))).astype(BF16)
    w_out_b = w_out.astype(BF16)
    w1_b = w1.astype(BF16)
    w2_b = w2.astype(BF16)
    pad_h = ((0, 0), (0, LANES - SSM_HEADS))
    dtb_p = jnp.pad(dt_bias, pad_h)[:, None, :]
    alog_p = jnp.pad(a_log, pad_h)[:, None, :]
    dskip_e = jnp.repeat(d_skip, HEAD_DIM, axis=-1)[:, None, :]
    bs_e = jnp.repeat(jnp.swapaxes(b_s, 1, 2), HEAD_DIM, axis=-1)
    ws0_e = jnp.repeat(w_s[:, :, 0, 0], HEAD_DIM, axis=-1)[:, None, :]
    bs0_e = jnp.repeat(b_s[:, :, 0], HEAD_DIM, axis=-1)[:, None, :]
    emat = (lax.broadcasted_iota(jnp.int32, (LANES, C_WIDTH), 0)
            == lax.broadcasted_iota(jnp.int32, (LANES, C_WIDTH), 1) // HEAD_DIM).astype(BF16)
    row = lambda a: a[:, None, :]
    lnvg, lnvb, convb, gnw = row(ln_v_g), row(ln_v_b), row(conv_b), row(gn_w)
    ln1g, ln1b, ln2g, ln2b = row(ln1_g), row(ln1_b), row(ln2_g), row(ln2_b)
    kst = state_attn_k.reshape(DEPTH, n_s, CHUNK, KV_WIDTH)
    vst = state_attn_v.reshape(DEPTH, n_s, CHUNK, KV_WIDTH)
    hst = state_ssm.reshape(DEPTH, n_s, C_WIDTH, D_STATE)
    cst = jnp.swapaxes(state_conv, 1, 2)

    yp = x_prompt
    ys = x_sample.reshape(n_s, D_MODEL)
    kp, vp, cp, hp, ksm, vsm, csm, hsm, vns = [], [], [], [], [], [], [], [], []
    for l in range(DEPTH):
        x1, k_l, v_l, c_l, h_l = _mixer_call(
            l, yp, sinks, w_in_b, lnvg, lnvb, w_s, bs_e, conv_w, convb,
            dt_bias[:, :, None], a_log[:, :, None], dskip_e, gnw, w_out_b, ln1g, ln1b, tl=PROMPT_TL)
        yp = _ffn_call(l, x1.reshape(bsz * seq, D_MODEL), w1_b, w2_b, ln2g, ln2b,
                       tm=PROMPT_TM).reshape(bsz, seq, D_MODEL)
        kp.append(k_l); vp.append(v_l); cp.append(c_l); hp.append(h_l)

        (vn_s, outa, q_s, k_s, v_s, z_s, cnew, xs_s, bm_s, xT, cT, dec) = _s1_call(
            l, ys, w_in_b, lnvg, lnvb, ws0_e, bs0_e, conv_w, convb, cst, dtb_p, alog_p, emat)
        outb, hnew, yT = _s2_call(l, sinks, q_s, k_s, v_s, kst, vst, hst, xT, bm_s, cT, dec)
        x1s = _s3_call(l, yT, xs_s, z_s, dskip_e, gnw, outa, outb, ys, w_out_b, ln1g, ln1b)
        ys = _ffn_call(l, x1s, w1_b, w2_b, ln2g, ln2b, tm=n_s)
        ksm.append(k_s); vsm.append(v_s); csm.append(cnew); hsm.append(hnew); vns.append(vn_s)

    kv_p = (DEPTH, bsz, CHUNK, 2, HEAD_DIM)
    kv_s = (DEPTH, n_s, 1, 2, HEAD_DIM)
    ssm_shape = (SSM_HEADS, HEAD_DIM, D_STATE)
    return (yp, ys.reshape(n_s, 1, D_MODEL),
            jnp.stack(kp).reshape(kv_p), jnp.stack(vp).reshape(kv_p),
            jnp.stack(cp), jnp.stack(hp).reshape((DEPTH, bsz) + ssm_shape),
            jnp.stack(ksm).reshape(kv_s), jnp.stack(vsm).reshape(kv_s),
            jnp.swapaxes(jnp.stack(csm), 1, 2),
            jnp.stack(hsm).reshape((DEPTH, n_s) + ssm_shape),
            jnp.stack(vns).reshape(DEPTH, n_s, 1, A_WIDTH))
```

```python
import functools

import jax
import jax.numpy as jnp
from jax import lax
from jax.experimental import pallas as pl
from jax.experimental.pallas import tpu as pltpu

F32 = jnp.float32
BF16 = jnp.bfloat16

D_MODEL = 1024
DEPTH = 4
HEAD_DIM = 64
A_WIDTH = 256
B_WIDTH = 256
KV_WIDTH = 128
C_WIDTH = 512
CONV_DIM = 1024
CONV_W = 4
SSM_HEADS = 8
D_STATE = 128
D_FF = 4096
CHUNK = 128
ALPHA = (2 * DEPTH) ** 0.25
LN_EPS = 1e-5
RMS_EPS = 1e-6
ATT_SCALE = HEAD_DIM ** -0.5
LOG2E = 1.4426950408889634
Q_SCALE = ATT_SCALE * LOG2E
NEG = -1e30

OFF_AU, OFF_AV, OFF_Q, OFF_K, OFF_V, OFF_Z = 0, 256, 512, 768, 896, 1024
OFF_XBC = 1536
OFF_DT = 2560
D_IN_PAD = 2688
LANES = 128
SUBLANES = 8
VMEM_LIMIT = 56 * 1024 * 1024

NT_DIMS = (((1,), (1,)), ((), ()))
PIECE = 256


def _dot(a, b):
    return jnp.dot(a, b, preferred_element_type=F32)


def _dot_nt(a, b):
    return lax.dot_general(a, b, NT_DIMS, preferred_element_type=F32)


def _layer_norm(x, g, b):
    mu = jnp.mean(x, axis=-1, keepdims=True)
    xc = x - mu
    var = jnp.mean(xc * xc, axis=-1, keepdims=True)
    return xc * lax.rsqrt(var + LN_EPS) * g + b


def _silu(x):
    return x * jax.nn.sigmoid(x)


def _split3(x):
    hi = x.astype(BF16)
    r = x - hi.astype(F32)
    mid = r.astype(BF16)
    lo = (r - mid.astype(F32)).astype(BF16)
    return hi, mid, lo


def _dot_exact_rhs(x, m_bf16):
    hi, mid, lo = _split3(x)
    return _dot(hi, m_bf16) + _dot(mid, m_bf16) + _dot(lo, m_bf16)


def _gated_rmsnorm(y, xs, z, dskip, gnw):
    y2 = (y + dskip * xs) * _silu(z)
    ms = jnp.mean(y2 * y2, axis=-1, keepdims=True)
    return y2 * lax.rsqrt(ms + RMS_EPS) * gnw


def _mixer_kernel(sinks_ref, x_ref, w_in_ref, lnvg_ref, lnvb_ref, ws_ref, bs_ref,
                  convw_ref, convb_ref, dtb_ref, alog_ref, dskip_ref, gnw_ref,
                  w_out_ref, ln1g_ref, ln1b_ref,
                  x1_ref, klast_ref, vlast_ref, convnew_ref, ssm_ref,
                  proj_ref, xbc_ref, act_ref, dtr_ref, mix_ref, kprev_ref, vprev_ref,
                  h_ref, wstk_ref, xb_ref, mo_ref, *, tl, layer):
    t = pl.program_id(1)
    nt = pl.num_programs(1)

    row_i = lax.broadcasted_iota(jnp.int32, (CHUNK, CHUNK), 0)
    col_i = lax.broadcasted_iota(jnp.int32, (CHUNK, CHUNK), 1)
    causal = col_i <= row_i
    lane_lo = col_i < HEAD_DIM

    @pl.when(t == 0)
    def _init():
        kprev_ref[...] = jnp.zeros_like(kprev_ref)
        vprev_ref[...] = jnp.zeros_like(vprev_ref)
        h_ref[...] = jnp.zeros_like(h_ref)
        xbc_ref[0:SUBLANES, :] = jnp.zeros((SUBLANES, CONV_DIM), F32)
        for h in range(4):
            wstk_ref[h * CHUNK:(h + 1) * CHUNK, :] = jnp.where(causal, ws_ref[h], 0.0).astype(BF16)

    half = tl // 2

    def project_pieces(hf):
        rows = slice(hf * half, (hf + 1) * half)

        def cast():
            xb_ref[rows, :] = x_ref[0, rows, :].astype(BF16)

        def main(c0):
            def run():
                proj_ref[rows, c0:c0 + PIECE] = _dot(xb_ref[rows, :], w_in_ref[:, c0:c0 + PIECE])
            return run

        def xbc(c0):
            def run():
                xbc_ref[SUBLANES + hf * half:SUBLANES + (hf + 1) * half, c0:c0 + PIECE] = _dot(
                    xb_ref[rows, :], w_in_ref[:, OFF_XBC + c0:OFF_XBC + c0 + PIECE])
            return run

        def dt():
            dtr_ref[rows, :] = _dot(xb_ref[rows, :], w_in_ref[:, OFF_DT:D_IN_PAD])

        return ([cast] + [main(c0) for c0 in range(0, OFF_XBC, PIECE)]
                + [xbc(c0) for c0 in range(0, CONV_DIM, PIECE)] + [dt])

    def conv(hf):
        xh = xbc_ref[hf * half:hf * half + SUBLANES + half, :]
        acc = xh * convw_ref[0:1, :]
        for i in range(1, CONV_W):
            acc = pltpu.roll(acc, 1, 0) + xh * convw_ref[i:i + 1, :]
        act_ref[hf * half:(hf + 1) * half, :] = _silu(acc[SUBLANES:] + convb_ref[...])

    def out_pieces(hf):
        rows = slice(hf * half, (hf + 1) * half)

        def part(c0):
            def run():
                mo_ref[rows, c0:c0 + PIECE] = _dot(mix_ref[rows, :], w_out_ref[:, c0:c0 + PIECE])
            return run

        def norm():
            x1_ref[0, rows, :] = _layer_norm(ALPHA * x_ref[0, rows, :] + mo_ref[rows, :],
                                             ln1g_ref[...], ln1b_ref[...])

        return [part(c0) for c0 in range(0, D_MODEL, PIECE)] + [norm]

    triu_b = jnp.where(row_i <= col_i, 1.0, 0.0).astype(BF16)
    a_col = -jnp.exp(alog_ref[...])
    zpad = jnp.zeros((CHUNK - SSM_HEADS, CHUNK), F32)
    lane256 = lax.broadcasted_iota(jnp.int32, (CHUNK, A_WIDTH), 1)
    r256 = lax.broadcasted_iota(jnp.int32, (2 * CHUNK, 2 * CHUNK), 0)
    c256 = lax.broadcasted_iota(jnp.int32, (2 * CHUNK, 2 * CHUNK), 1)
    t256 = jnp.bitwise_and(r256, CHUNK - 1)
    win_mask = (c256 > t256) & (c256 <= t256 + CHUNK)
    rcol = lax.broadcasted_iota(jnp.int32, (2 * CHUNK, 1), 0)
    sink_a = jnp.where(rcol < CHUNK, sinks_ref[layer, 0], sinks_ref[layer, 3]) * LOG2E
    sink_b = jnp.where(rcol < CHUNK, sinks_ref[layer, 1], sinks_ref[layer, 2]) * LOG2E

    def chunk_a(c):
        rows = slice(c * CHUNK, (c + 1) * CHUNK)
        u = jax.nn.gelu(proj_ref[rows, OFF_AU:OFF_AU + A_WIDTH])
        vn = _layer_norm(jax.nn.gelu(proj_ref[rows, OFF_AV:OFF_AV + A_WIDTH]),
                         lnvg_ref[...], lnvb_ref[...])
        pa = _dot(wstk_ref[...], vn.astype(BF16))
        mix_a = jnp.where(
            lane256 < HEAD_DIM, pa[0:CHUNK],
            jnp.where(lane256 < 2 * HEAD_DIM, pa[CHUNK:2 * CHUNK],
                      jnp.where(lane256 < 3 * HEAD_DIM, pa[2 * CHUNK:3 * CHUNK], pa[3 * CHUNK:])))
        mix_ref[rows, 0:A_WIDTH] = (u * (mix_a + bs_ref[...])).astype(BF16)

    def chunk_b(c):
        rows = slice(c * CHUNK, (c + 1) * CHUNK)
        q01 =proj_ref[rows, OFF_Q:OFF_Q + LANES] * Q_SCALE
        q23 = proj_ref[rows, OFF_Q + LANES:OFF_Q + 2 * LANES] * Q_SCALE
        kc = proj_ref[rows, OFF_K:OFF_K + KV_WIDTH]
        vc = proj_ref[rows, OFF_V:OFF_V + KV_WIDTH]
        if c == 0:
            kp, vp = kprev_ref[...], vprev_ref[...]
        else:
            prev = slice((c - 1) * CHUNK, c * CHUNK)
            kp = proj_ref[prev, OFF_K:OFF_K + KV_WIDTH]
            vp = proj_ref[prev, OFF_V:OFF_V + KV_WIDTH]
        k2 = jnp.concatenate([kp, kc], axis=0)
        v2 = jnp.concatenate([vp, vc], axis=0)
        k2b = k2.astype(BF16)
        v2b = v2.astype(BF16)
        k2rb = pltpu.roll(k2, HEAD_DIM, 1).astype(BF16)
        v2rb = pltpu.roll(v2, HEAD_DIM, 1).astype(BF16)
        q_a = jnp.concatenate([jnp.where(lane_lo, q01, 0.0), jnp.where(lane_lo, 0.0, q23)], axis=0)
        q_b = jnp.concatenate([jnp.where(lane_lo, 0.0, q01), jnp.where(lane_lo, q23, 0.0)], axis=0)
        vis = win_mask & (c256 >= jnp.where(t == 0, CHUNK, 0)) if c == 0 else win_mask

        def attend(qm, kk, vv, sink2):
            s = jnp.where(vis, _dot_nt(qm.astype(BF16), kk), NEG)
            m = jnp.maximum(jnp.max(s, axis=-1, keepdims=True), sink2)
            p = jnp.exp2(s - m)
            den = jnp.sum(p, axis=-1, keepdims=True) + jnp.exp2(sink2 - m)
            return _dot(p.astype(BF16), vv) * (1.0 / den)

        o_a = attend(q_a, k2b, v2b, sink_a)
        o_b = attend(q_b, k2rb, v2rb, sink_b)
        mix_ref[rows, A_WIDTH:A_WIDTH + LANES] = jnp.where(
            lane_lo, o_a[0:CHUNK], o_b[0:CHUNK]).astype(BF16)
        mix_ref[rows, A_WIDTH + LANES:A_WIDTH + 2 * LANES] = jnp.where(
            lane_lo, o_b[CHUNK:], o_a[CHUNK:]).astype(BF16)

    def chunk_c(c):
        rows = slice(c * CHUNK, (c + 1) * CHUNK)
        xs = act_ref[rows, 0:C_WIDTH]
        bm = act_ref[rows, C_WIDTH:C_WIDTH + 2 * D_STATE]
        cm = act_ref[rows, C_WIDTH + 2 * D_STATE:CONV_DIM]
        z = proj_ref[rows, OFF_Z:OFF_Z + C_WIDTH]
        dt_row = jax.nn.softplus(dtr_ref[rows, :].T[0:SSM_HEADS] + dtb_ref[...])
        cum_row = _dot_exact_rhs(dt_row * a_col, triu_b) * LOG2E
        cum = jnp.concatenate([cum_row, zpad], axis=0).T
        toend_row = jnp.exp2(cum_row[:, CHUNK - 1:CHUNK] - cum_row) * dt_row
        ecl = jnp.exp2(cum_row[:, CHUNK - 1:CHUNK])
        xt = xs.T
        yts = []
        for g in range(2):
            bg = bm[:, g * D_STATE:(g + 1) * D_STATE]
            cg = cm[:, g * D_STATE:(g + 1) * D_STATE]
            cb = _dot_nt(cg.astype(BF16), bg.astype(BF16))
            xw = []
            for hh in range(4):
                h = g * 4 + hh
                hs = slice(h * HEAD_DIM, (h + 1) * HEAD_DIM)
                cc = jnp.broadcast_to(cum[:, h:h + 1], (CHUNK, CHUNK))
                dec = jnp.exp2(jnp.where(causal, cc - cum_row[h:h + 1, :], NEG))
                m_h = cb * dec * dt_row[h:h + 1, :]
                ce_h = cg * jnp.exp2(cc)
                l_h = jnp.concatenate([m_h, ce_h], axis=1).astype(BF16)
                r_h = jnp.concatenate([xt[hs], h_ref[hs, :]], axis=1).astype(BF16)
                yts.append(_dot_nt(r_h, l_h))
                xw.append(xt[hs] * toend_row[h:h + 1, :])
            gs = slice(g * 4 * HEAD_DIM, (g + 1) * 4 * HEAD_DIM)
            upd = _dot(jnp.concatenate(xw, axis=0).astype(BF16), bg.astype(BF16))
            keep = jnp.concatenate(
                [jnp.broadcast_to(ecl[g * 4 + hh:g * 4 + hh + 1, :], (HEAD_DIM, D_STATE))
                 for hh in range(4)], axis=0)
            h_ref[gs, :] = h_ref[gs, :] * keep + upd
        y = jnp.concatenate(yts, axis=0).T
        out_c = _gated_rmsnorm(y, xs, z, dskip_ref[...], gnw_ref[...])
        mix_ref[rows, A_WIDTH + B_WIDTH:D_MODEL] = out_c.astype(BF16)

    cph = half // CHUNK

    for hf in range(2):
        for piece in project_pieces(hf):
            piece()
        conv(hf)
    for hf in range(2):
        for c in range(hf * cph, (hf + 1) * cph):
            chunk_a(c)
            chunk_b(c)
            chunk_c(c)
        for piece in out_pieces(hf):
            piece()

    last = slice(tl - CHUNK, tl)
    kprev_ref[...] = proj_ref[last, OFF_K:OFF_K + KV_WIDTH]
    vprev_ref[...] = proj_ref[last, OFF_V:OFF_V + KV_WIDTH]
    tail = xbc_ref[SUBLANES + tl - (CONV_W - 1):SUBLANES + tl, :]
    xbc_ref[SUBLANES - (CONV_W - 1):SUBLANES, :] = tail

    @pl.when(t == nt - 1)
    def _final():
        klast_ref[0] = kprev_ref[...]
        vlast_ref[0] = vprev_ref[...]
        convnew_ref[0] = tail
        ssm_ref[0] = h_ref[...]


def _full(shape):
    n = len(shape)
    return pl.BlockSpec(shape, lambda *_: (0,) * n)


def _layer(shape, l, single_buffer=False):
    n = len(shape)
    mode = pl.Buffered(1) if single_buffer else None
    return pl.BlockSpec((None,) + tuple(shape), lambda *_: (l,) + (0,) * n, pipeline_mode=mode)


def _mixer_call(l, x, sinks, w_in, lnvg, lnvb, ws, bs_exp, convw, convb, dtb, alog, dskip, gnw,
                w_out, ln1g, ln1b, *, tl):
    bsz, seq, _ = x.shape
    nt = seq // tl
    kern = functools.partial(_mixer_kernel, tl=tl, layer=l)
    in_specs = [
        pl.BlockSpec(memory_space=pltpu.SMEM),
        pl.BlockSpec((1, tl, D_MODEL), lambda b, t: (b, t, 0)),
        _layer((D_MODEL, D_IN_PAD), l, single_buffer=True),
        _layer((1, A_WIDTH), l), _layer((1, A_WIDTH), l),
        _layer((4, CHUNK, CHUNK), l), _layer((CHUNK, A_WIDTH), l),
        _layer((CONV_W, CONV_DIM), l), _layer((1, CONV_DIM), l),
        _layer((SSM_HEADS, 1), l), _layer((SSM_HEADS, 1), l),
        _layer((1, C_WIDTH), l), _layer((1, C_WIDTH), l),
        _layer((D_MODEL, D_MODEL), l, single_buffer=True),
        _layer((1, D_MODEL), l), _layer((1, D_MODEL), l),
    ]
    out_shape = (
        jax.ShapeDtypeStruct((bsz, seq, D_MODEL), F32),
        jax.ShapeDtypeStruct((bsz, CHUNK, KV_WIDTH), F32),
        jax.ShapeDtypeStruct((bsz, CHUNK, KV_WIDTH), F32),
        jax.ShapeDtypeStruct((bsz, CONV_W - 1, CONV_DIM), F32),
        jax.ShapeDtypeStruct((bsz, C_WIDTH, D_STATE), F32),
    )
    out_specs = (
        pl.BlockSpec((1, tl, D_MODEL), lambda b, t: (b, t, 0)),
        pl.BlockSpec((1, CHUNK, KV_WIDTH), lambda b, t: (b, 0, 0)),
        pl.BlockSpec((1, CHUNK, KV_WIDTH), lambda b, t: (b, 0, 0)),
        pl.BlockSpec((1, CONV_W - 1, CONV_DIM), lambda b, t: (b, 0, 0)),
        pl.BlockSpec((1, C_WIDTH, D_STATE), lambda b, t: (b, 0, 0)),
    )
    scratch = [
        pltpu.VMEM((tl, OFF_XBC), F32),
        pltpu.VMEM((SUBLANES + tl, CONV_DIM), F32),
        pltpu.VMEM((tl, CONV_DIM), F32),
        pltpu.VMEM((tl, LANES), F32),
        pltpu.VMEM((tl, D_MODEL), BF16),
        pltpu.VMEM((CHUNK, KV_WIDTH), F32),
        pltpu.VMEM((CHUNK, KV_WIDTH), F32),
        pltpu.VMEM((C_WIDTH, D_STATE), F32),
        pltpu.VMEM((4 * CHUNK, CHUNK), BF16),
        pltpu.VMEM((tl, D_MODEL), BF16),
        pltpu.VMEM((tl, D_MODEL), F32),
    ]
    return pl.pallas_call(
        kern, out_shape=out_shape, grid=(bsz, nt), in_specs=in_specs, out_specs=out_specs,
        scratch_shapes=scratch, name="prompt_mixer",
        compiler_params=pltpu.CompilerParams(
            dimension_semantics=("arbitrary", "arbitrary"), vmem_limit_bytes=VMEM_LIMIT),
    )(sinks, x, w_in, lnvg, lnvb, ws, bs_exp, convw, convb, dtb, alog, dskip, gnw, w_out, ln1g, ln1b)


FF_CHUNK = 1024


def _ffn_kernel(x_ref, w1_ref, w2_ref, g_ref, b_ref, o_ref):
    x = x_ref[...]
    xb = x.astype(BF16)
    acc = jnp.zeros(x.shape, F32)
    for c in range(D_FF // FF_CHUNK):
        sl = slice(c * FF_CHUNK, (c + 1) * FF_CHUNK)
        h = jnp.maximum(_dot(xb, w1_ref[:, sl]), 0.0)
        acc = acc + _dot((h * h).astype(BF16), w2_ref[sl, :])
    o_ref[...] = _layer_norm(ALPHA * x + acc, g_ref[...], b_ref[...])


def _ffn_call(l, x2d, w1, w2, g, b, *, tm):
    rows = x2d.shape[0]
    return pl.pallas_call(
        _ffn_kernel, out_shape=jax.ShapeDtypeStruct((rows, D_MODEL), F32),
        grid=(rows // tm,),
        in_specs=[pl.BlockSpec((tm, D_MODEL), lambda i: (i, 0)),
                  _layer((D_MODEL, D_FF), l, single_buffer=True),
                  _layer((D_FF, D_MODEL), l, single_buffer=True),
                  _layer((1, D_MODEL), l), _layer((1, D_MODEL), l)],
        out_specs=pl.BlockSpec((tm, D_MODEL), lambda i: (i, 0)),
        name="ffn",
        compiler_params=pltpu.CompilerParams(
            dimension_semantics=("arbitrary",), vmem_limit_bytes=VMEM_LIMIT),
    )(x2d, w1, w2, g, b)


def _s1_kernel(x_ref, w_in_ref, lnvg_ref, lnvb_ref, ws0_ref, bs0_ref, convw_ref, convb_ref,
               cst_ref, dtb_ref, alog_ref, e_ref,
               vn_ref, outa_ref, q_ref, k_ref, v_ref, z_ref, cnew_ref, xs_ref, bm_ref,
               xT_ref, cT_ref, dec_ref):
    xb = x_ref[...].astype(BF16)
    proj = _dot(xb, w_in_ref[...])
    u = jax.nn.gelu(proj[:, OFF_AU:OFF_AU + A_WIDTH])
    vn = _layer_norm(jax.nn.gelu(proj[:, OFF_AV:OFF_AV + A_WIDTH]), lnvg_ref[...], lnvb_ref[...])
    vn_ref[...] = vn
    outa_ref[...] = u * (ws0_ref[...] * vn + bs0_ref[...])
    q_ref[...] = proj[:, OFF_Q:OFF_Q + B_WIDTH]
    k_ref[...] = proj[:, OFF_K:OFF_K + KV_WIDTH]
    v_ref[...] = proj[:, OFF_V:OFF_V + KV_WIDTH]
    z_ref[...] = proj[:, OFF_Z:OFF_Z + C_WIDTH]
    xbc = proj[:, OFF_XBC:OFF_DT]
    acc = convb_ref[...] + xbc * convw_ref[CONV_W - 1:CONV_W, :]
    for i in range(CONV_W - 1):
        acc = acc + cst_ref[i] * convw_ref[i:i + 1, :]
    cnew_ref[0] = cst_ref[1]
    cnew_ref[1] = cst_ref[2]
    cnew_ref[2] = xbc
    act = _silu(acc)
    xs = act[:, 0:C_WIDTH]
    xs_ref[...] = xs
    bm_ref[...] = act[:, C_WIDTH:C_WIDTH + 2 * D_STATE]
    cT_ref[...] = act[:, C_WIDTH + 2 * D_STATE:CONV_DIM].T
    dt = jax.nn.softplus(proj[:, OFF_DT:D_IN_PAD] + dtb_ref[...])
    dec_ref[...] = jnp.exp(dt * (-jnp.exp(alog_ref[...])))
    dt_exp = _dot_exact_rhs(dt, e_ref[...])
    xT_ref[...] = (xs * dt_exp).T.astype(BF16)


S2_TB = SUBLANES


def _s2_kernel(sinks_ref, q_ref, kn_ref, vn_ref, kst_ref, vst_ref, h_ref, xT_ref, bm_ref,
               cT_ref, dec_ref, ob_ref, hnew_ref, yT_ref, *, layer, step):
    i = step
    tb = S2_TB
    lane = lax.broadcasted_iota(jnp.int32, (tb, LANES), 1)
    lo = lane < HEAD_DIM

    q01 = q_ref[:, 0:LANES]
    q23 = q_ref[:, LANES:2 * LANES]
    qh = jnp.concatenate([
        jnp.where(lo, q01, 0.0),
        jnp.where(lo, pltpu.roll(q01, HEAD_DIM, 1), 0.0),
        jnp.where(lo, 0.0, pltpu.roll(q23, HEAD_DIM, 1)),
        jnp.where(lo, 0.0, q23)], axis=0)
    qhb = qh.astype(BF16)
    rowi = lax.broadcasted_iota(jnp.int32, (4 * tb, LANES), 0)
    coli = lax.broadcasted_iota(jnp.int32, (4 * tb, LANES), 1)
    tok = jnp.bitwise_and(rowi, tb - 1)
    s = jnp.zeros((4 * tb, LANES), F32)
    for j in range(tb):
        sj = _dot_nt(qhb, kst_ref[j].astype(BF16))
        s = jnp.where(tok == j, sj, s)
    s = jnp.where(coli >= 1, s * ATT_SCALE, NEG)
    kn4 = jnp.concatenate([kn_ref[...]] * 4, axis=0)
    vn4 = jnp.concatenate([vn_ref[...]] * 4, axis=0)
    s_new = jnp.sum(qh * kn4, axis=-1, keepdims=True) * ATT_SCALE
    head = jnp.right_shift(rowi[:, 0:1], 3)
    sink = jnp.where(head == 0, sinks_ref[layer, 0],
                     jnp.where(head == 1, sinks_ref[layer, 1],
                               jnp.where(head == 2, sinks_ref[layer, 2], sinks_ref[layer, 3])))
    m = jnp.maximum(jnp.maximum(jnp.max(s, axis=-1, keepdims=True), s_new), sink)
    p = jnp.exp(s - m)
    p_new = jnp.exp(s_new - m)
    inv = 1.0 / (jnp.sum(p, axis=-1, keepdims=True) + p_new + jnp.exp(sink - m))
    p = p * inv
    o = (p_new * inv) * vn4
    for j in range(tb):
        pj = jnp.where(tok == j, p, 0.0).astype(BF16)
        o = o + _dot(pj, vst_ref[j].astype(BF16))
    ob_ref[:, 0:LANES] = jnp.where(lo, o[0:tb], pltpu.roll(o[tb:2 * tb], HEAD_DIM, 1))
    ob_ref[:, LANES:2 * LANES] = jnp.where(lo, pltpu.roll(o[2 * tb:3 * tb], HEAD_DIM, 1), o[3 * tb:])

    @pl.when(i == 0)
    def _():
        yT_ref[...] = jnp.zeros_like(yT_ref)

    r128 = lax.broadcasted_iota(jnp.int32, (LANES, LANES), 0)
    c128 = lax.broadcasted_iota(jnp.int32, (LANES, LANES), 1)
    hp_g = C_WIDTH // 2
    for j in range(tb):
        b = i * tb + j
        brow = bm_ref[pl.ds(b, 1), :]
        drow = dec_ref[pl.ds(b, 1), :]
        for g in range(2):
            dm = jnp.where(r128 == b,
                           jnp.broadcast_to(brow[:, g * D_STATE:(g + 1) * D_STATE], (LANES, D_STATE)),
                           0.0).astype(BF16)
            upd = _dot(xT_ref[g * hp_g:(g + 1) * hp_g, :], dm)
            parts = []
            for hh in range(4):
                h = g * 4 + hh
                rs = slice(h * HEAD_DIM, (h + 1) * HEAD_DIM)
                hn = (h_ref[j, rs, :] * jnp.broadcast_to(drow[:, h:h + 1], (HEAD_DIM, D_STATE))
                      + upd[hh * HEAD_DIM:(hh + 1) * HEAD_DIM])
                hnew_ref[j, rs, :] = hn
                parts.append(hn)
            cmm = jnp.where(c128 == b, cT_ref[g * D_STATE:(g + 1) * D_STATE, :], 0.0).astype(BF16)
            yT_ref[g * hp_g:(g + 1) * hp_g, :] += _dot(jnp.concatenate(parts, axis=0).astype(BF16), cmm)


def _s3_kernel(yT_ref, xs_ref, z_ref, dskip_ref, gnw_ref, outa_ref, outb_ref, x_ref, w_out_ref,
               g_ref, b_ref, x1_ref):
    out_c = _gated_rmsnorm(yT_ref[...].T, xs_ref[...], z_ref[...], dskip_ref[...], gnw_ref[...])
    mix = jnp.concatenate([outa_ref[...], outb_ref[...], out_c], axis=1).astype(BF16)
    x1_ref[...] = _layer_norm(ALPHA * x_ref[...] + _dot(mix, w_out_ref[...]), g_ref[...], b_ref[...])


def _sample_kernel(sinks_ref, x_ref, w_in_ref, lnvg_ref, lnvb_ref, ws0_ref, bs0_ref, convw_ref,
                   convb_ref, cst_ref, dtb_ref, alog_ref, e_ref, kst_ref, vst_ref, h_ref,
                   dskip_ref, gnw_ref, w_out_ref, ln1g_ref, ln1b_ref, w1_ref, w2_ref, ln2g_ref, ln2b_ref,
                   ys_ref, vns_ref, ksm_ref, vsm_ref, csm_ref, hnew_ref,
                   res_scr, outa_scr, q_scr, z_scr, xs_scr, bm_scr, xT_scr, cT_scr, dec_scr,
                   outb_scr, yT_scr, x1_scr):
    l = pl.program_id(0)
    i = pl.program_id(1)
    last = pl.num_programs(1) - 1

    @pl.when(jnp.logical_and(l == 0, i == 0))
    def _():
        res_scr[...] = x_ref[...]

    @pl.when(i == 0)
    def _():
        _s1_kernel(res_scr, w_in_ref, lnvg_ref, lnvb_ref, ws0_ref, bs0_ref, convw_ref, convb_ref,
                   cst_ref, dtb_ref, alog_ref, e_ref,
                   vns_ref, outa_scr, q_scr, ksm_ref, vsm_ref, z_scr, csm_ref, xs_scr, bm_scr,
                   xT_scr, cT_scr, dec_scr)

    rows = pl.ds(pl.multiple_of(i * S2_TB, S2_TB), S2_TB)
    _s2_kernel(sinks_ref, q_scr.at[rows], ksm_ref.at[rows], vsm_ref.at[rows], kst_ref, vst_ref,
               h_ref, xT_scr, bm_scr, cT_scr, dec_scr, outb_scr.at[rows], hnew_ref, yT_scr,
               layer=l, step=i)

    @pl.when(i == last)
    def _():
        _s3_kernel(yT_scr, xs_scr, z_scr, dskip_ref, gnw_ref, outa_scr, outb_scr, res_scr,
                   w_out_ref, ln1g_ref, ln1b_ref, x1_scr)
        _ffn_kernel(x1_scr, w1_ref, w2_ref, ln2g_ref, ln2b_ref, res_scr)
        ys_ref[...] = res_scr[...]


def _sample_call(x, sinks, w_in, lnvg, lnvb, ws0, bs0, convw, convb, cst, dtb, alog, emat,
                 kst, vst, hst, dskip, gnw, w_out, ln1g, ln1b, w1, w2, ln2g, ln2b):
    n = x.shape[0]
    tb = S2_TB

    def per_layer(shape, single_buffer=False):
        k = len(shape)
        mode = pl.Buffered(1) if single_buffer else None
        return pl.BlockSpec((None,) + tuple(shape), lambda l, i: (l,) + (0,) * k, pipeline_mode=mode)

    def per_block(shape):
        k = len(shape) - 1
        return pl.BlockSpec((None,) + tuple(shape), lambda l, i: (l, i) + (0,) * k)

    in_specs = [
        pl.BlockSpec(memory_space=pltpu.SMEM),
        _full((n, D_MODEL)),
        per_layer((D_MODEL, D_IN_PAD), True),
        per_layer((1, A_WIDTH)), per_layer((1, A_WIDTH)), per_layer((1, A_WIDTH)), per_layer((1, A_WIDTH)),
        per_layer((CONV_W, CONV_DIM)), per_layer((1, CONV_DIM)), per_layer((CONV_W - 1, n, CONV_DIM)),
        per_layer((1, LANES)), per_layer((1, LANES)), _full((LANES, C_WIDTH)),
        per_block((tb, CHUNK, KV_WIDTH)), per_block((tb, CHUNK, KV_WIDTH)),
        per_block((tb, C_WIDTH, D_STATE)),
        per_layer((1, C_WIDTH)), per_layer((1, C_WIDTH)),
        per_layer((D_MODEL, D_MODEL), True), per_layer((1, D_MODEL)), per_layer((1, D_MODEL)),
        per_layer((D_MODEL, D_FF), True), per_layer((D_FF, D_MODEL), True),
        per_layer((1, D_MODEL)), per_layer((1, D_MODEL)),
    ]
    out_shape = (
        jax.ShapeDtypeStruct((n, D_MODEL), F32),
        jax.ShapeDtypeStruct((DEPTH, n, A_WIDTH), F32),
        jax.ShapeDtypeStruct((DEPTH, n, KV_WIDTH), F32),
        jax.ShapeDtypeStruct((DEPTH, n, KV_WIDTH), F32),
        jax.ShapeDtypeStruct((DEPTH, CONV_W - 1, n, CONV_DIM), F32),
        jax.ShapeDtypeStruct((DEPTH, n, C_WIDTH, D_STATE), F32),
    )
    out_specs = (
        _full((n, D_MODEL)),
        per_layer((n, A_WIDTH)), per_layer((n, KV_WIDTH)), per_layer((n, KV_WIDTH)),
        per_layer((CONV_W - 1, n, CONV_DIM)),
        per_block((tb, C_WIDTH, D_STATE)),
    )
    scratch = [
        pltpu.VMEM((n, D_MODEL), F32),
        pltpu.VMEM((n, A_WIDTH), F32),
        pltpu.VMEM((n, B_WIDTH), F32),
        pltpu.VMEM((n, C_WIDTH), F32),
        pltpu.VMEM((n, C_WIDTH), F32),
        pltpu.VMEM((n, 2 * D_STATE), F32),
        pltpu.VMEM((C_WIDTH, n), BF16),
        pltpu.VMEM((2 * D_STATE, n), F32),
        pltpu.VMEM((n, LANES), F32),
        pltpu.VMEM((n, B_WIDTH), F32),
        pltpu.VMEM((C_WIDTH, n), F32),
        pltpu.VMEM((n, D_MODEL), F32),
    ]
    return pl.pallas_call(
        _sample_kernel, out_shape=out_shape, grid=(DEPTH, n // tb), in_specs=in_specs,
        out_specs=out_specs, scratch_shapes=scratch, name="sample_step",
        compiler_params=pltpu.CompilerParams(
            dimension_semantics=("arbitrary", "arbitrary"), vmem_limit_bytes=VMEM_LIMIT),
    )(sinks, x, w_in, lnvg, lnvb, ws0, bs0, convw, convb, cst, dtb, alog, emat, kst, vst, hst,
      dskip, gnw, w_out, ln1g, ln1b, w1, w2, ln2g, ln2b)


PROMPT_TL = 1024
PROMPT_TM = 1024


def kernel(x_prompt, x_sample, state_attn_k, state_attn_v, state_conv, state_ssm, w_in, ln_v_g,
           ln_v_b, w_s, b_s, sinks, conv_w, conv_b, dt_bias, a_log, d_skip, gn_w, w_out, ln1_g,
           ln1_b, w1, w2, ln2_g, ln2_b):
    bsz, seq, _ = x_prompt.shape
    n_s = x_sample.shape[0]
    d_in = w_in.shape[-1]

    w_in_b = jnp.pad(w_in, ((0, 0), (0, 0), (0, D_IN_PAD - d_in))).astype(BF16)
    w_out_b = w_out.astype(BF16)
    w1_b = w1.astype(BF16)
    w2_b = w2.astype(BF16)
    pad_h = ((0, 0), (0, LANES - SSM_HEADS))
    dtb_p = jnp.pad(dt_bias, pad_h)[:, None, :]
    alog_p = jnp.pad(a_log, pad_h)[:, None, :]
    dskip_e = jnp.repeat(d_skip, HEAD_DIM, axis=-1)[:, None, :]
    bs_e = jnp.repeat(jnp.swapaxes(b_s, 1, 2), HEAD_DIM, axis=-1)
    ws0_e = jnp.repeat(w_s[:, :, 0, 0], HEAD_DIM, axis=-1)[:, None, :]
    bs0_e = jnp.repeat(b_s[:, :, 0], HEAD_DIM, axis=-1)[:, None, :]
    emat = (lax.broadcasted_iota(jnp.int32, (LANES, C_WIDTH), 0)
            == lax.broadcasted_iota(jnp.int32, (LANES, C_WIDTH), 1) // HEAD_DIM).astype(BF16)
    row = lambda a: a[:, None, :]
    lnvg, lnvb, convb, gnw = row(ln_v_g), row(ln_v_b), row(conv_b), row(gn_w)
    ln1g, ln1b, ln2g, ln2b = row(ln1_g), row(ln1_b), row(ln2_g), row(ln2_b)
    kst = state_attn_k.reshape(DEPTH, n_s, CHUNK, KV_WIDTH)
    vst = state_attn_v.reshape(DEPTH, n_s, CHUNK, KV_WIDTH)
    hst = state_ssm.reshape(DEPTH, n_s, C_WIDTH, D_STATE)
    cst = jnp.swapaxes(state_conv, 1, 2)

    ys, vns, ksm, vsm, csm, hsm = _sample_call(
        x_sample.reshape(n_s, D_MODEL), sinks, w_in_b, lnvg, lnvb, ws0_e, bs0_e, conv_w, convb, cst,
        dtb_p, alog_p, emat, kst, vst, hst, dskip_e, gnw, w_out_b, ln1g, ln1b, w1_b, w2_b, ln2g, ln2b)

    yp = x_prompt
    kp, vp, cp, hp = [], [], [], []
    for l in range(DEPTH):
        x1, k_l, v_l, c_l, h_l = _mixer_call(
            l, yp, sinks, w_in_b, lnvg, lnvb, w_s, bs_e, conv_w, convb,
            dt_bias[:, :, None], a_log[:, :, None], dskip_e, gnw, w_out_b, ln1g, ln1b, tl=PROMPT_TL)
        yp = _ffn_call(l, x1.reshape(bsz * seq, D_MODEL), w1_b, w2_b, ln2g, ln2b,
                       tm=PROMPT_TM).reshape(bsz, seq, D_MODEL)
        kp.append(k_l); vp.append(v_l); cp.append(c_l); hp.append(h_l)

    kv_p = (DEPTH, bsz, CHUNK, 2, HEAD_DIM)
    kv_s = (DEPTH, n_s, 1, 2, HEAD_DIM)
    ssm_shape = (SSM_HEADS, HEAD_DIM, D_STATE)
    return (yp, ys.reshape(n_s, 1, D_MODEL),
            jnp.stack(kp).reshape(kv_p), jnp.stack(vp).reshape(kv_p),
            jnp.stack(cp), jnp.stack(hp).reshape((DEPTH, bsz) + ssm_shape),
            ksm.reshape(kv_s), vsm.reshape(kv_s),
            jnp.swapaxes(csm, 1, 2),
            hsm.reshape((DEPTH, n_s) + ssm_shape),
            vns.reshape(DEPTH, n_s, 1, A_WIDTH))
```

```python
import functools

import jax
import jax.numpy as jnp
from jax import lax
from jax.experimental import pallas as pl
from jax.experimental.pallas import tpu as pltpu

F32 = jnp.float32
BF16 = jnp.bfloat16

D_MODEL = 1024
DEPTH = 4
HEAD_DIM = 64
A_WIDTH = 256
B_WIDTH = 256
KV_WIDTH = 128
C_WIDTH = 512
CONV_DIM = 1024
CONV_W = 4
SSM_HEADS = 8
D_STATE = 128
D_FF = 4096
CHUNK = 128
ALPHA = (2 * DEPTH) ** 0.25
LN_EPS = 1e-5
RMS_EPS = 1e-6
ATT_SCALE = HEAD_DIM ** -0.5
LOG2E = 1.4426950408889634
Q_SCALE = ATT_SCALE * LOG2E
NEG = -1e30

OFF_AU, OFF_AV, OFF_Q, OFF_K, OFF_V, OFF_Z = 0, 256, 512, 768, 896, 1024
OFF_XBC = 1536
OFF_DT = 2560
D_IN_PAD = 2688
LANES = 128
SUBLANES = 8
VMEM_LIMIT = 56 * 1024 * 1024

NT_DIMS = (((1,), (1,)), ((), ()))
PIECE = 256


def _dot(a, b):
    return jnp.dot(a, b, preferred_element_type=F32)


def _dot_nt(a, b):
    return lax.dot_general(a, b, NT_DIMS, preferred_element_type=F32)


def _layer_norm(x, g, b):
    mu = jnp.mean(x, axis=-1, keepdims=True)
    xc = x - mu
    var = jnp.mean(xc * xc, axis=-1, keepdims=True)
    return xc * lax.rsqrt(var + LN_EPS) * g + b


def _silu(x):
    return x * jax.nn.sigmoid(x)


def _split3(x):
    hi = x.astype(BF16)
    r = x - hi.astype(F32)
    mid = r.astype(BF16)
    lo = (r - mid.astype(F32)).astype(BF16)
    return hi, mid, lo


def _dot_exact_rhs(x, m_bf16):
    hi, mid, lo = _split3(x)
    return _dot(hi, m_bf16) + _dot(mid, m_bf16) + _dot(lo, m_bf16)


def _gated_rmsnorm(y, xs, gate, dskip, gnw):
    y2 = (y + dskip * xs) * gate
    ms = jnp.mean(y2 * y2, axis=-1, keepdims=True)
    return y2 * lax.rsqrt(ms + RMS_EPS) * gnw


def _mixer_kernel(sinks_ref, x_ref, w_in_ref, lnvg_ref, lnvb_ref, ws_ref, bs_ref,
                  convw_ref, convb_ref, dtb_ref, alog_ref, dskip_ref, gnw_ref,
                  w_out_ref, ln1g_ref, ln1b_ref,
                  x1_ref, klast_ref, vlast_ref, convnew_ref, ssm_ref,
                  proj_ref, xbc_ref, act_ref, dtr_ref, mix_ref, kprev_ref, vprev_ref,
                  h_ref, wstk_ref, xb_ref, mo_ref, *, tl, layer):
    t = pl.program_id(1)
    nt = pl.num_programs(1)

    row_i = lax.broadcasted_iota(jnp.int32, (CHUNK, CHUNK), 0)
    col_i = lax.broadcasted_iota(jnp.int32, (CHUNK, CHUNK), 1)
    causal = col_i <= row_i
    lane_lo = col_i < HEAD_DIM

    @pl.when(t == 0)
    def _init():
        kprev_ref[...] = jnp.zeros_like(kprev_ref)
        vprev_ref[...] = jnp.zeros_like(vprev_ref)
        h_ref[...] = jnp.zeros_like(h_ref)
        xbc_ref[0:SUBLANES, :] = jnp.zeros((SUBLANES, CONV_DIM), F32)
        for h in range(4):
            wstk_ref[h * CHUNK:(h + 1) * CHUNK, :] = jnp.where(causal, ws_ref[h], 0.0).astype(BF16)

    half = tl // 2

    def project_pieces(hf):
        rows = slice(hf * half, (hf + 1) * half)

        def cast():
            xb_ref[rows, :] = x_ref[0, rows, :].astype(BF16)

        epilogue = {
            OFF_AU: jax.nn.gelu,
            OFF_AV: lambda r: _layer_norm(jax.nn.gelu(r), lnvg_ref[...], lnvb_ref[...]),
            OFF_Q: lambda r: r * Q_SCALE,
            OFF_Z: _silu,
            OFF_Z + PIECE: _silu,
        }

        def main(c0):
            def run():
                res = _dot(xb_ref[rows, :], w_in_ref[:, c0:c0 + PIECE])
                proj_ref[rows, c0:c0 + PIECE] = epilogue.get(c0, lambda r: r)(res)
            return run

        def xbc(c0):
            def run():
                cols = slice(c0, c0 + PIECE)
                res = _dot(xb_ref[rows, :], w_in_ref[:, OFF_XBC + c0:OFF_XBC + c0 + PIECE])
                xh = jnp.concatenate([xbc_ref[hf * half:hf * half + SUBLANES, cols], res], axis=0)
                xbc_ref[SUBLANES + hf * half:SUBLANES + (hf + 1) * half, cols] = res
                acc = xh * convw_ref[0:1, cols]
                for i in range(1, CONV_W):
                    acc = pltpu.roll(acc, 1, 0) + xh * convw_ref[i:i + 1, cols]
                act_ref[rows, cols] = _silu(acc[SUBLANES:] + convb_ref[:, cols])
            return run

        def dt():
            dtr_ref[rows, :] = _dot(xb_ref[rows, :], w_in_ref[:, OFF_DT:D_IN_PAD])

        return ([cast] + [main(c0) for c0 in range(0, OFF_XBC, PIECE)]
                + [xbc(c0) for c0 in range(0, CONV_DIM, PIECE)] + [dt])

    def out_pieces(hf):
        rows = slice(hf * half, (hf + 1) * half)

        def part(c0):
            def run():
                mo_ref[rows, c0:c0 + PIECE] = _dot(mix_ref[rows, :], w_out_ref[:, c0:c0 + PIECE])
            return run
        return [part(c0) for c0 in range(0, D_MODEL, PIECE)]

    def norm_pieces(hf):
        def piece(r0):
            def run():
                rows = slice(r0, r0 + CHUNK)
                x1_ref[0, rows, :] = _layer_norm(ALPHA * x_ref[0, rows, :] + mo_ref[rows, :],
                                                 ln1g_ref[...], ln1b_ref[...])
            return run
        return [piece(r0) for r0 in range(hf * half, (hf + 1) * half, CHUNK)]

    def interleave(mxu_pieces, vpu_pieces):
        n_m, n_v = len(mxu_pieces), len(vpu_pieces)
        done = 0
        for i, piece in enumerate(mxu_pieces):
            piece()
            upto = (n_v * (i + 1)) // n_m
            for other in vpu_pieces[done:upto]:
                other()
            done = upto

    triu_b = jnp.where(row_i <= col_i, 1.0, 0.0).astype(BF16)
    a_col = -jnp.exp(alog_ref[...])
    zpad = jnp.zeros((CHUNK - SSM_HEADS, CHUNK), F32)
    lane256 = lax.broadcasted_iota(jnp.int32, (CHUNK, A_WIDTH), 1)
    r256 = lax.broadcasted_iota(jnp.int32, (2 * CHUNK, 2 * CHUNK), 0)
    c256 = lax.broadcasted_iota(jnp.int32, (2 * CHUNK, 2 * CHUNK), 1)
    t256 = jnp.bitwise_and(r256, CHUNK - 1)
    win_mask = (c256 > t256) & (c256 <= t256 + CHUNK)
    rcol = lax.broadcasted_iota(jnp.int32, (2 * CHUNK, 1), 0)
    sink_a = jnp.where(rcol < CHUNK, sinks_ref[layer, 0], sinks_ref[layer, 3]) * LOG2E
    sink_b = jnp.where(rcol < CHUNK, sinks_ref[layer, 1], sinks_ref[layer, 2]) * LOG2E

    def chunk_a(c):
        rows = slice(c * CHUNK, (c + 1) * CHUNK)
        u = proj_ref[rows, OFF_AU:OFF_AU + A_WIDTH]
        vn = proj_ref[rows, OFF_AV:OFF_AV + A_WIDTH]
        pa = _dot(wstk_ref[...], vn.astype(BF16))
        mix_a = jnp.where(
            lane256 < HEAD_DIM, pa[0:CHUNK],
            jnp.where(lane256 < 2 * HEAD_DIM, pa[CHUNK:2 * CHUNK],
                      jnp.where(lane256 < 3 * HEAD_DIM, pa[2 * CHUNK:3 * CHUNK], pa[3 * CHUNK:])))
        mix_ref[rows, 0:A_WIDTH] = (u * (mix_a + bs_ref[...])).astype(BF16)

    def chunk_b(c):
        rows = slice(c * CHUNK, (c + 1) * CHUNK)
        q01 = proj_ref[rows, OFF_Q:OFF_Q + LANES]
        q23 = proj_ref[rows, OFF_Q + LANES:OFF_Q + 2 * LANES]
        kc = proj_ref[rows, OFF_K:OFF_K + KV_WIDTH]
        vc = proj_ref[rows, OFF_V:OFF_V + KV_WIDTH]
        if c == 0:
            kp, vp = kprev_ref[...], vprev_ref[...]
        else:
            prev = slice((c - 1) * CHUNK, c * CHUNK)
            kp = proj_ref[prev, OFF_K:OFF_K + KV_WIDTH]
            vp = proj_ref[prev, OFF_V:OFF_V + KV_WIDTH]
        k2 = jnp.concatenate([kp, kc], axis=0)
        v2 = jnp.concatenate([vp, vc], axis=0)
        k2b = k2.astype(BF16)
        v2b = v2.astype(BF16)
        k2rb = pltpu.roll(k2, HEAD_DIM, 1).astype(BF16)
        v2rb = pltpu.roll(v2, HEAD_DIM, 1).astype(BF16)
        q_a = jnp.concatenate([jnp.where(lane_lo, q01, 0.0), jnp.where(lane_lo, 0.0, q23)], axis=0)
        q_b = jnp.concatenate([jnp.where(lane_lo, 0.0, q01), jnp.where(lane_lo, q23, 0.0)], axis=0)
        vis = win_mask & (c256 >= jnp.where(t == 0, CHUNK, 0)) if c == 0 else win_mask

        def attend(qm, kk, vv, sink2):
            s = jnp.where(vis, _dot_nt(qm.astype(BF16), kk), NEG)
            m = jnp.maximum(jnp.max(s, axis=-1, keepdims=True), sink2)
            p = jnp.exp2(s - m)
            den = jnp.sum(p, axis=-1, keepdims=True) + jnp.exp2(sink2 - m)
            return _dot(p.astype(BF16), vv) * (1.0 / den)

        o_a = attend(q_a, k2b, v2b, sink_a)
        o_b = attend(q_b, k2rb, v2rb, sink_b)
        mix_ref[rows, A_WIDTH:A_WIDTH + LANES] = jnp.where(
            lane_lo, o_a[0:CHUNK], o_b[0:CHUNK]).astype(BF16)
        mix_ref[rows, A_WIDTH + LANES:A_WIDTH + 2 * LANES] = jnp.where(
            lane_lo, o_b[CHUNK:], o_a[CHUNK:]).astype(BF16)

    def chunk_c(c):
        rows = slice(c * CHUNK, (c + 1) * CHUNK)
        xs = act_ref[rows, 0:C_WIDTH]
        bm = act_ref[rows, C_WIDTH:C_WIDTH + 2 * D_STATE]
        cm = act_ref[rows, C_WIDTH + 2 * D_STATE:CONV_DIM]
        gate = proj_ref[rows, OFF_Z:OFF_Z + C_WIDTH]
        dt_row = jax.nn.softplus(dtr_ref[rows, :].T[0:SSM_HEADS] + dtb_ref[...])
        cum_row = _dot_exact_rhs(dt_row * a_col, triu_b) * LOG2E
        cum = jnp.concatenate([cum_row, zpad], axis=0).T
        toend_row = jnp.exp2(cum_row[:, CHUNK - 1:CHUNK] - cum_row) * dt_row
        ecl = jnp.exp2(cum_row[:, CHUNK - 1:CHUNK])
        xt = xs.T
        yts = []
        for g in range(2):
            bg = bm[:, g * D_STATE:(g + 1) * D_STATE]
            cg = cm[:, g * D_STATE:(g + 1) * D_STATE]
            cb = _dot_nt(cg.astype(BF16), bg.astype(BF16))
            xw = []
            for hh in range(4):
                h = g * 4 + hh
                hs = slice(h * HEAD_DIM, (h + 1) * HEAD_DIM)
                cc = jnp.broadcast_to(cum[:, h:h + 1], (CHUNK, CHUNK))
                dec = jnp.exp2(jnp.where(causal, cc - cum_row[h:h + 1, :], NEG))
                m_h = cb * dec * dt_row[h:h + 1, :]
                ce_h = cg * jnp.exp2(cc)
                l_h = jnp.concatenate([m_h, ce_h], axis=1).astype(BF16)
                r_h = jnp.concatenate([xt[hs], h_ref[hs, :]], axis=1).astype(BF16)
                yts.append(_dot_nt(r_h, l_h))
                xw.append(xt[hs] * toend_row[h:h + 1, :])
            gs = slice(g * 4 * HEAD_DIM, (g + 1) * 4 * HEAD_DIM)
            upd = _dot(jnp.concatenate(xw, axis=0).astype(BF16), bg.astype(BF16))
            keep = jnp.concatenate(
                [jnp.broadcast_to(ecl[g * 4 + hh:g * 4 + hh + 1, :], (HEAD_DIM, D_STATE))
                 for hh in range(4)], axis=0)
            h_ref[gs, :] = h_ref[gs, :] * keep + upd
        y = jnp.concatenate(yts, axis=0).T
        out_c = _gated_rmsnorm(y, xs, gate, dskip_ref[...], gnw_ref[...])
        mix_ref[rows, A_WIDTH + B_WIDTH:D_MODEL] = out_c.astype(BF16)

    cph = half // CHUNK

    def chunks(hf):
        for c in range(hf * cph, (hf + 1) * cph):
            chunk_a(c)
            chunk_b(c)
            chunk_c(c)

    for hf in range(2):
        for piece in project_pieces(hf):
            piece()
    chunks(0)
    for piece in out_pieces(0):
        piece()
    chunks(1)
    interleave(out_pieces(1), norm_pieces(0))
    for piece in norm_pieces(1):
        piece()

    last = slice(tl - CHUNK, tl)
    kprev_ref[...] = proj_ref[last, OFF_K:OFF_K + KV_WIDTH]
    vprev_ref[...] = proj_ref[last, OFF_V:OFF_V + KV_WIDTH]
    tail = xbc_ref[SUBLANES + tl - (CONV_W - 1):SUBLANES + tl, :]
    xbc_ref[SUBLANES - (CONV_W - 1):SUBLANES, :] = tail

    @pl.when(t == nt - 1)
    def _final():
        klast_ref[0] = kprev_ref[...]
        vlast_ref[0] = vprev_ref[...]
        convnew_ref[0] = tail
        ssm_ref[0] = h_ref[...]


def _full(shape):
    n = len(shape)
    return pl.BlockSpec(shape, lambda *_: (0,) * n)


def _layer(shape, l, single_buffer=False):
    n = len(shape)
    mode = pl.Buffered(1) if single_buffer else None
    return pl.BlockSpec((None,) + tuple(shape), lambda *_: (l,) + (0,) * n, pipeline_mode=mode)


def _mixer_call(l, x, sinks, w_in, lnvg, lnvb, ws, bs_exp, convw, convb, dtb, alog, dskip, gnw,
                w_out, ln1g, ln1b, *, tl):
    bsz, seq, _ = x.shape
    nt = seq // tl
    kern = functools.partial(_mixer_kernel, tl=tl, layer=l)
    in_specs = [
        pl.BlockSpec(memory_space=pltpu.SMEM),
        pl.BlockSpec((1, tl, D_MODEL), lambda b, t: (b, t, 0)),
        _layer((D_MODEL, D_IN_PAD), l, single_buffer=True),
        _layer((1, A_WIDTH), l), _layer((1, A_WIDTH), l),
        _layer((4, CHUNK, CHUNK), l), _layer((CHUNK, A_WIDTH), l),
        _layer((CONV_W, CONV_DIM), l), _layer((1, CONV_DIM), l),
        _layer((SSM_HEADS, 1), l), _layer((SSM_HEADS, 1), l),
        _layer((1, C_WIDTH), l), _layer((1, C_WIDTH), l),
        _layer((D_MODEL, D_MODEL), l, single_buffer=True),
        _layer((1, D_MODEL), l), _layer((1, D_MODEL), l),
    ]
    out_shape = (
        jax.ShapeDtypeStruct((bsz, seq, D_MODEL), F32),
        jax.ShapeDtypeStruct((bsz, CHUNK, KV_WIDTH), F32),
        jax.ShapeDtypeStruct((bsz, CHUNK, KV_WIDTH), F32),
        jax.ShapeDtypeStruct((bsz, CONV_W - 1, CONV_DIM), F32),
        jax.ShapeDtypeStruct((bsz, C_WIDTH, D_STATE), F32),
    )
    out_specs = (
        pl.BlockSpec((1, tl, D_MODEL), lambda b, t: (b, t, 0)),
        pl.BlockSpec((1, CHUNK, KV_WIDTH), lambda b, t: (b, 0, 0)),
        pl.BlockSpec((1, CHUNK, KV_WIDTH), lambda b, t: (b, 0, 0)),
        pl.BlockSpec((1, CONV_W - 1, CONV_DIM), lambda b, t: (b, 0, 0)),
        pl.BlockSpec((1, C_WIDTH, D_STATE), lambda b, t: (b, 0, 0)),
    )
    scratch = [
        pltpu.VMEM((tl, OFF_XBC), F32),
        pltpu.VMEM((SUBLANES + tl, CONV_DIM), F32),
        pltpu.VMEM((tl, CONV_DIM), F32),
        pltpu.VMEM((tl, LANES), F32),
        pltpu.VMEM((tl, D_MODEL), BF16),
        pltpu.VMEM((CHUNK, KV_WIDTH), F32),
        pltpu.VMEM((CHUNK, KV_WIDTH), F32),
        pltpu.VMEM((C_WIDTH, D_STATE), F32),
        pltpu.VMEM((4 * CHUNK, CHUNK), BF16),
        pltpu.VMEM((tl, D_MODEL), BF16),
        pltpu.VMEM((tl, D_MODEL), F32),
    ]
    return pl.pallas_call(
        kern, out_shape=out_shape, grid=(bsz, nt), in_specs=in_specs, out_specs=out_specs,
        scratch_shapes=scratch, name="prompt_mixer",
        compiler_params=pltpu.CompilerParams(
            dimension_semantics=("arbitrary", "arbitrary"), vmem_limit_bytes=VMEM_LIMIT),
    )(sinks, x, w_in, lnvg, lnvb, ws, bs_exp, convw, convb, dtb, alog, dskip, gnw, w_out, ln1g, ln1b)


FF_CHUNK = 1024


def _exact_zero_after(v):
    u = lax.bitcast_convert_type(v, jnp.uint32)
    z = lax.shift_right_logical(lax.shift_right_logical(u, jnp.uint32(16)), jnp.uint32(16))
    return lax.bitcast_convert_type(z, F32)


def _ffn_matmuls(x, w1_ref, w2_ref, anchor=None):
    xb = x.astype(BF16)
    acc = None
    n_c = D_FF // FF_CHUNK
    for c in range(n_c):
        sl = slice(c * FF_CHUNK, (c + 1) * FF_CHUNK)
        h = jnp.maximum(_dot(xb, w1_ref[:, sl]), 0.0)
        if anchor is not None:
            q = anchor.shape[0] // n_c
            rows = slice(c * q, (c + 1) * q)
            top = jnp.concatenate([h[rows, 0:LANES] + anchor[rows], h[rows, LANES:]], axis=1)
            pieces = ([h[0:c * q]] if c else []) + [top] + ([h[(c + 1) * q:]] if c < n_c - 1 else [])
            h = jnp.concatenate(pieces, axis=0)
        part = _dot((h * h).astype(BF16), w2_ref[sl, :])
        acc = part if acc is None else acc + part
    return acc


def _ffn_kernel(x_ref, xprev_ref, w1_ref, w2_ref, g_ref, b_ref, o_ref, acc_ref, *, n_blocks):
    i = pl.program_id(0)

    def finish():
        o = _layer_norm(ALPHA * xprev_ref[...] + acc_ref[...], g_ref[...], b_ref[...])
        o_ref[...] = o
        return o

    @pl.when(i == 0)
    def _():
        acc_ref[...] = _ffn_matmuls(x_ref[...], w1_ref, w2_ref)

    @pl.when(jnp.logical_and(i > 0, i < n_blocks))
    def _():
        o = finish()
        acc_ref[...] = _ffn_matmuls(x_ref[...], w1_ref, w2_ref,
                                    anchor=_exact_zero_after(o[:, 0:LANES]))

    @pl.when(i == n_blocks)
    def _():
        finish()


def _ffn_call(l, x2d, w1, w2, g, b, *, tm):
    rows = x2d.shape[0]
    n_blocks = rows // tm
    return pl.pallas_call(
        functools.partial(_ffn_kernel, n_blocks=n_blocks),
        out_shape=jax.ShapeDtypeStruct((rows, D_MODEL), F32),
        grid=(n_blocks + 1,),
        in_specs=[pl.BlockSpec((tm, D_MODEL), lambda i: (jnp.minimum(i, n_blocks - 1), 0)),
                  pl.BlockSpec((tm, D_MODEL), lambda i: (jnp.maximum(i - 1, 0), 0)),
                  _layer((D_MODEL, D_FF), l, single_buffer=True),
                  _layer((D_FF, D_MODEL), l, single_buffer=True),
                  _layer((1, D_MODEL), l), _layer((1, D_MODEL), l)],
        out_specs=pl.BlockSpec((tm, D_MODEL), lambda i: (jnp.maximum(i - 1, 0), 0)),
        scratch_shapes=[pltpu.VMEM((tm, D_MODEL), F32)],
        name="ffn",
        compiler_params=pltpu.CompilerParams(
            dimension_semantics=("arbitrary",), vmem_limit_bytes=VMEM_LIMIT),
    )(x2d, x2d, w1, w2, g, b)


def _s1_kernel(x_ref, w_in_ref, lnvg_ref, lnvb_ref, ws0_ref, bs0_ref, convw_ref, convb_ref,
               cst_ref, dtb_ref, alog_ref, e_ref,
               vn_ref, outa_ref, q_ref, k_ref, v_ref, z_ref, cnew_ref, xs_ref, bm_ref,
               xT_ref, cT_ref, dec_ref):
    xb = x_ref[...].astype(BF16)
    proj = _dot(xb, w_in_ref[...])
    u = jax.nn.gelu(proj[:, OFF_AU:OFF_AU + A_WIDTH])
    vn = _layer_norm(jax.nn.gelu(proj[:, OFF_AV:OFF_AV + A_WIDTH]), lnvg_ref[...], lnvb_ref[...])
    vn_ref[...] = vn
    outa_ref[...] = u * (ws0_ref[...] * vn + bs0_ref[...])
    q_ref[...] = proj[:, OFF_Q:OFF_Q + B_WIDTH]
    k_ref[...] = proj[:, OFF_K:OFF_K + KV_WIDTH]
    v_ref[...] = proj[:, OFF_V:OFF_V + KV_WIDTH]
    z_ref[...] = proj[:, OFF_Z:OFF_Z + C_WIDTH]
    xbc = proj[:, OFF_XBC:OFF_DT]
    acc = convb_ref[...] + xbc * convw_ref[CONV_W - 1:CONV_W, :]
    for i in range(CONV_W - 1):
        acc = acc + cst_ref[i] * convw_ref[i:i + 1, :]
    cnew_ref[0] = cst_ref[1]
    cnew_ref[1] = cst_ref[2]
    cnew_ref[2] = xbc
    act = _silu(acc)
    xs = act[:, 0:C_WIDTH]
    xs_ref[...] = xs
    bm_ref[...] = act[:, C_WIDTH:C_WIDTH + 2 * D_STATE]
    cT_ref[...] = act[:, C_WIDTH + 2 * D_STATE:CONV_DIM].T
    dt = jax.nn.softplus(proj[:, OFF_DT:D_IN_PAD] + dtb_ref[...])
    dec_ref[...] = jnp.exp(dt * (-jnp.exp(alog_ref[...])))
    dt_exp = _dot_exact_rhs(dt, e_ref[...])
    xT_ref[...] = (xs * dt_exp).T.astype(BF16)


S2_TB = SUBLANES


def _s2_kernel(sinks_ref, q_ref, kn_ref, vn_ref, kst_ref, vst_ref, h_ref, xT_ref, bm_ref,
               cT_ref, dec_ref, ob_ref, hnew_ref, yT_ref, *, layer, step):
    i = step
    tb = S2_TB
    lane = lax.broadcasted_iota(jnp.int32, (tb, LANES), 1)
    lo = lane < HEAD_DIM

    q01 = q_ref[:, 0:LANES]
    q23 = q_ref[:, LANES:2 * LANES]
    qh = jnp.concatenate([
        jnp.where(lo, q01, 0.0),
        jnp.where(lo, pltpu.roll(q01, HEAD_DIM, 1), 0.0),
        jnp.where(lo, 0.0, pltpu.roll(q23, HEAD_DIM, 1)),
        jnp.where(lo, 0.0, q23)], axis=0)
    qhb = qh.astype(BF16)
    rowi = lax.broadcasted_iota(jnp.int32, (4 * tb, LANES), 0)
    coli = lax.broadcasted_iota(jnp.int32, (4 * tb, LANES), 1)
    tok = jnp.bitwise_and(rowi, tb - 1)
    s = jnp.zeros((4 * tb, LANES), F32)
    for j in range(tb):
        sj = _dot_nt(qhb, kst_ref[j].astype(BF16))
        s = jnp.where(tok == j, sj, s)
    s = jnp.where(coli >= 1, s * ATT_SCALE, NEG)
    kn4 = jnp.concatenate([kn_ref[...]] * 4, axis=0)
    vn4 = jnp.concatenate([vn_ref[...]] * 4, axis=0)
    s_new = jnp.sum(qh * kn4, axis=-1, keepdims=True) * ATT_SCALE
    head = jnp.right_shift(rowi[:, 0:1], 3)
    sink = jnp.where(head == 0, sinks_ref[layer, 0],
                     jnp.where(head == 1, sinks_ref[layer, 1],
                               jnp.where(head == 2, sinks_ref[layer, 2], sinks_ref[layer, 3])))
    m = jnp.maximum(jnp.maximum(jnp.max(s, axis=-1, keepdims=True), s_new), sink)
    p = jnp.exp(s - m)
    p_new = jnp.exp(s_new - m)
    inv = 1.0 / (jnp.sum(p, axis=-1, keepdims=True) + p_new + jnp.exp(sink - m))
    p = p * inv
    o = (p_new * inv) * vn4
    for j in range(tb):
        pj = jnp.where(tok == j, p, 0.0).astype(BF16)
        o = o + _dot(pj, vst_ref[j].astype(BF16))
    ob_ref[:, 0:LANES] = jnp.where(lo, o[0:tb], pltpu.roll(o[tb:2 * tb], HEAD_DIM, 1))
    ob_ref[:, LANES:2 * LANES] = jnp.where(lo, pltpu.roll(o[2 * tb:3 * tb], HEAD_DIM, 1), o[3 * tb:])

    @pl.when(i == 0)
    def _():
        yT_ref[...] = jnp.zeros_like(yT_ref)

    r128 = lax.broadcasted_iota(jnp.int32, (LANES, LANES), 0)
    c128 = lax.broadcasted_iota(jnp.int32, (LANES, LANES), 1)
    hp_g = C_WIDTH // 2
    for j in range(tb):
        b = i * tb + j
        brow = bm_ref[pl.ds(b, 1), :]
        drow = dec_ref[pl.ds(b, 1), :]
        for g in range(2):
            dm = jnp.where(r128 == b,
                           jnp.broadcast_to(brow[:, g * D_STATE:(g + 1) * D_STATE], (LANES, D_STATE)),
                           0.0).astype(BF16)
            upd = _dot(xT_ref[g * hp_g:(g + 1) * hp_g, :], dm)
            parts = []
            for hh in range(4):
                h = g * 4 + hh
                rs = slice(h * HEAD_DIM, (h + 1) * HEAD_DIM)
                hn = (h_ref[j, rs, :] * jnp.broadcast_to(drow[:, h:h + 1], (HEAD_DIM, D_STATE))
                      + upd[hh * HEAD_DIM:(hh + 1) * HEAD_DIM])
                hnew_ref[j, rs, :] = hn
                parts.append(hn)
            cmm = jnp.where(c128 == b, cT_ref[g * D_STATE:(g + 1) * D_STATE, :], 0.0).astype(BF16)
            yT_ref[g * hp_g:(g + 1) * hp_g, :] += _dot(jnp.concatenate(parts, axis=0).astype(BF16), cmm)


def _s3_kernel(yT_ref, xs_ref, z_ref, dskip_ref, gnw_ref, outa_ref, outb_ref, x_ref, w_out_ref,
               g_ref, b_ref, x1_ref):
    out_c = _gated_rmsnorm(yT_ref[...].T, xs_ref[...], _silu(z_ref[...]), dskip_ref[...], gnw_ref[...])
    mix = jnp.concatenate([outa_ref[...], outb_ref[...], out_c], axis=1).astype(BF16)
    x1_ref[...] = _layer_norm(ALPHA * x_ref[...] + _dot(mix, w_out_ref[...]), g_ref[...], b_ref[...])


def _sample_kernel(sinks_ref, x_ref, w_in_ref, lnvg_ref, lnvb_ref, ws0_ref, bs0_ref, convw_ref,
                   convb_ref, cst_ref, dtb_ref, alog_ref, e_ref, kst_ref, vst_ref, h_ref,
                   dskip_ref, gnw_ref, w_out_ref, ln1g_ref, ln1b_ref, w1_ref, w2_ref, ln2g_ref, ln2b_ref,
                   ys_ref, vns_ref, ksm_ref, vsm_ref, csm_ref, hnew_ref,
                   res_scr, outa_scr, q_scr, z_scr, xs_scr, bm_scr, xT_scr, cT_scr, dec_scr,
                   outb_scr, yT_scr, x1_scr):
    l = pl.program_id(0)
    i = pl.program_id(1)
    last = pl.num_programs(1) - 1

    @pl.when(jnp.logical_and(l == 0, i == 0))
    def _():
        res_scr[...] = x_ref[...]

    @pl.when(i == 0)
    def _():
        _s1_kernel(res_scr, w_in_ref, lnvg_ref, lnvb_ref, ws0_ref, bs0_ref, convw_ref, convb_ref,
                   cst_ref, dtb_ref, alog_ref, e_ref,
                   vns_ref, outa_scr, q_scr, ksm_ref, vsm_ref, z_scr, csm_ref, xs_scr, bm_scr,
                   xT_scr, cT_scr, dec_scr)

    rows = pl.ds(pl.multiple_of(i * S2_TB, S2_TB), S2_TB)
    _s2_kernel(sinks_ref, q_scr.at[rows], ksm_ref.at[rows], vsm_ref.at[rows], kst_ref, vst_ref,
               h_ref, xT_scr, bm_scr, cT_scr, dec_scr, outb_scr.at[rows], hnew_ref, yT_scr,
               layer=l, step=i)

    @pl.when(i == last)
    def _():
        _s3_kernel(yT_scr, xs_scr, z_scr, dskip_ref, gnw_ref, outa_scr, outb_scr, res_scr,
                   w_out_ref, ln1g_ref, ln1b_ref, x1_scr)
        x1 = x1_scr[...]
        res_scr[...] = _layer_norm(ALPHA * x1 + _ffn_matmuls(x1, w1_ref, w2_ref),
                                   ln2g_ref[...], ln2b_ref[...])
        ys_ref[...] = res_scr[...]


def _sample_call(x, sinks, w_in, lnvg, lnvb, ws0, bs0, convw, convb, cst, dtb, alog, emat,
                 kst, vst, hst, dskip, gnw, w_out, ln1g, ln1b, w1, w2, ln2g, ln2b):
    n = x.shape[0]
    tb = S2_TB

    def per_layer(shape, single_buffer=False):
        k = len(shape)
        mode = pl.Buffered(1) if single_buffer else None
        return pl.BlockSpec((None,) + tuple(shape), lambda l, i: (l,) + (0,) * k, pipeline_mode=mode)

    def per_block(shape):
        k = len(shape) - 1
        return pl.BlockSpec((None,) + tuple(shape), lambda l, i: (l, i) + (0,) * k)

    in_specs = [
        pl.BlockSpec(memory_space=pltpu.SMEM),
        _full((n, D_MODEL)),
        per_layer((D_MODEL, D_IN_PAD), True),
        per_layer((1, A_WIDTH)), per_layer((1, A_WIDTH)), per_layer((1, A_WIDTH)), per_layer((1, A_WIDTH)),
        per_layer((CONV_W, CONV_DIM)), per_layer((1, CONV_DIM)), per_layer((CONV_W - 1, n, CONV_DIM)),
        per_layer((1, LANES)), per_layer((1, LANES)), _full((LANES, C_WIDTH)),
        per_block((tb, CHUNK, KV_WIDTH)), per_block((tb, CHUNK, KV_WIDTH)),
        per_block((tb, C_WIDTH, D_STATE)),
        per_layer((1, C_WIDTH)), per_layer((1, C_WIDTH)),
        per_layer((D_MODEL, D_MODEL), True), per_layer((1, D_MODEL)), per_layer((1, D_MODEL)),
        per_layer((D_MODEL, D_FF), True), per_layer((D_FF, D_MODEL), True),
        per_layer((1, D_MODEL)), per_layer((1, D_MODEL)),
    ]
    out_shape = (
        jax.ShapeDtypeStruct((n, D_MODEL), F32),
        jax.ShapeDtypeStruct((DEPTH, n, A_WIDTH), F32),
        jax.ShapeDtypeStruct((DEPTH, n, KV_WIDTH), F32),
        jax.ShapeDtypeStruct((DEPTH, n, KV_WIDTH), F32),
        jax.ShapeDtypeStruct((DEPTH, CONV_W - 1, n, CONV_DIM), F32),
        jax.ShapeDtypeStruct((DEPTH, n, C_WIDTH, D_STATE), F32),
    )
    out_specs = (
        _full((n, D_MODEL)),
        per_layer((n, A_WIDTH)), per_layer((n, KV_WIDTH)), per_layer((n, KV_WIDTH)),
        per_layer((CONV_W - 1, n, CONV_DIM)),
        per_block((tb, C_WIDTH, D_STATE)),
    )
    scratch = [
        pltpu.VMEM((n, D_MODEL), F32),
        pltpu.VMEM((n, A_WIDTH), F32),
        pltpu.VMEM((n, B_WIDTH), F32),
        pltpu.VMEM((n, C_WIDTH), F32),
        pltpu.VMEM((n, C_WIDTH), F32),
        pltpu.VMEM((n, 2 * D_STATE), F32),
        pltpu.VMEM((C_WIDTH, n), BF16),
        pltpu.VMEM((2 * D_STATE, n), F32),
        pltpu.VMEM((n, LANES), F32),
        pltpu.VMEM((n, B_WIDTH), F32),
        pltpu.VMEM((C_WIDTH, n), F32),
        pltpu.VMEM((n, D_MODEL), F32),
    ]
    return pl.pallas_call(
        _sample_kernel, out_shape=out_shape, grid=(DEPTH, n // tb), in_specs=in_specs,
        out_specs=out_specs, scratch_shapes=scratch, name="sample_step",
        compiler_params=pltpu.CompilerParams(
            dimension_semantics=("arbitrary", "arbitrary"), vmem_limit_bytes=VMEM_LIMIT),
    )(sinks, x, w_in, lnvg, lnvb, ws0, bs0, convw, convb, cst, dtb, alog, emat, kst, vst, hst,
      dskip, gnw, w_out, ln1g, ln1b, w1, w2, ln2g, ln2b)


PROMPT_TL = 1024
PROMPT_TM = 1024


def kernel(x_prompt, x_sample, state_attn_k, state_attn_v, state_conv, state_ssm, w_in, ln_v_g,
           ln_v_b, w_s, b_s, sinks, conv_w, conv_b, dt_bias, a_log, d_skip, gn_w, w_out, ln1_g,
           ln1_b, w1, w2, ln2_g, ln2_b):
    bsz, seq, _ = x_prompt.shape
    n_s = x_sample.shape[0]
    d_in = w_in.shape[-1]

    w_in_b = jnp.pad(w_in, ((0, 0), (0, 0), (0, D_IN_PAD - d_in))).astype(BF16)
    w_out_b = w_out.astype(BF16)
    w1_b = w1.astype(BF16)
    w2_b = w2.astype(BF16)
    pad_h = ((0, 0), (0, LANES - SSM_HEADS))
    dtb_p = jnp.pad(dt_bias, pad_h)[:, None, :]
    alog_p = jnp.pad(a_log, pad_h)[:, None, :]
    dskip_e = jnp.repeat(d_skip, HEAD_DIM, axis=-1)[:, None, :]
    bs_e = jnp.repeat(jnp.swapaxes(b_s, 1, 2), HEAD_DIM, axis=-1)
    ws0_e = jnp.repeat(w_s[:, :, 0, 0], HEAD_DIM, axis=-1)[:, None, :]
    bs0_e = jnp.repeat(b_s[:, :, 0], HEAD_DIM, axis=-1)[:, None, :]
    emat = (lax.broadcasted_iota(jnp.int32, (LANES, C_WIDTH), 0)
            == lax.broadcasted_iota(jnp.int32, (LANES, C_WIDTH), 1) // HEAD_DIM).astype(BF16)
    row = lambda a: a[:, None, :]
    lnvg, lnvb, convb, gnw = row(ln_v_g), row(ln_v_b), row(conv_b), row(gn_w)
    ln1g, ln1b, ln2g, ln2b = row(ln1_g), row(ln1_b), row(ln2_g), row(ln2_b)
    kst = state_attn_k.reshape(DEPTH, n_s, CHUNK, KV_WIDTH)
    vst = state_attn_v.reshape(DEPTH, n_s, CHUNK, KV_WIDTH)
    hst = state_ssm.reshape(DEPTH, n_s, C_WIDTH, D_STATE)
    cst = jnp.swapaxes(state_conv, 1, 2)

    ys, vns, ksm, vsm, csm, hsm = _sample_call(
        x_sample.reshape(n_s, D_MODEL), sinks, w_in_b, lnvg, lnvb, ws0_e, bs0_e, conv_w, convb, cst,
        dtb_p, alog_p, emat, kst, vst, hst, dskip_e, gnw, w_out_b, ln1g, ln1b, w1_b, w2_b, ln2g, ln2b)

    yp = x_prompt
    kp, vp, cp, hp = [], [], [], []
    for l in range(DEPTH):
        x1, k_l, v_l, c_l, h_l = _mixer_call(
            l, yp, sinks, w_in_b, lnvg, lnvb, w_s, bs_e, conv_w, convb,
            dt_bias[:, :, None], a_log[:, :, None], dskip_e, gnw, w_out_b, ln1g, ln1b, tl=PROMPT_TL)
        yp = _ffn_call(l, x1.reshape(bsz * seq, D_MODEL), w1_b, w2_b, ln2g, ln2b,
                       tm=PROMPT_TM).reshape(bsz, seq, D_MODEL)
        kp.append(k_l); vp.append(v_l); cp.append(c_l); hp.append(h_l)

    kv_p = (DEPTH, bsz, CHUNK, 2, HEAD_DIM)
    kv_s = (DEPTH, n_s, 1, 2, HEAD_DIM)
    ssm_shape = (SSM_HEADS, HEAD_DIM, D_STATE)
    return (yp, ys.reshape(n_s, 1, D_MODEL),
            jnp.stack(kp).reshape(kv_p), jnp.stack(vp).reshape(kv_p),
            jnp.stack(cp), jnp.stack(hp).reshape((DEPTH, bsz) + ssm_shape),
            ksm.reshape(kv_s), vsm.reshape(kv_s),
            jnp.swapaxes(csm, 1, 2),
            hsm.reshape((DEPTH, n_s) + ssm_shape),
            vns.reshape(DEPTH, n_s, 1, A_WIDTH))
```

```python
import functools

import jax
import jax.numpy as jnp
from jax import lax
from jax.experimental import pallas as pl
from jax.experimental.pallas import tpu as pltpu

F32 = jnp.float32
BF16 = jnp.bfloat16

D_MODEL = 1024
DEPTH = 4
HEAD_DIM = 64
A_WIDTH = 256
B_WIDTH = 256
KV_WIDTH = 128
C_WIDTH = 512
CONV_DIM = 1024
CONV_W = 4
SSM_HEADS = 8
D_STATE = 128
D_FF = 4096
CHUNK = 128
ALPHA = (2 * DEPTH) ** 0.25
LN_EPS = 1e-5
RMS_EPS = 1e-6
ATT_SCALE = HEAD_DIM ** -0.5
LOG2E = 1.4426950408889634
Q_SCALE = ATT_SCALE * LOG2E
NEG = -1e30

OFF_AU, OFF_AV, OFF_Q, OFF_K, OFF_V, OFF_Z = 0, 256, 512, 768, 896, 1024
OFF_XBC = 1536
OFF_DT = 2560
D_IN_PAD = 2688
LANES = 128
SUBLANES = 8
VMEM_LIMIT = 56 * 1024 * 1024

NT_DIMS = (((1,), (1,)), ((), ()))
PIECE = 256


def _dot(a, b):
    return jnp.dot(a, b, preferred_element_type=F32)


def _dot_nt(a, b):
    return lax.dot_general(a, b, NT_DIMS, preferred_element_type=F32)


def _layer_norm(x, g, b):
    mu = jnp.mean(x, axis=-1, keepdims=True)
    xc = x - mu
    var = jnp.mean(xc * xc, axis=-1, keepdims=True)
    return xc * lax.rsqrt(var + LN_EPS) * g + b


def _silu(x):
    return x * jax.nn.sigmoid(x)


def _split3(x):
    hi = x.astype(BF16)
    r = x - hi.astype(F32)
    mid = r.astype(BF16)
    lo = (r - mid.astype(F32)).astype(BF16)
    return hi, mid, lo


def _dot_exact_rhs(x, m_bf16):
    hi, mid, lo = _split3(x)
    return _dot(hi, m_bf16) + _dot(mid, m_bf16) + _dot(lo, m_bf16)


def _gated_rmsnorm(y, xs, gate, dskip, gnw):
    y2 = (y + dskip * xs) * gate
    ms = jnp.mean(y2 * y2, axis=-1, keepdims=True)
    return y2 * lax.rsqrt(ms + RMS_EPS) * gnw


def _mixer_kernel(sinks_ref, x_ref, w_in_ref, lnvg_ref, lnvb_ref, ws_ref, bs_ref,
                  convw_ref, convb_ref, dtb_ref, alog_ref, dskip_ref, gnw_ref,
                  w_out_ref, ln1g_ref, ln1b_ref,
                  x1_ref, klast_ref, vlast_ref, convnew_ref, ssm_ref,
                  proj_ref, xbc_ref, act_ref, dtr_ref, mix_ref, kprev_ref, vprev_ref,
                  h_ref, wstk_ref, xb_ref, mo_ref, *, tl, layer):
    t = pl.program_id(1)
    nt = pl.num_programs(1)

    row_i = lax.broadcasted_iota(jnp.int32, (CHUNK, CHUNK), 0)
    col_i = lax.broadcasted_iota(jnp.int32, (CHUNK, CHUNK), 1)
    causal = col_i <= row_i
    lane_lo = col_i < HEAD_DIM

    @pl.when(t == 0)
    def _init():
        kprev_ref[...] = jnp.zeros_like(kprev_ref)
        vprev_ref[...] = jnp.zeros_like(vprev_ref)
        h_ref[...] = jnp.zeros_like(h_ref)
        xbc_ref[0:SUBLANES, :] = jnp.zeros((SUBLANES, CONV_DIM), F32)
        for h in range(4):
            wstk_ref[h * CHUNK:(h + 1) * CHUNK, :] = jnp.where(causal, ws_ref[h], 0.0).astype(BF16)

    half = tl // 2

    def project_pieces(hf):
        rows = slice(hf * half, (hf + 1) * half)

        def cast():
            xb_ref[rows, :] = x_ref[0, rows, :].astype(BF16)

        epilogue = {
            OFF_AU: jax.nn.gelu,
            OFF_AV: lambda r: _layer_norm(jax.nn.gelu(r), lnvg_ref[...], lnvb_ref[...]),
            OFF_Q: lambda r: r * Q_SCALE,
            OFF_Z: _silu,
            OFF_Z + PIECE: _silu,
        }

        def main(c0):
            def run():
                res = _dot(xb_ref[rows, :], w_in_ref[:, c0:c0 + PIECE])
                proj_ref[rows, c0:c0 + PIECE] = epilogue.get(c0, lambda r: r)(res)
            return run

        def xbc(c0):
            def run():
                cols = slice(c0, c0 + PIECE)
                res = _dot(xb_ref[rows, :], w_in_ref[:, OFF_XBC + c0:OFF_XBC + c0 + PIECE])
                xh = jnp.concatenate([xbc_ref[hf * half:hf * half + SUBLANES, cols], res], axis=0)
                xbc_ref[SUBLANES + hf * half:SUBLANES + (hf + 1) * half, cols] = res
                acc = xh * convw_ref[0:1, cols]
                for i in range(1, CONV_W):
                    acc = pltpu.roll(acc, 1, 0) + xh * convw_ref[i:i + 1, cols]
                act_ref[rows, cols] = _silu(acc[SUBLANES:] + convb_ref[:, cols])
            return run

        def dt():
            dtr_ref[rows, :] = _dot(xb_ref[rows, :], w_in_ref[:, OFF_DT:D_IN_PAD])

        return ([cast] + [main(c0) for c0 in range(0, OFF_XBC, PIECE)]
                + [xbc(c0) for c0 in range(0, CONV_DIM, PIECE)] + [dt])

    def out_pieces(hf):
        rows = slice(hf * half, (hf + 1) * half)

        def part(c0):
            def run():
                mo_ref[rows, c0:c0 + PIECE] = _dot(mix_ref[rows, :], w_out_ref[:, c0:c0 + PIECE])
            return run
        return [part(c0) for c0 in range(0, D_MODEL, PIECE)]

    def norm_pieces(hf):
        def piece(r0):
            def run():
                rows = slice(r0, r0 + CHUNK)
                x1_ref[0, rows, :] = _layer_norm(ALPHA * x_ref[0, rows, :] + mo_ref[rows, :],
                                                 ln1g_ref[...], ln1b_ref[...])
            return run
        return [piece(r0) for r0 in range(hf * half, (hf + 1) * half, CHUNK)]

    def interleave(mxu_pieces, vpu_pieces):
        n_m, n_v = len(mxu_pieces), len(vpu_pieces)
        done = 0
        for i, piece in enumerate(mxu_pieces):
            piece()
            upto = (n_v * (i + 1)) // n_m
            for other in vpu_pieces[done:upto]:
                other()
            done = upto

    triu_b = jnp.where(row_i <= col_i, 1.0, 0.0).astype(BF16)
    a_col = -jnp.exp(alog_ref[...])
    zpad = jnp.zeros((CHUNK - SSM_HEADS, CHUNK), F32)
    lane256 = lax.broadcasted_iota(jnp.int32, (CHUNK, A_WIDTH), 1)
    r256 = lax.broadcasted_iota(jnp.int32, (2 * CHUNK, 2 * CHUNK), 0)
    c256 = lax.broadcasted_iota(jnp.int32, (2 * CHUNK, 2 * CHUNK), 1)
    t256 = jnp.bitwise_and(r256, CHUNK - 1)
    win_mask = (c256 > t256) & (c256 <= t256 + CHUNK)
    rcol = lax.broadcasted_iota(jnp.int32, (2 * CHUNK, 1), 0)
    sink_a = jnp.where(rcol < CHUNK, sinks_ref[layer, 0], sinks_ref[layer, 3]) * LOG2E
    sink_b = jnp.where(rcol < CHUNK, sinks_ref[layer, 1], sinks_ref[layer, 2]) * LOG2E

    def chunk_a(c):
        rows = slice(c * CHUNK, (c + 1) * CHUNK)
        u = proj_ref[rows, OFF_AU:OFF_AU + A_WIDTH]
        vn = proj_ref[rows, OFF_AV:OFF_AV + A_WIDTH]
        pa = _dot(wstk_ref[...], vn.astype(BF16))
        mix_a = jnp.where(
            lane256 < HEAD_DIM, pa[0:CHUNK],
            jnp.where(lane256 < 2 * HEAD_DIM, pa[CHUNK:2 * CHUNK],
                      jnp.where(lane256 < 3 * HEAD_DIM, pa[2 * CHUNK:3 * CHUNK], pa[3 * CHUNK:])))
        mix_ref[rows, 0:A_WIDTH] = (u * (mix_a + bs_ref[...])).astype(BF16)

    def chunk_b(c):
        rows = slice(c * CHUNK, (c + 1) * CHUNK)
        q01 = proj_ref[rows, OFF_Q:OFF_Q + LANES]
        q23 = proj_ref[rows, OFF_Q + LANES:OFF_Q + 2 * LANES]
        kc = proj_ref[rows, OFF_K:OFF_K + KV_WIDTH]
        vc = proj_ref[rows, OFF_V:OFF_V + KV_WIDTH]
        if c == 0:
            kp, vp = kprev_ref[...], vprev_ref[...]
        else:
            prev = slice((c - 1) * CHUNK, c * CHUNK)
            kp = proj_ref[prev, OFF_K:OFF_K + KV_WIDTH]
            vp = proj_ref[prev, OFF_V:OFF_V + KV_WIDTH]
        k2 = jnp.concatenate([kp, kc], axis=0)
        v2 = jnp.concatenate([vp, vc], axis=0)
        k2b = k2.astype(BF16)
        v2b = v2.astype(BF16)
        k2rb = pltpu.roll(k2, HEAD_DIM, 1).astype(BF16)
        v2rb = pltpu.roll(v2, HEAD_DIM, 1).astype(BF16)
        q_a = jnp.concatenate([jnp.where(lane_lo, q01, 0.0), jnp.where(lane_lo, 0.0, q23)], axis=0)
        q_b = jnp.concatenate([jnp.where(lane_lo, 0.0, q01), jnp.where(lane_lo, q23, 0.0)], axis=0)
        vis = win_mask & (c256 >= jnp.where(t == 0, CHUNK, 0)) if c == 0 else win_mask

        def attend(qm, kk, vv, sink2):
            s = jnp.where(vis, _dot_nt(qm.astype(BF16), kk), NEG)
            m = jnp.maximum(jnp.max(s, axis=-1, keepdims=True), sink2)
            p = jnp.exp2(s - m)
            den = jnp.sum(p, axis=-1, keepdims=True) + jnp.exp2(sink2 - m)
            return _dot(p.astype(BF16), vv) * (1.0 / den)

        o_a = attend(q_a, k2b, v2b, sink_a)
        o_b = attend(q_b, k2rb, v2rb, sink_b)
        mix_ref[rows, A_WIDTH:A_WIDTH + LANES] = jnp.where(
            lane_lo, o_a[0:CHUNK], o_b[0:CHUNK]).astype(BF16)
        mix_ref[rows, A_WIDTH + LANES:A_WIDTH + 2 * LANES] = jnp.where(
            lane_lo, o_b[CHUNK:], o_a[CHUNK:]).astype(BF16)

    def chunk_c(c):
        rows = slice(c * CHUNK, (c + 1) * CHUNK)
        xs = act_ref[rows, 0:C_WIDTH]
        bm = act_ref[rows, C_WIDTH:C_WIDTH + 2 * D_STATE]
        cm = act_ref[rows, C_WIDTH + 2 * D_STATE:CONV_DIM]
        gate = proj_ref[rows, OFF_Z:OFF_Z + C_WIDTH]
        dt_row = jax.nn.softplus(dtr_ref[rows, :].T[0:SSM_HEADS] + dtb_ref[...])
        cum_row = _dot_exact_rhs(dt_row * a_col, triu_b) * LOG2E
        cum = jnp.concatenate([cum_row, zpad], axis=0).T
        toend_row = jnp.exp2(cum_row[:, CHUNK - 1:CHUNK] - cum_row) * dt_row
        ecl = jnp.exp2(cum_row[:, CHUNK - 1:CHUNK])
        xt = xs.T
        yts = []
        for g in range(2):
            bg = bm[:, g * D_STATE:(g + 1) * D_STATE]
            cg = cm[:, g * D_STATE:(g + 1) * D_STATE]
            cb = _dot_nt(cg.astype(BF16), bg.astype(BF16))
            xw = []
            for hh in range(4):
                h = g * 4 + hh
                hs = slice(h * HEAD_DIM, (h + 1) * HEAD_DIM)
                cc = jnp.broadcast_to(cum[:, h:h + 1], (CHUNK, CHUNK))
                dec = jnp.exp2(jnp.where(causal, cc - cum_row[h:h + 1, :], NEG))
                m_h = cb * dec * dt_row[h:h + 1, :]
                ce_h = cg * jnp.exp2(cc)
                l_h = jnp.concatenate([m_h, ce_h], axis=1).astype(BF16)
                r_h = jnp.concatenate([xt[hs], h_ref[hs, :]], axis=1).astype(BF16)
                yts.append(_dot_nt(r_h, l_h))
                xw.append(xt[hs] * toend_row[h:h + 1, :])
            gs = slice(g * 4 * HEAD_DIM, (g + 1) * 4 * HEAD_DIM)
            upd = _dot(jnp.concatenate(xw, axis=0).astype(BF16), bg.astype(BF16))
            keep = jnp.concatenate(
                [jnp.broadcast_to(ecl[g * 4 + hh:g * 4 + hh + 1, :], (HEAD_DIM, D_STATE))
                 for hh in range(4)], axis=0)
            h_ref[gs, :] = h_ref[gs, :] * keep + upd
        y = jnp.concatenate(yts, axis=0).T
        out_c = _gated_rmsnorm(y, xs, gate, dskip_ref[...], gnw_ref[...])
        mix_ref[rows, A_WIDTH + B_WIDTH:D_MODEL] = out_c.astype(BF16)

    cph = half // CHUNK

    def chunks(hf):
        for c in range(hf * cph, (hf + 1) * cph):
            chunk_a(c)
            chunk_b(c)
            chunk_c(c)

    for hf in range(2):
        for piece in project_pieces(hf):
            piece()
    chunks(0)
    for piece in out_pieces(0):
        piece()
    chunks(1)
    interleave(out_pieces(1), norm_pieces(0))
    for piece in norm_pieces(1):
        piece()

    last = slice(tl - CHUNK, tl)
    kprev_ref[...] = proj_ref[last, OFF_K:OFF_K + KV_WIDTH]
    vprev_ref[...] = proj_ref[last, OFF_V:OFF_V + KV_WIDTH]
    tail = xbc_ref[SUBLANES + tl - (CONV_W - 1):SUBLANES + tl, :]
    xbc_ref[SUBLANES - (CONV_W - 1):SUBLANES, :] = tail

    @pl.when(t == nt - 1)
    def _final():
        klast_ref[0] = kprev_ref[...]
        vlast_ref[0] = vprev_ref[...]
        convnew_ref[0] = tail
        ssm_ref[0] = h_ref[...]


def _full(shape):
    n = len(shape)
    return pl.BlockSpec(shape, lambda *_: (0,) * n)


def _layer(shape, l, single_buffer=False):
    n = len(shape)
    mode = pl.Buffered(1) if single_buffer else None
    return pl.BlockSpec((None,) + tuple(shape), lambda *_: (l,) + (0,) * n, pipeline_mode=mode)


def _mixer_call(l, x, sinks, w_in, lnvg, lnvb, ws, bs_exp, convw, convb, dtb, alog, dskip, gnw,
                w_out, ln1g, ln1b, *, tl):
    bsz, seq, _ = x.shape
    nt = seq // tl
    kern = functools.partial(_mixer_kernel, tl=tl, layer=l)
    in_specs = [
        pl.BlockSpec(memory_space=pltpu.SMEM),
        pl.BlockSpec((1, tl, D_MODEL), lambda b, t: (b, t, 0)),
        _layer((D_MODEL, D_IN_PAD), l, single_buffer=True),
        _layer((1, A_WIDTH), l), _layer((1, A_WIDTH), l),
        _layer((4, CHUNK, CHUNK), l), _layer((CHUNK, A_WIDTH), l),
        _layer((CONV_W, CONV_DIM), l), _layer((1, CONV_DIM), l),
        _layer((SSM_HEADS, 1), l), _layer((SSM_HEADS, 1), l),
        _layer((1, C_WIDTH), l), _layer((1, C_WIDTH), l),
        _layer((D_MODEL, D_MODEL), l, single_buffer=True),
        _layer((1, D_MODEL), l), _layer((1, D_MODEL), l),
    ]
    out_shape = (
        jax.ShapeDtypeStruct((bsz, seq, D_MODEL), F32),
        jax.ShapeDtypeStruct((bsz, CHUNK, KV_WIDTH), F32),
        jax.ShapeDtypeStruct((bsz, CHUNK, KV_WIDTH), F32),
        jax.ShapeDtypeStruct((bsz, CONV_W - 1, CONV_DIM), F32),
        jax.ShapeDtypeStruct((bsz, C_WIDTH, D_STATE), F32),
    )
    out_specs = (
        pl.BlockSpec((1, tl, D_MODEL), lambda b, t: (b, t, 0)),
        pl.BlockSpec((1, CHUNK, KV_WIDTH), lambda b, t: (b, 0, 0)),
        pl.BlockSpec((1, CHUNK, KV_WIDTH), lambda b, t: (b, 0, 0)),
        pl.BlockSpec((1, CONV_W - 1, CONV_DIM), lambda b, t: (b, 0, 0)),
        pl.BlockSpec((1, C_WIDTH, D_STATE), lambda b, t: (b, 0, 0)),
    )
    scratch = [
        pltpu.VMEM((tl, OFF_XBC), F32),
        pltpu.VMEM((SUBLANES + tl, CONV_DIM), F32),
        pltpu.VMEM((tl, CONV_DIM), F32),
        pltpu.VMEM((tl, LANES), F32),
        pltpu.VMEM((tl, D_MODEL), BF16),
        pltpu.VMEM((CHUNK, KV_WIDTH), F32),
        pltpu.VMEM((CHUNK, KV_WIDTH), F32),
        pltpu.VMEM((C_WIDTH, D_STATE), F32),
        pltpu.VMEM((4 * CHUNK, CHUNK), BF16),
        pltpu.VMEM((tl, D_MODEL), BF16),
        pltpu.VMEM((tl, D_MODEL), F32),
    ]
    return pl.pallas_call(
        kern, out_shape=out_shape, grid=(bsz, nt), in_specs=in_specs, out_specs=out_specs,
        scratch_shapes=scratch, name="prompt_mixer",
        compiler_params=pltpu.CompilerParams(
            dimension_semantics=("arbitrary", "arbitrary"), vmem_limit_bytes=VMEM_LIMIT),
    )(sinks, x, w_in, lnvg, lnvb, ws, bs_exp, convw, convb, dtb, alog, dskip, gnw, w_out, ln1g, ln1b)


FF_CHUNK = 1024


def _ffn(x, w1_ref, w2_ref, g_ref, b_ref):
    xb = x.astype(BF16)
    acc = jnp.zeros(x.shape, F32)
    for c in range(D_FF // FF_CHUNK):
        sl = slice(c * FF_CHUNK, (c + 1) * FF_CHUNK)
        h = jnp.maximum(_dot(xb, w1_ref[:, sl]), 0.0)
        acc = acc + _dot((h * h).astype(BF16), w2_ref[sl, :])
    return _layer_norm(ALPHA * x + acc, g_ref[...], b_ref[...])


def _ffn_kernel(x_ref, w1_ref, w2_ref, g_ref, b_ref, o_ref):
    o_ref[...] = _ffn(x_ref[...], w1_ref, w2_ref, g_ref, b_ref)


def _ffn_call(l, x2d, w1, w2, g, b, *, tm):
    rows = x2d.shape[0]
    return pl.pallas_call(
        _ffn_kernel, out_shape=jax.ShapeDtypeStruct((rows, D_MODEL), F32),
        grid=(rows // tm,),
        in_specs=[pl.BlockSpec((tm, D_MODEL), lambda i: (i, 0)),
                  _layer((D_MODEL, D_FF), l, single_buffer=True),
                  _layer((D_FF, D_MODEL), l, single_buffer=True),
                  _layer((1, D_MODEL), l), _layer((1, D_MODEL), l)],
        out_specs=pl.BlockSpec((tm, D_MODEL), lambda i: (i, 0)),
        name="ffn",
        compiler_params=pltpu.CompilerParams(
            dimension_semantics=("arbitrary",), vmem_limit_bytes=VMEM_LIMIT),
    )(x2d, w1, w2, g, b)


def _s1_kernel(x_ref, w_in_ref, lnvg_ref, lnvb_ref, ws0_ref, bs0_ref, convw_ref, convb_ref,
               cst_ref, dtb_ref, alog_ref, e_ref,
               vn_ref, outa_ref, q_ref, k_ref, v_ref, z_ref, cnew_ref, xs_ref, bm_ref,
               xT_ref, cT_ref, dec_ref):
    xb = x_ref[...].astype(BF16)
    proj = _dot(xb, w_in_ref[...])
    u = jax.nn.gelu(proj[:, OFF_AU:OFF_AU + A_WIDTH])
    vn = _layer_norm(jax.nn.gelu(proj[:, OFF_AV:OFF_AV + A_WIDTH]), lnvg_ref[...], lnvb_ref[...])
    vn_ref[...] = vn
    outa_ref[...] = u * (ws0_ref[...] * vn + bs0_ref[...])
    q_ref[...] = proj[:, OFF_Q:OFF_Q + B_WIDTH]
    k_ref[...] = proj[:, OFF_K:OFF_K + KV_WIDTH]
    v_ref[...] = proj[:, OFF_V:OFF_V + KV_WIDTH]
    z_ref[...] = proj[:, OFF_Z:OFF_Z + C_WIDTH]
    xbc = proj[:, OFF_XBC:OFF_DT]
    acc = convb_ref[...] + xbc * convw_ref[CONV_W - 1:CONV_W, :]
    for i in range(CONV_W - 1):
        acc = acc + cst_ref[i] * convw_ref[i:i + 1, :]
    cnew_ref[0] = cst_ref[1]
    cnew_ref[1] = cst_ref[2]
    cnew_ref[2] = xbc
    act = _silu(acc)
    xs = act[:, 0:C_WIDTH]
    xs_ref[...] = xs
    bm_ref[...] = act[:, C_WIDTH:C_WIDTH + 2 * D_STATE]
    cT_ref[...] = act[:, C_WIDTH + 2 * D_STATE:CONV_DIM].T
    dt = jax.nn.softplus(proj[:, OFF_DT:D_IN_PAD] + dtb_ref[...])
    dec_ref[...] = jnp.exp(dt * (-jnp.exp(alog_ref[...])))
    dt_exp = _dot_exact_rhs(dt, e_ref[...])
    xT_ref[...] = (xs * dt_exp).T


S2_TB = SUBLANES


def _s2_kernel(sinks_ref, q_ref, kn_ref, vn_ref, kst_ref, vst_ref, h_ref, xT_ref, bm_ref,
               cT_ref, dec_ref, ob_ref, hnew_ref, yT_ref, *, layer, step):
    i = step
    tb = S2_TB
    lane = lax.broadcasted_iota(jnp.int32, (tb, LANES), 1)
    lo = lane < HEAD_DIM

    q01 = q_ref[:, 0:LANES]
    q23 = q_ref[:, LANES:2 * LANES]
    qh = jnp.concatenate([
        jnp.where(lo, q01, 0.0),
        jnp.where(lo, pltpu.roll(q01, HEAD_DIM, 1), 0.0),
        jnp.where(lo, 0.0, pltpu.roll(q23, HEAD_DIM, 1)),
        jnp.where(lo, 0.0, q23)], axis=0)
    qhb = qh.astype(BF16)
    rowi = lax.broadcasted_iota(jnp.int32, (4 * tb, LANES), 0)
    coli = lax.broadcasted_iota(jnp.int32, (4 * tb, LANES), 1)
    tok = jnp.bitwise_and(rowi, tb - 1)
    s = jnp.zeros((4 * tb, LANES), F32)
    for j in range(tb):
        sj = _dot_nt(qhb, kst_ref[j].astype(BF16))
        s = jnp.where(tok == j, sj, s)
    s = jnp.where(coli >= 1, s * ATT_SCALE, NEG)
    kn4 = jnp.concatenate([kn_ref[...]] * 4, axis=0)
    vn4 = jnp.concatenate([vn_ref[...]] * 4, axis=0)
    s_new = jnp.sum(qh * kn4, axis=-1, keepdims=True) * ATT_SCALE
    head = jnp.right_shift(rowi[:, 0:1], 3)
    sink = jnp.where(head == 0, sinks_ref[layer, 0],
                     jnp.where(head == 1, sinks_ref[layer, 1],
                               jnp.where(head == 2, sinks_ref[layer, 2], sinks_ref[layer, 3])))
    m = jnp.maximum(jnp.maximum(jnp.max(s, axis=-1, keepdims=True), s_new), sink)
    p = jnp.exp(s - m)
    p_new = jnp.exp(s_new - m)
    inv = 1.0 / (jnp.sum(p, axis=-1, keepdims=True) + p_new + jnp.exp(sink - m))
    p = p * inv
    o = (p_new * inv) * vn4
    for j in range(tb):
        pj = jnp.where(tok == j, p, 0.0).astype(BF16)
        o = o + _dot(pj, vst_ref[j].astype(BF16))
    ob_ref[:, 0:LANES] = jnp.where(lo, o[0:tb], pltpu.roll(o[tb:2 * tb], HEAD_DIM, 1))
    ob_ref[:, LANES:2 * LANES] = jnp.where(lo, pltpu.roll(o[2 * tb:3 * tb], HEAD_DIM, 1), o[3 * tb:])

    @pl.when(i == 0)
    def _():
        yT_ref[...] = jnp.zeros_like(yT_ref)

    r8 = lax.broadcasted_iota(jnp.int32, (tb, D_STATE), 0)
    c128 = lax.broadcasted_iota(jnp.int32, (LANES, LANES), 1)
    hp_g = C_WIDTH // 2
    b0 = pl.multiple_of(i * tb, tb)
    zrows = jnp.zeros((LANES - tb, tb * D_STATE), BF16)
    to_front = jnp.bitwise_and(LANES - b0, LANES - 1)
    for g in range(2):
        gs = slice(g * hp_g, (g + 1) * hp_g)
        ns = slice(g * D_STATE, (g + 1) * D_STATE)
        bblk = bm_ref[pl.ds(b0, tb), ns]
        bdiag = jnp.concatenate([jnp.where(r8 == j, bblk, 0.0) for j in range(tb)], axis=1)
        rhs = jnp.concatenate([bdiag.astype(BF16), zrows], axis=0)
        xg = pltpu.roll(xT_ref[gs, :], to_front, 1).astype(BF16)
        upd = _dot(xg, rhs)
        hns = []
        cms = []
        for j in range(tb):
            drow = dec_ref[pl.ds(b0 + j, 1), :]
            parts = []
            for hh in range(4):
                h = g * 4 + hh
                rs = slice(h * HEAD_DIM, (h + 1) * HEAD_DIM)
                hn = (h_ref[j, rs, :] * jnp.broadcast_to(drow[:, h:h + 1], (HEAD_DIM, D_STATE))
                      + upd[hh * HEAD_DIM:(hh + 1) * HEAD_DIM, j * D_STATE:(j + 1) * D_STATE])
                hnew_ref[j, rs, :] = hn
                parts.append(hn)
            hns.append(jnp.concatenate(parts, axis=0).astype(BF16))
            cms.append(jnp.where(c128 == b0 + j, cT_ref[ns, :], 0.0).astype(BF16))
        yT_ref[gs, :] += _dot(jnp.concatenate(hns, axis=1), jnp.concatenate(cms, axis=0))


def _s3_kernel(yT_ref, xs_ref, z_ref, dskip_ref, gnw_ref, outa_ref, outb_ref, x_ref, w_out_ref,
               g_ref, b_ref, x1_ref):
    out_c = _gated_rmsnorm(yT_ref[...].T, xs_ref[...], _silu(z_ref[...]), dskip_ref[...], gnw_ref[...])
    mix = jnp.concatenate([outa_ref[...], outb_ref[...], out_c], axis=1).astype(BF16)
    x1_ref[...] = _layer_norm(ALPHA * x_ref[...] + _dot(mix, w_out_ref[...]), g_ref[...], b_ref[...])


def _sample_kernel(sinks_ref, x_ref, w_in_ref, lnvg_ref, lnvb_ref, ws0_ref, bs0_ref, convw_ref,
                   convb_ref, cst_ref, dtb_ref, alog_ref, e_ref, kst_ref, vst_ref, h_ref,
                   dskip_ref, gnw_ref, w_out_ref, ln1g_ref, ln1b_ref, w1_ref, w2_ref, ln2g_ref, ln2b_ref,
                   ys_ref, vns_ref, ksm_ref, vsm_ref, csm_ref, hnew_ref,
                   res_scr, outa_scr, q_scr, z_scr, xs_scr, bm_scr, xT_scr, cT_scr, dec_scr,
                   outb_scr, yT_scr, x1_scr):
    l = pl.program_id(0)
    i = pl.program_id(1)
    last = pl.num_programs(1) - 1

    @pl.when(jnp.logical_and(l == 0, i == 0))
    def _():
        res_scr[...] = x_ref[...]

    @pl.when(i == 0)
    def _():
        _s1_kernel(res_scr, w_in_ref, lnvg_ref, lnvb_ref, ws0_ref, bs0_ref, convw_ref, convb_ref,
                   cst_ref, dtb_ref, alog_ref, e_ref,
                   vns_ref, outa_scr, q_scr, ksm_ref, vsm_ref, z_scr, csm_ref, xs_scr, bm_scr,
                   xT_scr, cT_scr, dec_scr)

    rows = pl.ds(pl.multiple_of(i * S2_TB, S2_TB), S2_TB)
    _s2_kernel(sinks_ref, q_scr.at[rows], ksm_ref.at[rows], vsm_ref.at[rows], kst_ref, vst_ref,
               h_ref, xT_scr, bm_scr, cT_scr, dec_scr, outb_scr.at[rows], hnew_ref, yT_scr,
               layer=l, step=i)

    @pl.when(i == last)
    def _():
        _s3_kernel(yT_scr, xs_scr, z_scr, dskip_ref, gnw_ref, outa_scr, outb_scr, res_scr,
                   w_out_ref, ln1g_ref, ln1b_ref, x1_scr)
        res_scr[...] = _ffn(x1_scr[...], w1_ref, w2_ref, ln2g_ref, ln2b_ref)
        ys_ref[...] = res_scr[...]


def _sample_call(x, sinks, w_in, lnvg, lnvb, ws0, bs0, convw, convb, cst, dtb, alog, emat,
                 kst, vst, hst, dskip, gnw, w_out, ln1g, ln1b, w1, w2, ln2g, ln2b):
    n = x.shape[0]
    tb = S2_TB

    def per_layer(shape, single_buffer=False):
        k = len(shape)
        mode = pl.Buffered(1) if single_buffer else None
        return pl.BlockSpec((None,) + tuple(shape), lambda l, i: (l,) + (0,) * k, pipeline_mode=mode)

    def per_block(shape):
        k = len(shape) - 1
        return pl.BlockSpec((None,) + tuple(shape), lambda l, i: (l, i) + (0,) * k)

    in_specs = [
        pl.BlockSpec(memory_space=pltpu.SMEM),
        _full((n, D_MODEL)),
        per_layer((D_MODEL, D_IN_PAD), True),
        per_layer((1, A_WIDTH)), per_layer((1, A_WIDTH)), per_layer((1, A_WIDTH)), per_layer((1, A_WIDTH)),
        per_layer((CONV_W, CONV_DIM)), per_layer((1, CONV_DIM)), per_layer((CONV_W - 1, n, CONV_DIM)),
        per_layer((1, LANES)), per_layer((1, LANES)), _full((LANES, C_WIDTH)),
        per_block((tb, CHUNK, KV_WIDTH)), per_block((tb, CHUNK, KV_WIDTH)),
        per_block((tb, C_WIDTH, D_STATE)),
        per_layer((1, C_WIDTH)), per_layer((1, C_WIDTH)),
        per_layer((D_MODEL, D_MODEL), True), per_layer((1, D_MODEL)), per_layer((1, D_MODEL)),
        per_layer((D_MODEL, D_FF), True), per_layer((D_FF, D_MODEL), True),
        per_layer((1, D_MODEL)), per_layer((1, D_MODEL)),
    ]
    out_shape = (
        jax.ShapeDtypeStruct((n, D_MODEL), F32),
        jax.ShapeDtypeStruct((DEPTH, n, A_WIDTH), F32),
        jax.ShapeDtypeStruct((DEPTH, n, KV_WIDTH), F32),
        jax.ShapeDtypeStruct((DEPTH, n, KV_WIDTH), F32),
        jax.ShapeDtypeStruct((DEPTH, CONV_W - 1, n, CONV_DIM), F32),
        jax.ShapeDtypeStruct((DEPTH, n, C_WIDTH, D_STATE), F32),
    )
    out_specs = (
        _full((n, D_MODEL)),
        per_layer((n, A_WIDTH)), per_layer((n, KV_WIDTH)), per_layer((n, KV_WIDTH)),
        per_layer((CONV_W - 1, n, CONV_DIM)),
        per_block((tb, C_WIDTH, D_STATE)),
    )
    scratch = [
        pltpu.VMEM((n, D_MODEL), F32),
        pltpu.VMEM((n, A_WIDTH), F32),
        pltpu.VMEM((n, B_WIDTH), F32),
        pltpu.VMEM((n, C_WIDTH), F32),
        pltpu.VMEM((n, C_WIDTH), F32),
        pltpu.VMEM((n, 2 * D_STATE), F32),
        pltpu.VMEM((C_WIDTH, n), F32),
        pltpu.VMEM((2 * D_STATE, n), F32),
        pltpu.VMEM((n, LANES), F32),
        pltpu.VMEM((n, B_WIDTH), F32),
        pltpu.VMEM((C_WIDTH, n), F32),
        pltpu.VMEM((n, D_MODEL), F32),
    ]
    return pl.pallas_call(
        _sample_kernel, out_shape=out_shape, grid=(DEPTH, n // tb), in_specs=in_specs,
        out_specs=out_specs, scratch_shapes=scratch, name="sample_step",
        compiler_params=pltpu.CompilerParams(
            dimension_semantics=("arbitrary", "arbitrary"), vmem_limit_bytes=VMEM_LIMIT),
    )(sinks, x, w_in, lnvg, lnvb, ws0, bs0, convw, convb, cst, dtb, alog, emat, kst, vst, hst,
      dskip, gnw, w_out, ln1g, ln1b, w1, w2, ln2g, ln2b)


PROMPT_TL = 1024
PROMPT_TM = 1024


def kernel(x_prompt, x_sample, state_attn_k, state_attn_v, state_conv, state_ssm, w_in, ln_v_g,
           ln_v_b, w_s, b_s, sinks, conv_w, conv_b, dt_bias, a_log, d_skip, gn_w, w_out, ln1_g,
           ln1_b, w1, w2, ln2_g, ln2_b):
    bsz, seq, _ = x_prompt.shape
    n_s = x_sample.shape[0]
    d_in = w_in.shape[-1]

    w_in_b = jnp.pad(w_in, ((0, 0), (0, 0), (0, D_IN_PAD - d_in))).astype(BF16)
    w_out_b = w_out.astype(BF16)
    w1_b = w1.astype(BF16)
    w2_b = w2.astype(BF16)
    pad_h = ((0, 0), (0, LANES - SSM_HEADS))
    dtb_p = jnp.pad(dt_bias, pad_h)[:, None, :]
    alog_p = jnp.pad(a_log, pad_h)[:, None, :]
    dskip_e = jnp.repeat(d_skip, HEAD_DIM, axis=-1)[:, None, :]
    bs_e = jnp.repeat(jnp.swapaxes(b_s, 1, 2), HEAD_DIM, axis=-1)
    ws0_e = jnp.repeat(w_s[:, :, 0, 0], HEAD_DIM, axis=-1)[:, None, :]
    bs0_e = jnp.repeat(b_s[:, :, 0], HEAD_DIM, axis=-1)[:, None, :]
    emat = (lax.broadcasted_iota(jnp.int32, (LANES, C_WIDTH), 0)
            == lax.broadcasted_iota(jnp.int32, (LANES, C_WIDTH), 1) // HEAD_DIM).astype(BF16)
    row = lambda a: a[:, None, :]
    lnvg, lnvb, convb, gnw = row(ln_v_g), row(ln_v_b), row(conv_b), row(gn_w)
    ln1g, ln1b, ln2g, ln2b = row(ln1_g), row(ln1_b), row(ln2_g), row(ln2_b)
    kst = state_attn_k.reshape(DEPTH, n_s, CHUNK, KV_WIDTH)
    vst = state_attn_v.reshape(DEPTH, n_s, CHUNK, KV_WIDTH)
    hst = state_ssm.reshape(DEPTH, n_s, C_WIDTH, D_STATE)
    cst = jnp.swapaxes(state_conv, 1, 2)

    ys, vns, ksm, vsm, csm, hsm = _sample_call(
        x_sample.reshape(n_s, D_MODEL), sinks, w_in_b, lnvg, lnvb, ws0_e, bs0_e, conv_w, convb, cst,
        dtb_p, alog_p, emat, kst, vst, hst, dskip_e, gnw, w_out_b, ln1g, ln1b, w1_b, w2_b, ln2g, ln2b)

    yp = x_prompt
    kp, vp, cp, hp = [], [], [], []
    for l in range(DEPTH):
        x1, k_l, v_l, c_l, h_l = _mixer_call(
            l, yp, sinks, w_in_b, lnvg, lnvb, w_s, bs_e, conv_w, convb,
            dt_bias[:, :, None], a_log[:, :, None], dskip_e, gnw, w_out_b, ln1g, ln1b, tl=PROMPT_TL)
        yp = _ffn_call(l, x1.reshape(bsz * seq, D_MODEL), w1_b, w2_b, ln2g, ln2b,
                       tm=PROMPT_TM).reshape(bsz, seq, D_MODEL)
        kp.append(k_l); vp.append(v_l); cp.append(c_l); hp.append(h_l)

    kv_p = (DEPTH, bsz, CHUNK, 2, HEAD_DIM)
    kv_s = (DEPTH, n_s, 1, 2, HEAD_DIM)
    ssm_shape = (SSM_HEADS, HEAD_DIM, D_STATE)
    return (yp, ys.reshape(n_s, 1, D_MODEL),
            jnp.stack(kp).reshape(kv_p), jnp.stack(vp).reshape(kv_p),
            jnp.stack(cp), jnp.stack(hp).reshape((DEPTH, bsz) + ssm_shape),
            ksm.reshape(kv_s), vsm.reshape(kv_s),
            jnp.swapaxes(csm, 1, 2),
            hsm.reshape((DEPTH, n_s) + ssm_shape),
            vns.reshape(DEPTH, n_s, 1, A_WIDTH))
```

```python
import functools

import jax
import jax.numpy as jnp
from jax import lax
from jax.experimental import pallas as pl
from jax.experimental.pallas import tpu as pltpu

F32 = jnp.float32
BF16 = jnp.bfloat16

D_MODEL = 1024
DEPTH = 4
HEAD_DIM = 64
A_WIDTH = 256
B_WIDTH = 256
KV_WIDTH = 128
C_WIDTH = 512
CONV_DIM = 1024
CONV_W = 4
SSM_HEADS = 8
D_STATE = 128
D_FF = 4096
CHUNK = 128
ALPHA = (2 * DEPTH) ** 0.25
LN_EPS = 1e-5
RMS_EPS = 1e-6
ATT_SCALE = HEAD_DIM ** -0.5
LOG2E = 1.4426950408889634
Q_SCALE = ATT_SCALE * LOG2E
NEG = -1e30

OFF_AU, OFF_AV, OFF_Q, OFF_K, OFF_V, OFF_Z = 0, 256, 512, 768, 896, 1024
OFF_XBC = 1536
OFF_DT = 2560
D_IN_PAD = 2688
LANES = 128
SUBLANES = 8
VMEM_LIMIT = 56 * 1024 * 1024

NT_DIMS = (((1,), (1,)), ((), ()))
PIECE = 256


def _dot(a, b):
    return jnp.dot(a, b, preferred_element_type=F32)


def _dot_nt(a, b):
    return lax.dot_general(a, b, NT_DIMS, preferred_element_type=F32)


def _layer_norm(x, g, b):
    mu = jnp.mean(x, axis=-1, keepdims=True)
    xc = x - mu
    var = jnp.mean(xc * xc, axis=-1, keepdims=True)
    return xc * lax.rsqrt(var + LN_EPS) * g + b


def _silu(x):
    return x * jax.nn.sigmoid(x)


def _split3(x):
    hi = x.astype(BF16)
    r = x - hi.astype(F32)
    mid = r.astype(BF16)
    lo = (r - mid.astype(F32)).astype(BF16)
    return hi, mid, lo


def _dot_exact_rhs(x, m_bf16):
    hi, mid, lo = _split3(x)
    return _dot(hi, m_bf16) + _dot(mid, m_bf16) + _dot(lo, m_bf16)


def _gated_rmsnorm(y, xs, gate, dskip, gnw):
    y2 = (y + dskip * xs) * gate
    ms = jnp.mean(y2 * y2, axis=-1, keepdims=True)
    return y2 * lax.rsqrt(ms + RMS_EPS) * gnw


def _mixer_kernel(sinks_ref, x_ref, w_in_ref, lnvg_ref, lnvb_ref, ws_ref, bs_ref,
                  convw_ref, convb_ref, dtb_ref, alog_ref, dskip_ref, gnw_ref,
                  w_out_ref, ln1g_ref, ln1b_ref,
                  x1_ref, klast_ref, vlast_ref, convnew_ref, ssm_ref,
                  proj_ref, xbc_ref, act_ref, dtr_ref, mix_ref, kprev_ref, vprev_ref,
                  h_ref, wstk_ref, xb_ref, mo_ref, *, tl, layer):
    t = pl.program_id(1)
    nt = pl.num_programs(1)

    row_i = lax.broadcasted_iota(jnp.int32, (CHUNK, CHUNK), 0)
    col_i = lax.broadcasted_iota(jnp.int32, (CHUNK, CHUNK), 1)
    causal = col_i <= row_i
    lane_lo = col_i < HEAD_DIM

    @pl.when(t == 0)
    def _init():
        kprev_ref[...] = jnp.zeros_like(kprev_ref)
        vprev_ref[...] = jnp.zeros_like(vprev_ref)
        h_ref[...] = jnp.zeros_like(h_ref)
        xbc_ref[0:SUBLANES, :] = jnp.zeros((SUBLANES, CONV_DIM), F32)
        for h in range(4):
            wstk_ref[h * CHUNK:(h + 1) * CHUNK, :] = jnp.where(causal, ws_ref[h], 0.0).astype(BF16)

    half = tl // 2

    def project_pieces(hf):
        rows = slice(hf * half, (hf + 1) * half)

        def cast():
            xb_ref[rows, :] = x_ref[0, rows, :].astype(BF16)

        epilogue = {
            OFF_AU: jax.nn.gelu,
            OFF_AV: lambda r: _layer_norm(jax.nn.gelu(r), lnvg_ref[...], lnvb_ref[...]),
            OFF_Q: lambda r: r * Q_SCALE,
            OFF_Z: _silu,
            OFF_Z + PIECE: _silu,
        }

        def main(c0):
            def run():
                res = _dot(xb_ref[rows, :], w_in_ref[:, c0:c0 + PIECE])
                proj_ref[rows, c0:c0 + PIECE] = epilogue.get(c0, lambda r: r)(res)
            return run

        def xbc(c0):
            def run():
                cols = slice(c0, c0 + PIECE)
                res = _dot(xb_ref[rows, :], w_in_ref[:, OFF_XBC + c0:OFF_XBC + c0 + PIECE])
                xh = jnp.concatenate([xbc_ref[hf * half:hf * half + SUBLANES, cols], res], axis=0)
                xbc_ref[SUBLANES + hf * half:SUBLANES + (hf + 1) * half, cols] = res
                acc = xh * convw_ref[0:1, cols]
                for i in range(1, CONV_W):
                    acc = pltpu.roll(acc, 1, 0) + xh * convw_ref[i:i + 1, cols]
                act_ref[rows, cols] = _silu(acc[SUBLANES:] + convb_ref[:, cols])
            return run

        def dt():
            dtr_ref[rows, :] = _dot(xb_ref[rows, :], w_in_ref[:, OFF_DT:D_IN_PAD])

        return ([cast] + [main(c0) for c0 in range(0, OFF_XBC, PIECE)]
                + [xbc(c0) for c0 in range(0, CONV_DIM, PIECE)] + [dt])

    def out_pieces(hf):
        rows = slice(hf * half, (hf + 1) * half)

        def part(c0):
            def run():
                mo_ref[rows, c0:c0 + PIECE] = _dot(mix_ref[rows, :], w_out_ref[:, c0:c0 + PIECE])
            return run
        return [part(c0) for c0 in range(0, D_MODEL, PIECE)]

    def norm_pieces(hf):
        def piece(r0):
            def run():
                rows = slice(r0, r0 + CHUNK)
                x1_ref[0, rows, :] = _layer_norm(ALPHA * x_ref[0, rows, :] + mo_ref[rows, :],
                                                 ln1g_ref[...], ln1b_ref[...])
            return run
        return [piece(r0) for r0 in range(hf * half, (hf + 1) * half, CHUNK)]

    def interleave(mxu_pieces, vpu_pieces):
        n_m, n_v = len(mxu_pieces), len(vpu_pieces)
        done = 0
        for i, piece in enumerate(mxu_pieces):
            piece()
            upto = (n_v * (i + 1)) // n_m
            for other in vpu_pieces[done:upto]:
                other()
            done = upto

    triu_b = jnp.where(row_i <= col_i, 1.0, 0.0).astype(BF16)
    a_col = -jnp.exp(alog_ref[...])
    zpad = jnp.zeros((CHUNK - SSM_HEADS, CHUNK), F32)
    lane256 = lax.broadcasted_iota(jnp.int32, (CHUNK, A_WIDTH), 1)
    r256 = lax.broadcasted_iota(jnp.int32, (2 * CHUNK, 2 * CHUNK), 0)
    c256 = lax.broadcasted_iota(jnp.int32, (2 * CHUNK, 2 * CHUNK), 1)
    t256 = jnp.bitwise_and(r256, CHUNK - 1)
    win_mask = (c256 > t256) & (c256 <= t256 + CHUNK)
    rcol = lax.broadcasted_iota(jnp.int32, (2 * CHUNK, 1), 0)
    sink_a = jnp.where(rcol < CHUNK, sinks_ref[layer, 0], sinks_ref[layer, 3]) * LOG2E
    sink_b = jnp.where(rcol < CHUNK, sinks_ref[layer, 1], sinks_ref[layer, 2]) * LOG2E

    def chunk_a(c):
        rows = slice(c * CHUNK, (c + 1) * CHUNK)
        u = proj_ref[rows, OFF_AU:OFF_AU + A_WIDTH]
        vn = proj_ref[rows, OFF_AV:OFF_AV + A_WIDTH]
        pa = _dot(wstk_ref[...], vn.astype(BF16))
        mix_a = jnp.where(
            lane256 < HEAD_DIM, pa[0:CHUNK],
            jnp.where(lane256 < 2 * HEAD_DIM, pa[CHUNK:2 * CHUNK],
                      jnp.where(lane256 < 3 * HEAD_DIM, pa[2 * CHUNK:3 * CHUNK], pa[3 * CHUNK:])))
        mix_ref[rows, 0:A_WIDTH] = (u * (mix_a + bs_ref[...])).astype(BF16)

    def chunk_b(c):
        rows = slice(c * CHUNK, (c + 1) * CHUNK)
        q01 = proj_ref[rows, OFF_Q:OFF_Q + LANES]
        q23 = proj_ref[rows, OFF_Q + LANES:OFF_Q + 2 * LANES]
        kc = proj_ref[rows, OFF_K:OFF_K + KV_WIDTH]
        vc = proj_ref[rows, OFF_V:OFF_V + KV_WIDTH]
        if c == 0:
            kp, vp = kprev_ref[...], vprev_ref[...]
        else:
            prev = slice((c - 1) * CHUNK, c * CHUNK)
            kp = proj_ref[prev, OFF_K:OFF_K + KV_WIDTH]
            vp = proj_ref[prev, OFF_V:OFF_V + KV_WIDTH]
        k2 = jnp.concatenate([kp, kc], axis=0)
        v2 = jnp.concatenate([vp, vc], axis=0)
        k2b = k2.astype(BF16)
        v2b = v2.astype(BF16)
        k2rb = pltpu.roll(k2, HEAD_DIM, 1).astype(BF16)
        v2rb = pltpu.roll(v2, HEAD_DIM, 1).astype(BF16)
        q_a = jnp.concatenate([jnp.where(lane_lo, q01, 0.0), jnp.where(lane_lo, 0.0, q23)], axis=0)
        q_b = jnp.concatenate([jnp.where(lane_lo, 0.0, q01), jnp.where(lane_lo, q23, 0.0)], axis=0)
        vis = win_mask & (c256 >= jnp.where(t == 0, CHUNK, 0)) if c == 0 else win_mask

        def attend(qm, kk, vv, sink2):
            s = jnp.where(vis, _dot_nt(qm.astype(BF16), kk), NEG)
            m = jnp.maximum(jnp.max(s, axis=-1, keepdims=True), sink2)
            p = jnp.exp2(s - m)
            den = jnp.sum(p, axis=-1, keepdims=True) + jnp.exp2(sink2 - m)
            return _dot(p.astype(BF16), vv) * (1.0 / den)

        o_a = attend(q_a, k2b, v2b, sink_a)
        o_b = attend(q_b, k2rb, v2rb, sink_b)
        mix_ref[rows, A_WIDTH:A_WIDTH + LANES] = jnp.where(
            lane_lo, o_a[0:CHUNK], o_b[0:CHUNK]).astype(BF16)
        mix_ref[rows, A_WIDTH + LANES:A_WIDTH + 2 * LANES] = jnp.where(
            lane_lo, o_b[CHUNK:], o_a[CHUNK:]).astype(BF16)

    def chunk_c(c):
        rows = slice(c * CHUNK, (c + 1) * CHUNK)
        xs = act_ref[rows, 0:C_WIDTH]
        bm = act_ref[rows, C_WIDTH:C_WIDTH + 2 * D_STATE]
        cm = act_ref[rows, C_WIDTH + 2 * D_STATE:CONV_DIM]
        gate = proj_ref[rows, OFF_Z:OFF_Z + C_WIDTH]
        dt_row = jax.nn.softplus(dtr_ref[rows, :].T[0:SSM_HEADS] + dtb_ref[...])
        cum_row = _dot_exact_rhs(dt_row * a_col, triu_b) * LOG2E
        cum = jnp.concatenate([cum_row, zpad], axis=0).T
        toend_row = jnp.exp2(cum_row[:, CHUNK - 1:CHUNK] - cum_row) * dt_row
        ecl = jnp.exp2(cum_row[:, CHUNK - 1:CHUNK])
        xt = xs.T
        yts = []
        for g in range(2):
            bg = bm[:, g * D_STATE:(g + 1) * D_STATE]
            cg = cm[:, g * D_STATE:(g + 1) * D_STATE]
            cb = _dot_nt(cg.astype(BF16), bg.astype(BF16))
            xw = []
            for hh in range(4):
                h = g * 4 + hh
                hs = slice(h * HEAD_DIM, (h + 1) * HEAD_DIM)
                cc = jnp.broadcast_to(cum[:, h:h + 1], (CHUNK, CHUNK))
                dec = jnp.exp2(jnp.where(causal, cc - cum_row[h:h + 1, :], NEG))
                m_h = cb * dec * dt_row[h:h + 1, :]
                ce_h = cg * jnp.exp2(cc)
                l_h = jnp.concatenate([m_h, ce_h], axis=1).astype(BF16)
                r_h = jnp.concatenate([xt[hs], h_ref[hs, :]], axis=1).astype(BF16)
                yts.append(_dot_nt(r_h, l_h))
                xw.append(xt[hs] * toend_row[h:h + 1, :])
            gs = slice(g * 4 * HEAD_DIM, (g + 1) * 4 * HEAD_DIM)
            upd = _dot(jnp.concatenate(xw, axis=0).astype(BF16), bg.astype(BF16))
            keep = jnp.concatenate(
                [jnp.broadcast_to(ecl[g * 4 + hh:g * 4 + hh + 1, :], (HEAD_DIM, D_STATE))
                 for hh in range(4)], axis=0)
            h_ref[gs, :] = h_ref[gs, :] * keep + upd
        y = jnp.concatenate(yts, axis=0).T
        out_c = _gated_rmsnorm(y, xs, gate, dskip_ref[...], gnw_ref[...])
        mix_ref[rows, A_WIDTH + B_WIDTH:D_MODEL] = out_c.astype(BF16)

    cph = half // CHUNK

    def chunks(hf):
        for c in range(hf * cph, (hf + 1) * cph):
            chunk_a(c)
            chunk_b(c)
            chunk_c(c)

    for hf in range(2):
        for piece in project_pieces(hf):
            piece()
    chunks(0)
    for piece in out_pieces(0):
        piece()
    chunks(1)
    interleave(out_pieces(1), norm_pieces(0))
    for piece in norm_pieces(1):
        piece()

    last = slice(tl - CHUNK, tl)
    kprev_ref[...] = proj_ref[last, OFF_K:OFF_K + KV_WIDTH]
    vprev_ref[...] = proj_ref[last, OFF_V:OFF_V + KV_WIDTH]
    tail = xbc_ref[SUBLANES + tl - (CONV_W - 1):SUBLANES + tl, :]
    xbc_ref[SUBLANES - (CONV_W - 1):SUBLANES, :] = tail

    @pl.when(t == nt - 1)
    def _final():
        klast_ref[0] = kprev_ref[...]
        vlast_ref[0] = vprev_ref[...]
        convnew_ref[0] = tail
        ssm_ref[0] = h_ref[...]


def _full(shape):
    n = len(shape)
    return pl.BlockSpec(shape, lambda *_: (0,) * n)


def _layer(shape, l, single_buffer=False):
    n = len(shape)
    mode = pl.Buffered(1) if single_buffer else None
    return pl.BlockSpec((None,) + tuple(shape), lambda *_: (l,) + (0,) * n, pipeline_mode=mode)


def _mixer_call(l, x, sinks, w_in, lnvg, lnvb, ws, bs_exp, convw, convb, dtb, alog, dskip, gnw,
                w_out, ln1g, ln1b, *, tl):
    bsz, seq, _ = x.shape
    nt = seq // tl
    kern = functools.partial(_mixer_kernel, tl=tl, layer=l)
    in_specs = [
        pl.BlockSpec(memory_space=pltpu.SMEM),
        pl.BlockSpec((1, tl, D_MODEL), lambda b, t: (b, t, 0)),
        _layer((D_MODEL, D_IN_PAD), l, single_buffer=True),
        _layer((1, A_WIDTH), l), _layer((1, A_WIDTH), l),
        _layer((4, CHUNK, CHUNK), l), _layer((CHUNK, A_WIDTH), l),
        _layer((CONV_W, CONV_DIM), l), _layer((1, CONV_DIM), l),
        _layer((SSM_HEADS, 1), l), _layer((SSM_HEADS, 1), l),
        _layer((1, C_WIDTH), l), _layer((1, C_WIDTH), l),
        _layer((D_MODEL, D_MODEL), l, single_buffer=True),
        _layer((1, D_MODEL), l), _layer((1, D_MODEL), l),
    ]
    out_shape = (
        jax.ShapeDtypeStruct((bsz, seq, D_MODEL), F32),
        jax.ShapeDtypeStruct((bsz, CHUNK, KV_WIDTH), F32),
        jax.ShapeDtypeStruct((bsz, CHUNK, KV_WIDTH), F32),
        jax.ShapeDtypeStruct((bsz, CONV_W - 1, CONV_DIM), F32),
        jax.ShapeDtypeStruct((bsz, C_WIDTH, D_STATE), F32),
    )
    out_specs = (
        pl.BlockSpec((1, tl, D_MODEL), lambda b, t: (b, t, 0)),
        pl.BlockSpec((1, CHUNK, KV_WIDTH), lambda b, t: (b, 0, 0)),
        pl.BlockSpec((1, CHUNK, KV_WIDTH), lambda b, t: (b, 0, 0)),
        pl.BlockSpec((1, CONV_W - 1, CONV_DIM), lambda b, t: (b, 0, 0)),
        pl.BlockSpec((1, C_WIDTH, D_STATE), lambda b, t: (b, 0, 0)),
    )
    scratch = [
        pltpu.VMEM((tl, OFF_XBC), F32),
        pltpu.VMEM((SUBLANES + tl, CONV_DIM), F32),
        pltpu.VMEM((tl, CONV_DIM), F32),
        pltpu.VMEM((tl, LANES), F32),
        pltpu.VMEM((tl, D_MODEL), BF16),
        pltpu.VMEM((CHUNK, KV_WIDTH), F32),
        pltpu.VMEM((CHUNK, KV_WIDTH), F32),
        pltpu.VMEM((C_WIDTH, D_STATE), F32),
        pltpu.VMEM((4 * CHUNK, CHUNK), BF16),
        pltpu.VMEM((tl, D_MODEL), BF16),
        pltpu.VMEM((tl, D_MODEL), F32),
    ]
    return pl.pallas_call(
        kern, out_shape=out_shape, grid=(bsz, nt), in_specs=in_specs, out_specs=out_specs,
        scratch_shapes=scratch, name="prompt_mixer",
        compiler_params=pltpu.CompilerParams(
            dimension_semantics=("arbitrary", "arbitrary"), vmem_limit_bytes=VMEM_LIMIT),
    )(sinks, x, w_in, lnvg, lnvb, ws, bs_exp, convw, convb, dtb, alog, dskip, gnw, w_out, ln1g, ln1b)


CAST_ROWS = 256


def _cast_kernel(w_ref, o_ref):
    cols = w_ref.shape[-1]
    pad = o_ref.shape[-1] - cols
    o_ref[:, 0:cols] = w_ref[...].astype(BF16)
    if pad:
        o_ref[:, cols:] = jnp.zeros((o_ref.shape[0], pad), BF16)


def _to_bf16(w, out_cols):
    depth, rows, cols = w.shape
    return pl.pallas_call(
        _cast_kernel, out_shape=jax.ShapeDtypeStruct((depth, rows, out_cols), BF16),
        grid=(depth, rows // CAST_ROWS),
        in_specs=[pl.BlockSpec((None, CAST_ROWS, cols), lambda l, i: (l, i, 0))],
        out_specs=pl.BlockSpec((None, CAST_ROWS, out_cols), lambda l, i: (l, i, 0)),
        name="to_bf16",
        compiler_params=pltpu.CompilerParams(
            dimension_semantics=("arbitrary", "arbitrary"), vmem_limit_bytes=VMEM_LIMIT),
    )(w)


FF_CHUNK = 1024


def _ffn(x, w1_ref, w2_ref, g_ref, b_ref):
    xb = x.astype(BF16)
    acc = jnp.zeros(x.shape, F32)
    for c in range(D_FF // FF_CHUNK):
        sl = slice(c * FF_CHUNK, (c + 1) * FF_CHUNK)
        h = jnp.maximum(_dot(xb, w1_ref[:, sl]), 0.0)
        acc = acc + _dot((h * h).astype(BF16), w2_ref[sl, :])
    return _layer_norm(ALPHA * x + acc, g_ref[...], b_ref[...])


def _ffn_kernel(x_ref, w1_ref, w2_ref, g_ref, b_ref, o_ref):
    o_ref[...] = _ffn(x_ref[...], w1_ref, w2_ref, g_ref, b_ref)


def _ffn_call(l, x2d, w1, w2, g, b, *, tm):
    rows = x2d.shape[0]
    return pl.pallas_call(
        _ffn_kernel, out_shape=jax.ShapeDtypeStruct((rows, D_MODEL), F32),
        grid=(rows // tm,),
        in_specs=[pl.BlockSpec((tm, D_MODEL), lambda i: (i, 0)),
                  _layer((D_MODEL, D_FF), l, single_buffer=True),
                  _layer((D_FF, D_MODEL), l, single_buffer=True),
                  _layer((1, D_MODEL), l), _layer((1, D_MODEL), l)],
        out_specs=pl.BlockSpec((tm, D_MODEL), lambda i: (i, 0)),
        name="ffn",
        compiler_params=pltpu.CompilerParams(
            dimension_semantics=("arbitrary",), vmem_limit_bytes=VMEM_LIMIT),
    )(x2d, w1, w2, g, b)


def _s1_kernel(x_ref, w_in_ref, lnvg_ref, lnvb_ref, ws0_ref, bs0_ref, convw_ref, convb_ref,
               cst_ref, dtb_ref, alog_ref, e_ref,
               vn_ref, outa_ref, q_ref, k_ref, v_ref, z_ref, cnew_ref, xs_ref, bm_ref,
               xT_ref, cT_ref, dec_ref):
    xb = x_ref[...].astype(BF16)
    proj = _dot(xb, w_in_ref[...])
    u = jax.nn.gelu(proj[:, OFF_AU:OFF_AU + A_WIDTH])
    vn = _layer_norm(jax.nn.gelu(proj[:, OFF_AV:OFF_AV + A_WIDTH]), lnvg_ref[...], lnvb_ref[...])
    vn_ref[...] = vn
    outa_ref[...] = u * (ws0_ref[...] * vn + bs0_ref[...])
    q_ref[...] = proj[:, OFF_Q:OFF_Q + B_WIDTH]
    k_ref[...] = proj[:, OFF_K:OFF_K + KV_WIDTH]
    v_ref[...] = proj[:, OFF_V:OFF_V + KV_WIDTH]
    z_ref[...] = proj[:, OFF_Z:OFF_Z + C_WIDTH]
    xbc = proj[:, OFF_XBC:OFF_DT]
    acc = convb_ref[...] + xbc * convw_ref[CONV_W - 1:CONV_W, :]
    for i in range(CONV_W - 1):
        acc = acc + cst_ref[i] * convw_ref[i:i + 1, :]
    cnew_ref[0] = cst_ref[1]
    cnew_ref[1] = cst_ref[2]
    cnew_ref[2] = xbc
    act = _silu(acc)
    xs = act[:, 0:C_WIDTH]
    xs_ref[...] = xs
    bm_ref[...] = act[:, C_WIDTH:C_WIDTH + 2 * D_STATE]
    cT_ref[...] = act[:, C_WIDTH + 2 * D_STATE:CONV_DIM].T
    dt = jax.nn.softplus(proj[:, OFF_DT:D_IN_PAD] + dtb_ref[...])
    dec_ref[...] = jnp.exp(dt * (-jnp.exp(alog_ref[...])))
    dt_exp = _dot_exact_rhs(dt, e_ref[...])
    xT_ref[...] = (xs * dt_exp).T


S2_TB = SUBLANES


def _s2_kernel(sinks_ref, q_ref, kn_ref, vn_ref, kst_ref, vst_ref, h_ref, xT_ref, bm_ref,
               cT_ref, dec_ref, ob_ref, hnew_ref, yT_ref, *, layer, step):
    i = step
    tb = S2_TB
    lane = lax.broadcasted_iota(jnp.int32, (tb, LANES), 1)
    lo = lane < HEAD_DIM

    q01 = q_ref[:, 0:LANES]
    q23 = q_ref[:, LANES:2 * LANES]
    qh = jnp.concatenate([
        jnp.where(lo, q01, 0.0),
        jnp.where(lo, pltpu.roll(q01, HEAD_DIM, 1), 0.0),
        jnp.where(lo, 0.0, pltpu.roll(q23, HEAD_DIM, 1)),
        jnp.where(lo, 0.0, q23)], axis=0)
    qhb = qh.astype(BF16)
    rowi = lax.broadcasted_iota(jnp.int32, (4 * tb, LANES), 0)
    coli = lax.broadcasted_iota(jnp.int32, (4 * tb, LANES), 1)
    tok = jnp.bitwise_and(rowi, tb - 1)
    s = jnp.zeros((4 * tb, LANES), F32)
    for j in range(tb):
        sj = _dot_nt(qhb, kst_ref[j].astype(BF16))
        s = jnp.where(tok == j, sj, s)
    s = jnp.where(coli >= 1, s * ATT_SCALE, NEG)
    kn4 = jnp.concatenate([kn_ref[...]] * 4, axis=0)
    vn4 = jnp.concatenate([vn_ref[...]] * 4, axis=0)
    s_new = jnp.sum(qh * kn4, axis=-1, keepdims=True) * ATT_SCALE
    head = jnp.right_shift(rowi[:, 0:1], 3)
    sink = jnp.where(head == 0, sinks_ref[layer, 0],
                     jnp.where(head == 1, sinks_ref[layer, 1],
                               jnp.where(head == 2, sinks_ref[layer, 2], sinks_ref[layer, 3])))
    m = jnp.maximum(jnp.maximum(jnp.max(s, axis=-1, keepdims=True), s_new), sink)
    p = jnp.exp(s - m)
    p_new = jnp.exp(s_new - m)
    inv = 1.0 / (jnp.sum(p, axis=-1, keepdims=True) + p_new + jnp.exp(sink - m))
    p = p * inv
    o = (p_new * inv) * vn4
    for j in range(tb):
        pj = jnp.where(tok == j, p, 0.0).astype(BF16)
        o = o + _dot(pj, vst_ref[j].astype(BF16))
    ob_ref[:, 0:LANES] = jnp.where(lo, o[0:tb], pltpu.roll(o[tb:2 * tb], HEAD_DIM, 1))
    ob_ref[:, LANES:2 * LANES] = jnp.where(lo, pltpu.roll(o[2 * tb:3 * tb], HEAD_DIM, 1), o[3 * tb:])

    @pl.when(i == 0)
    def _():
        yT_ref[...] = jnp.zeros_like(yT_ref)

    r8 = lax.broadcasted_iota(jnp.int32, (tb, D_STATE), 0)
    c128 = lax.broadcasted_iota(jnp.int32, (LANES, LANES), 1)
    hp_g = C_WIDTH // 2
    b0 = pl.multiple_of(i * tb, tb)
    zrows = jnp.zeros((LANES - tb, tb * D_STATE), BF16)
    to_front = jnp.bitwise_and(LANES - b0, LANES - 1)
    for g in range(2):
        gs = slice(g * hp_g, (g + 1) * hp_g)
        ns = slice(g * D_STATE, (g + 1) * D_STATE)
        bblk = bm_ref[pl.ds(b0, tb), ns]
        bdiag = jnp.concatenate([jnp.where(r8 == j, bblk, 0.0) for j in range(tb)], axis=1)
        rhs = jnp.concatenate([bdiag.astype(BF16), zrows], axis=0)
        xg = pltpu.roll(xT_ref[gs, :], to_front, 1).astype(BF16)
        upd = _dot(xg, rhs)
        hns = []
        cms = []
        for j in range(tb):
            drow = dec_ref[pl.ds(b0 + j, 1), :]
            parts = []
            for hh in range(4):
                h = g * 4 + hh
                rs = slice(h * HEAD_DIM, (h + 1) * HEAD_DIM)
                hn = (h_ref[j, rs, :] * jnp.broadcast_to(drow[:, h:h + 1], (HEAD_DIM, D_STATE))
                      + upd[hh * HEAD_DIM:(hh + 1) * HEAD_DIM, j * D_STATE:(j + 1) * D_STATE])
                hnew_ref[j, rs, :] = hn
                parts.append(hn)
            hns.append(jnp.concatenate(parts, axis=0).astype(BF16))
            cms.append(jnp.where(c128 == b0 + j, cT_ref[ns, :], 0.0).astype(BF16))
        yT_ref[gs, :] += _dot(jnp.concatenate(hns, axis=1), jnp.concatenate(cms, axis=0))


def _s3_kernel(yT_ref, xs_ref, z_ref, dskip_ref, gnw_ref, outa_ref, outb_ref, x_ref, w_out_ref,
               g_ref, b_ref, x1_ref):
    out_c = _gated_rmsnorm(yT_ref[...].T, xs_ref[...], _silu(z_ref[...]), dskip_ref[...], gnw_ref[...])
    mix = jnp.concatenate([outa_ref[...], outb_ref[...], out_c], axis=1).astype(BF16)
    x1_ref[...] = _layer_norm(ALPHA * x_ref[...] + _dot(mix, w_out_ref[...]), g_ref[...], b_ref[...])


def _sample_kernel(sinks_ref, x_ref, w_in_ref, lnvg_ref, lnvb_ref, ws0_ref, bs0_ref, convw_ref,
                   convb_ref, cst_ref, dtb_ref, alog_ref, e_ref, kst_ref, vst_ref, h_ref,
                   dskip_ref, gnw_ref, w_out_ref, ln1g_ref, ln1b_ref, w1_ref, w2_ref, ln2g_ref, ln2b_ref,
                   ys_ref, vns_ref, ksm_ref, vsm_ref, csm_ref, hnew_ref,
                   res_scr, outa_scr, q_scr, z_scr, xs_scr, bm_scr, xT_scr, cT_scr, dec_scr,
                   outb_scr, yT_scr, x1_scr):
    l = pl.program_id(0)
    i = pl.program_id(1)
    last = pl.num_programs(1) - 1

    @pl.when(jnp.logical_and(l == 0, i == 0))
    def _():
        res_scr[...] = x_ref[...]

    @pl.when(i == 0)
    def _():
        _s1_kernel(res_scr, w_in_ref, lnvg_ref, lnvb_ref, ws0_ref, bs0_ref, convw_ref, convb_ref,
                   cst_ref, dtb_ref, alog_ref, e_ref,
                   vns_ref, outa_scr, q_scr, ksm_ref, vsm_ref, z_scr, csm_ref, xs_scr, bm_scr,
                   xT_scr, cT_scr, dec_scr)

    rows = pl.ds(pl.multiple_of(i * S2_TB, S2_TB), S2_TB)
    _s2_kernel(sinks_ref, q_scr.at[rows], ksm_ref.at[rows], vsm_ref.at[rows], kst_ref, vst_ref,
               h_ref, xT_scr, bm_scr, cT_scr, dec_scr, outb_scr.at[rows], hnew_ref, yT_scr,
               layer=l, step=i)

    @pl.when(i == last)
    def _():
        _s3_kernel(yT_scr, xs_scr, z_scr, dskip_ref, gnw_ref, outa_scr, outb_scr, res_scr,
                   w_out_ref, ln1g_ref, ln1b_ref, x1_scr)
        res_scr[...] = _ffn(x1_scr[...], w1_ref, w2_ref, ln2g_ref, ln2b_ref)
        ys_ref[...] = res_scr[...]


def _sample_call(x, sinks, w_in, lnvg, lnvb, ws0, bs0, convw, convb, cst, dtb, alog, emat,
                 kst, vst, hst, dskip, gnw, w_out, ln1g, ln1b, w1, w2, ln2g, ln2b):
    n = x.shape[0]
    tb = S2_TB

    def per_layer(shape, single_buffer=False):
        k = len(shape)
        mode = pl.Buffered(1) if single_buffer else None
        return pl.BlockSpec((None,) + tuple(shape), lambda l, i: (l,) + (0,) * k, pipeline_mode=mode)

    def per_block(shape):
        k = len(shape) - 1
        return pl.BlockSpec((None,) + tuple(shape), lambda l, i: (l, i) + (0,) * k)

    in_specs = [
        pl.BlockSpec(memory_space=pltpu.SMEM),
        _full((n, D_MODEL)),
        per_layer((D_MODEL, D_IN_PAD), True),
        per_layer((1, A_WIDTH)), per_layer((1, A_WIDTH)), per_layer((1, A_WIDTH)), per_layer((1, A_WIDTH)),
        per_layer((CONV_W, CONV_DIM)), per_layer((1, CONV_DIM)), per_layer((CONV_W - 1, n, CONV_DIM)),
        per_layer((1, LANES)), per_layer((1, LANES)), _full((LANES, C_WIDTH)),
        per_block((tb, CHUNK, KV_WIDTH)), per_block((tb, CHUNK, KV_WIDTH)),
        per_block((tb, C_WIDTH, D_STATE)),
        per_layer((1, C_WIDTH)), per_layer((1, C_WIDTH)),
        per_layer((D_MODEL, D_MODEL), True), per_layer((1, D_MODEL)), per_layer((1, D_MODEL)),
        per_layer((D_MODEL, D_FF), True), per_layer((D_FF, D_MODEL), True),
        per_layer((1, D_MODEL)), per_layer((1, D_MODEL)),
    ]
    out_shape = (
        jax.ShapeDtypeStruct((n, D_MODEL), F32),
        jax.ShapeDtypeStruct((DEPTH, n, A_WIDTH), F32),
        jax.ShapeDtypeStruct((DEPTH, n, KV_WIDTH), F32),
        jax.ShapeDtypeStruct((DEPTH, n, KV_WIDTH), F32),
        jax.ShapeDtypeStruct((DEPTH, CONV_W - 1, n, CONV_DIM), F32),
        jax.ShapeDtypeStruct((DEPTH, n, C_WIDTH, D_STATE), F32),
    )
    out_specs = (
        _full((n, D_MODEL)),
        per_layer((n, A_WIDTH)), per_layer((n, KV_WIDTH)), per_layer((n, KV_WIDTH)),
        per_layer((CONV_W - 1, n, CONV_DIM)),
        per_block((tb, C_WIDTH, D_STATE)),
    )
    scratch = [
        pltpu.VMEM((n, D_MODEL), F32),
        pltpu.VMEM((n, A_WIDTH), F32),
        pltpu.VMEM((n, B_WIDTH), F32),
        pltpu.VMEM((n, C_WIDTH), F32),
        pltpu.VMEM((n, C_WIDTH), F32),
        pltpu.VMEM((n, 2 * D_STATE), F32),
        pltpu.VMEM((C_WIDTH, n), F32),
        pltpu.VMEM((2 * D_STATE, n), F32),
        pltpu.VMEM((n, LANES), F32),
        pltpu.VMEM((n, B_WIDTH), F32),
        pltpu.VMEM((C_WIDTH, n), F32),
        pltpu.VMEM((n, D_MODEL), F32),
    ]
    return pl.pallas_call(
        _sample_kernel, out_shape=out_shape, grid=(DEPTH, n // tb), in_specs=in_specs,
        out_specs=out_specs, scratch_shapes=scratch, name="sample_step",
        compiler_params=pltpu.CompilerParams(
            dimension_semantics=("arbitrary", "arbitrary"), vmem_limit_bytes=VMEM_LIMIT),
    )(sinks, x, w_in, lnvg, lnvb, ws0, bs0, convw, convb, cst, dtb, alog, emat, kst, vst, hst,
      dskip, gnw, w_out, ln1g, ln1b, w1, w2, ln2g, ln2b)


PROMPT_TL = 1024
PROMPT_TM = 1024


def kernel(x_prompt, x_sample, state_attn_k, state_attn_v, state_conv, state_ssm, w_in, ln_v_g,
           ln_v_b, w_s, b_s, sinks, conv_w, conv_b, dt_bias, a_log, d_skip, gn_w, w_out, ln1_g,
           ln1_b, w1, w2, ln2_g, ln2_b):
    bsz, seq, _ = x_prompt.shape
    n_s = x_sample.shape[0]
    d_in = w_in.shape[-1]

    w_in_b = _to_bf16(w_in, D_IN_PAD)
    w_out_b = _to_bf16(w_out, D_MODEL)
    w1_b = _to_bf16(w1, D_FF)
    w2_b = _to_bf16(w2, D_MODEL)
    pad_h = ((0, 0), (0, LANES - SSM_HEADS))
    dtb_p = jnp.pad(dt_bias, pad_h)[:, None, :]
    alog_p = jnp.pad(a_log, pad_h)[:, None, :]
    dskip_e = jnp.repeat(d_skip, HEAD_DIM, axis=-1)[:, None, :]
    bs_e = jnp.repeat(jnp.swapaxes(b_s, 1, 2), HEAD_DIM, axis=-1)
    ws0_e = jnp.repeat(w_s[:, :, 0, 0], HEAD_DIM, axis=-1)[:, None, :]
    bs0_e = jnp.repeat(b_s[:, :, 0], HEAD_DIM, axis=-1)[:, None, :]
    emat = (lax.broadcasted_iota(jnp.int32, (LANES, C_WIDTH), 0)
            == lax.broadcasted_iota(jnp.int32, (LANES, C_WIDTH), 1) // HEAD_DIM).astype(BF16)
    row = lambda a: a[:, None, :]
    lnvg, lnvb, convb, gnw = row(ln_v_g), row(ln_v_b), row(conv_b), row(gn_w)
    ln1g, ln1b, ln2g, ln2b = row(ln1_g), row(ln1_b), row(ln2_g), row(ln2_b)
    kst = state_attn_k.reshape(DEPTH, n_s, CHUNK, KV_WIDTH)
    vst = state_attn_v.reshape(DEPTH, n_s, CHUNK, KV_WIDTH)
    hst = state_ssm.reshape(DEPTH, n_s, C_WIDTH, D_STATE)
    cst = jnp.swapaxes(state_conv, 1, 2)

    ys, vns, ksm, vsm, csm, hsm = _sample_call(
        x_sample.reshape(n_s, D_MODEL), sinks, w_in_b, lnvg, lnvb, ws0_e, bs0_e, conv_w, convb, cst,
        dtb_p, alog_p, emat, kst, vst, hst, dskip_e, gnw, w_out_b, ln1g, ln1b, w1_b, w2_b, ln2g, ln2b)

    yp = x_prompt
    kp, vp, cp, hp = [], [], [], []
    for l in range(DEPTH):
        x1, k_l, v_l, c_l, h_l = _mixer_call(
            l, yp, sinks, w_in_b, lnvg, lnvb, w_s, bs_e, conv_w, convb,
            dt_bias[:, :, None], a_log[:, :, None], dskip_e, gnw, w_out_b, ln1g, ln1b, tl=PROMPT_TL)
        yp = _ffn_call(l, x1.reshape(bsz * seq, D_MODEL), w1_b, w2_b, ln2g, ln2b,
                       tm=PROMPT_TM).reshape(bsz, seq, D_MODEL)
        kp.append(k_l); vp.append(v_l); cp.append(c_l); hp.append(h_l)

    kv_p = (DEPTH, bsz, CHUNK, 2, HEAD_DIM)
    kv_s = (DEPTH, n_s, 1, 2, HEAD_DIM)
    ssm_shape = (SSM_HEADS, HEAD_DIM, D_STATE)
    return (yp, ys.reshape(n_s, 1, D_MODEL),
            jnp.stack(kp).reshape(kv_p), jnp.stack(vp).reshape(kv_p),
            jnp.stack(cp), jnp.stack(hp).reshape((DEPTH, bsz) + ssm_shape),
            ksm.reshape(kv_s), vsm.reshape(kv_s),
            jnp.swapaxes(csm, 1, 2),
            hsm.reshape((DEPTH, n_s) + ssm_shape),
            vns.reshape(DEPTH, n_s, 1, A_WIDTH))
```

```python
import functools

import jax
import jax.numpy as jnp
from jax import lax
from jax.experimental import pallas as pl
from jax.experimental.pallas import tpu as pltpu

F32 = jnp.float32
BF16 = jnp.bfloat16

D_MODEL = 1024
DEPTH = 4
HEAD_DIM = 64
A_WIDTH = 256
B_WIDTH = 256
KV_WIDTH = 128
C_WIDTH = 512
CONV_DIM = 1024
CONV_W = 4
SSM_HEADS = 8
D_STATE = 128
D_FF = 4096
CHUNK = 128
ALPHA = (2 * DEPTH) ** 0.25
LN_EPS = 1e-5
RMS_EPS = 1e-6
ATT_SCALE = HEAD_DIM ** -0.5
LOG2E = 1.4426950408889634
Q_SCALE = ATT_SCALE * LOG2E
NEG = -1e30

OFF_AU, OFF_AV, OFF_Q, OFF_K, OFF_V, OFF_Z = 0, 256, 512, 768, 896, 1024
OFF_XBC = 1536
OFF_DT = 2560
D_IN_PAD = 2688
LANES = 128
SUBLANES = 8
VMEM_LIMIT = 56 * 1024 * 1024

NT_DIMS = (((1,), (1,)), ((), ()))
PIECE = 256


def _dot(a, b):
    return jnp.dot(a, b, preferred_element_type=F32)


def _dot_nt(a, b):
    return lax.dot_general(a, b, NT_DIMS, preferred_element_type=F32)


def _layer_norm(x, g, b):
    mu = jnp.mean(x, axis=-1, keepdims=True)
    xc = x - mu
    var = jnp.mean(xc * xc, axis=-1, keepdims=True)
    return xc * lax.rsqrt(var + LN_EPS) * g + b


def _silu(x):
    return x * jax.nn.sigmoid(x)


def _split3(x):
    hi = x.astype(BF16)
    r = x - hi.astype(F32)
    mid = r.astype(BF16)
    lo = (r - mid.astype(F32)).astype(BF16)
    return hi, mid, lo


def _dot_exact_rhs(x, m_bf16):
    hi, mid, lo = _split3(x)
    return _dot(hi, m_bf16) + _dot(mid, m_bf16) + _dot(lo, m_bf16)


def _gated_rmsnorm(y, xs, gate, dskip, gnw):
    y2 = (y + dskip * xs) * gate
    ms = jnp.mean(y2 * y2, axis=-1, keepdims=True)
    return y2 * lax.rsqrt(ms + RMS_EPS) * gnw


def _mixer_kernel(sinks_ref, x_ref, w_in_ref, lnvg_ref, lnvb_ref, ws_ref, bs_ref,
                  convw_ref, convb_ref, dtb_ref, alog_ref, dskip_ref, gnw_ref,
                  w_out_ref, ln1g_ref, ln1b_ref,
                  x1_ref, klast_ref, vlast_ref, convnew_ref, ssm_ref,
                  proj_ref, xbc_ref, act_ref, dtr_ref, mix_ref, kprev_ref, vprev_ref,
                  h_ref, wstk_ref, xb_ref, mo_ref, *, tl, layer):
    t = pl.program_id(1)
    nt = pl.num_programs(1)

    row_i = lax.broadcasted_iota(jnp.int32, (CHUNK, CHUNK), 0)
    col_i = lax.broadcasted_iota(jnp.int32, (CHUNK, CHUNK), 1)
    causal = col_i <= row_i
    lane_lo = col_i < HEAD_DIM

    @pl.when(t == 0)
    def _init():
        kprev_ref[...] = jnp.zeros_like(kprev_ref)
        vprev_ref[...] = jnp.zeros_like(vprev_ref)
        h_ref[...] = jnp.zeros_like(h_ref)
        xbc_ref[0:SUBLANES, :] = jnp.zeros((SUBLANES, CONV_DIM), F32)
        for h in range(4):
            wstk_ref[h * CHUNK:(h + 1) * CHUNK, :] = jnp.where(causal, ws_ref[h], 0.0).astype(BF16)

    half = tl // 2

    def project_pieces(hf):
        rows = slice(hf * half, (hf + 1) * half)

        def cast():
            xb_ref[rows, :] = x_ref[0, rows, :].astype(BF16)

        epilogue = {
            OFF_AU: jax.nn.gelu,
            OFF_AV: lambda r: _layer_norm(jax.nn.gelu(r), lnvg_ref[...], lnvb_ref[...]),
            OFF_Q: lambda r: r * Q_SCALE,
            OFF_Z: _silu,
            OFF_Z + PIECE: _silu,
        }

        def main(c0):
            def run():
                res = _dot(xb_ref[rows, :], w_in_ref[:, c0:c0 + PIECE])
                proj_ref[rows, c0:c0 + PIECE] = epilogue.get(c0, lambda r: r)(res)
            return run

        def xbc(c0):
            def run():
                cols = slice(c0, c0 + PIECE)
                res = _dot(xb_ref[rows, :], w_in_ref[:, OFF_XBC + c0:OFF_XBC + c0 + PIECE])
                xh = jnp.concatenate([xbc_ref[hf * half:hf * half + SUBLANES, cols], res], axis=0)
                xbc_ref[SUBLANES + hf * half:SUBLANES + (hf + 1) * half, cols] = res
                acc = xh * convw_ref[0:1, cols]
                for i in range(1, CONV_W):
                    acc = pltpu.roll(acc, 1, 0) + xh * convw_ref[i:i + 1, cols]
                act_ref[rows, cols] = _silu(acc[SUBLANES:] + convb_ref[:, cols])
            return run

        def dt():
            dtr_ref[rows, :] = _dot(xb_ref[rows, :], w_in_ref[:, OFF_DT:D_IN_PAD])

        return ([cast] + [main(c0) for c0 in range(0, OFF_XBC, PIECE)]
                + [xbc(c0) for c0 in range(0, CONV_DIM, PIECE)] + [dt])

    def out_pieces(hf):
        rows = slice(hf * half, (hf + 1) * half)

        def part(c0):
            def run():
                mo_ref[rows, c0:c0 + PIECE] = _dot(mix_ref[rows, :], w_out_ref[:, c0:c0 + PIECE])
            return run
        return [part(c0) for c0 in range(0, D_MODEL, PIECE)]

    def norm_pieces(hf):
        def piece(r0):
            def run():
                rows = slice(r0, r0 + CHUNK)
                x1_ref[0, rows, :] = _layer_norm(ALPHA * x_ref[0, rows, :] + mo_ref[rows, :],
                                                 ln1g_ref[...], ln1b_ref[...])
            return run
        return [piece(r0) for r0 in range(hf * half, (hf + 1) * half, CHUNK)]

    triu_b = jnp.where(row_i <= col_i, 1.0, 0.0).astype(BF16)
    a_col = -jnp.exp(alog_ref[...])
    zpad = jnp.zeros((CHUNK - SSM_HEADS, CHUNK), F32)
    lane256 = lax.broadcasted_iota(jnp.int32, (CHUNK, A_WIDTH), 1)
    r256 = lax.broadcasted_iota(jnp.int32, (2 * CHUNK, 2 * CHUNK), 0)
    c256 = lax.broadcasted_iota(jnp.int32, (2 * CHUNK, 2 * CHUNK), 1)
    t256 = jnp.bitwise_and(r256, CHUNK - 1)
    win_mask = (c256 > t256) & (c256 <= t256 + CHUNK)
    rcol = lax.broadcasted_iota(jnp.int32, (2 * CHUNK, 1), 0)
    sink_a = jnp.where(rcol < CHUNK, sinks_ref[layer, 0], sinks_ref[layer, 3]) * LOG2E
    sink_b = jnp.where(rcol < CHUNK, sinks_ref[layer, 1], sinks_ref[layer, 2]) * LOG2E

    def chunk_a(c):
        rows = slice(c * CHUNK, (c + 1) * CHUNK)
        u = proj_ref[rows, OFF_AU:OFF_AU + A_WIDTH]
        vn = proj_ref[rows, OFF_AV:OFF_AV + A_WIDTH]
        pa = _dot(wstk_ref[...], vn.astype(BF16))
        mix_a = jnp.where(
            lane256 < HEAD_DIM, pa[0:CHUNK],
            jnp.where(lane256 < 2 * HEAD_DIM, pa[CHUNK:2 * CHUNK],
                      jnp.where(lane256 < 3 * HEAD_DIM, pa[2 * CHUNK:3 * CHUNK], pa[3 * CHUNK:])))
        mix_ref[rows, 0:A_WIDTH] = (u * (mix_a + bs_ref[...])).astype(BF16)

    def chunk_b(c):
        rows = slice(c * CHUNK, (c + 1) * CHUNK)
        q01 = proj_ref[rows, OFF_Q:OFF_Q + LANES]
        q23 = proj_ref[rows, OFF_Q + LANES:OFF_Q + 2 * LANES]
        kc = proj_ref[rows, OFF_K:OFF_K + KV_WIDTH]
        vc = proj_ref[rows, OFF_V:OFF_V + KV_WIDTH]
        if c == 0:
            kp, vp = kprev_ref[...], vprev_ref[...]
        else:
            prev = slice((c - 1) * CHUNK, c * CHUNK)
            kp = proj_ref[prev, OFF_K:OFF_K + KV_WIDTH]
            vp = proj_ref[prev, OFF_V:OFF_V + KV_WIDTH]
        k2 = jnp.concatenate([kp, kc], axis=0)
        v2 = jnp.concatenate([vp, vc], axis=0)
        k2b = k2.astype(BF16)
        v2b = v2.astype(BF16)
        k2rb = pltpu.roll(k2, HEAD_DIM, 1).astype(BF16)
        v2rb = pltpu.roll(v2, HEAD_DIM, 1).astype(BF16)
        q_a = jnp.concatenate([jnp.where(lane_lo, q01, 0.0), jnp.where(lane_lo, 0.0, q23)], axis=0)
        q_b = jnp.concatenate([jnp.where(lane_lo, 0.0, q01), jnp.where(lane_lo, q23, 0.0)], axis=0)
        vis = win_mask & (c256 >= jnp.where(t == 0, CHUNK, 0)) if c == 0 else win_mask

        def attend(qm, kk, vv, sink2):
            s = jnp.where(vis, _dot_nt(qm.astype(BF16), kk), NEG)
            m = jnp.maximum(jnp.max(s, axis=-1, keepdims=True), sink2)
            p = jnp.exp2(s - m)
            den = jnp.sum(p, axis=-1, keepdims=True) + jnp.exp2(sink2 - m)
            return _dot(p.astype(BF16), vv) * (1.0 / den)

        o_a = attend(q_a, k2b, v2b, sink_a)
        o_b = attend(q_b, k2rb, v2rb, sink_b)
        mix_ref[rows, A_WIDTH:A_WIDTH + LANES] = jnp.where(
            lane_lo, o_a[0:CHUNK], o_b[0:CHUNK]).astype(BF16)
        mix_ref[rows, A_WIDTH + LANES:A_WIDTH + 2 * LANES] = jnp.where(
            lane_lo, o_b[CHUNK:], o_a[CHUNK:]).astype(BF16)

    def chunk_c(c):
        rows = slice(c * CHUNK, (c + 1) * CHUNK)
        xs = act_ref[rows, 0:C_WIDTH]
        bm = act_ref[rows, C_WIDTH:C_WIDTH + 2 * D_STATE]
        cm = act_ref[rows, C_WIDTH + 2 * D_STATE:CONV_DIM]
        gate = proj_ref[rows, OFF_Z:OFF_Z + C_WIDTH]
        dt_row = jax.nn.softplus(dtr_ref[rows, :].T[0:SSM_HEADS] + dtb_ref[...])
        cum_row = _dot_exact_rhs(dt_row * a_col, triu_b) * LOG2E
        cum = jnp.concatenate([cum_row, zpad], axis=0).T
        toend_row = jnp.exp2(cum_row[:, CHUNK - 1:CHUNK] - cum_row) * dt_row
        ecl = jnp.exp2(cum_row[:, CHUNK - 1:CHUNK])
        xt = xs.T
        yts = []
        for g in range(2):
            bg = bm[:, g * D_STATE:(g + 1) * D_STATE]
            cg = cm[:, g * D_STATE:(g + 1) * D_STATE]
            cb = _dot_nt(cg.astype(BF16), bg.astype(BF16))
            xw = []
            for hh in range(4):
                h = g * 4 + hh
                hs = slice(h * HEAD_DIM, (h + 1) * HEAD_DIM)
                cc = jnp.broadcast_to(cum[:, h:h + 1], (CHUNK, CHUNK))
                dec = jnp.exp2(jnp.where(causal, cc - cum_row[h:h + 1, :], NEG))
                m_h = cb * dec * dt_row[h:h + 1, :]
                ce_h = cg * jnp.exp2(cc)
                l_h = jnp.concatenate([m_h, ce_h], axis=1).astype(BF16)
                r_h = jnp.concatenate([xt[hs], h_ref[hs, :]], axis=1).astype(BF16)
                yts.append(_dot_nt(r_h, l_h))
                xw.append(xt[hs] * toend_row[h:h + 1, :])
            gs = slice(g * 4 * HEAD_DIM, (g + 1) * 4 * HEAD_DIM)
            upd = _dot(jnp.concatenate(xw, axis=0).astype(BF16), bg.astype(BF16))
            keep = jnp.concatenate(
                [jnp.broadcast_to(ecl[g * 4 + hh:g * 4 + hh + 1, :], (HEAD_DIM, D_STATE))
                 for hh in range(4)], axis=0)
            h_ref[gs, :] = h_ref[gs, :] * keep + upd
        y = jnp.concatenate(yts, axis=0).T
        out_c = _gated_rmsnorm(y, xs, gate, dskip_ref[...], gnw_ref[...])
        mix_ref[rows, A_WIDTH + B_WIDTH:D_MODEL] = out_c.astype(BF16)

    cph = half // CHUNK

    def chunks(hf):
        for c in range(hf * cph, (hf + 1) * cph):
            chunk_a(c)
            chunk_b(c)
            chunk_c(c)

    for hf in range(2):
        for piece in project_pieces(hf):
            piece()
    for hf in range(2):
        chunks(hf)
        for piece in out_pieces(hf):
            piece()
    for hf in range(2):
        for piece in norm_pieces(hf):
            piece()

    last = slice(tl - CHUNK, tl)
    kprev_ref[...] = proj_ref[last, OFF_K:OFF_K + KV_WIDTH]
    vprev_ref[...] = proj_ref[last, OFF_V:OFF_V + KV_WIDTH]
    tail = xbc_ref[SUBLANES + tl - (CONV_W - 1):SUBLANES + tl, :]
    xbc_ref[SUBLANES - (CONV_W - 1):SUBLANES, :] = tail

    @pl.when(t == nt - 1)
    def _final():
        klast_ref[0] = kprev_ref[...]
        vlast_ref[0] = vprev_ref[...]
        convnew_ref[0] = tail
        ssm_ref[0] = h_ref[...]


def _full(shape):
    n = len(shape)
    return pl.BlockSpec(shape, lambda *_: (0,) * n)


def _layer(shape, l, single_buffer=False):
    n = len(shape)
    mode = pl.Buffered(1) if single_buffer else None
    return pl.BlockSpec((None,) + tuple(shape), lambda *_: (l,) + (0,) * n, pipeline_mode=mode)


def _mixer_call(l, x, sinks, w_in, lnvg, lnvb, ws, bs_exp, convw, convb, dtb, alog, dskip, gnw,
                w_out, ln1g, ln1b, *, tl):
    bsz, seq, _ = x.shape
    nt = seq // tl
    kern = functools.partial(_mixer_kernel, tl=tl, layer=l)
    in_specs = [
        pl.BlockSpec(memory_space=pltpu.SMEM),
        pl.BlockSpec((1, tl, D_MODEL), lambda b, t: (b, t, 0)),
        _layer((D_MODEL, D_IN_PAD), l, single_buffer=True),
        _layer((1, A_WIDTH), l), _layer((1, A_WIDTH), l),
        _layer((4, CHUNK, CHUNK), l), _layer((CHUNK, A_WIDTH), l),
        _layer((CONV_W, CONV_DIM), l), _layer((1, CONV_DIM), l),
        _layer((SSM_HEADS, 1), l), _layer((SSM_HEADS, 1), l),
        _layer((1, C_WIDTH), l), _layer((1, C_WIDTH), l),
        _layer((D_MODEL, D_MODEL), l, single_buffer=True),
        _layer((1, D_MODEL), l), _layer((1, D_MODEL), l),
    ]
    out_shape = (
        jax.ShapeDtypeStruct((bsz, seq, D_MODEL), F32),
        jax.ShapeDtypeStruct((bsz, CHUNK, KV_WIDTH), F32),
        jax.ShapeDtypeStruct((bsz, CHUNK, KV_WIDTH), F32),
        jax.ShapeDtypeStruct((bsz, CONV_W - 1, CONV_DIM), F32),
        jax.ShapeDtypeStruct((bsz, C_WIDTH, D_STATE), F32),
    )
    out_specs = (
        pl.BlockSpec((1, tl, D_MODEL), lambda b, t: (b, t, 0)),
        pl.BlockSpec((1, CHUNK, KV_WIDTH), lambda b, t: (b, 0, 0)),
        pl.BlockSpec((1, CHUNK, KV_WIDTH), lambda b, t: (b, 0, 0)),
        pl.BlockSpec((1, CONV_W - 1, CONV_DIM), lambda b, t: (b, 0, 0)),
        pl.BlockSpec((1, C_WIDTH, D_STATE), lambda b, t: (b, 0, 0)),
    )
    scratch = [
        pltpu.VMEM((tl, OFF_XBC), F32),
        pltpu.VMEM((SUBLANES + tl, CONV_DIM), F32),
        pltpu.VMEM((tl, CONV_DIM), F32),
        pltpu.VMEM((tl, LANES), F32),
        pltpu.VMEM((tl, D_MODEL), BF16),
        pltpu.VMEM((CHUNK, KV_WIDTH), F32),
        pltpu.VMEM((CHUNK, KV_WIDTH), F32),
        pltpu.VMEM((C_WIDTH, D_STATE), F32),
        pltpu.VMEM((4 * CHUNK, CHUNK), BF16),
        pltpu.VMEM((tl, D_MODEL), BF16),
        pltpu.VMEM((tl, D_MODEL), F32),
    ]
    return pl.pallas_call(
        kern, out_shape=out_shape, grid=(bsz, nt), in_specs=in_specs, out_specs=out_specs,
        scratch_shapes=scratch, name="prompt_mixer",
        compiler_params=pltpu.CompilerParams(
            dimension_semantics=("arbitrary", "arbitrary"), vmem_limit_bytes=VMEM_LIMIT),
    )(sinks, x, w_in, lnvg, lnvb, ws, bs_exp, convw, convb, dtb, alog, dskip, gnw, w_out, ln1g, ln1b)


FF_CHUNK = 1024


def _ffn(x, w1_ref, w2_ref, g_ref, b_ref):
    xb = x.astype(BF16)
    acc = jnp.zeros(x.shape, F32)
    for c in range(D_FF // FF_CHUNK):
        sl = slice(c * FF_CHUNK, (c + 1) * FF_CHUNK)
        h = jnp.maximum(_dot(xb, w1_ref[:, sl]), 0.0)
        acc = acc + _dot((h * h).astype(BF16), w2_ref[sl, :])
    return _layer_norm(ALPHA * x + acc, g_ref[...], b_ref[...])


def _ffn_kernel(x_ref, w1_ref, w2_ref, g_ref, b_ref, o_ref):
    o_ref[...] = _ffn(x_ref[...], w1_ref, w2_ref, g_ref, b_ref)


def _ffn_call(l, x2d, w1, w2, g, b, *, tm):
    rows = x2d.shape[0]
    return pl.pallas_call(
        _ffn_kernel, out_shape=jax.ShapeDtypeStruct((rows, D_MODEL), F32),
        grid=(rows // tm,),
        in_specs=[pl.BlockSpec((tm, D_MODEL), lambda i: (i, 0)),
                  _layer((D_MODEL, D_FF), l, single_buffer=True),
                  _layer((D_FF, D_MODEL), l, single_buffer=True),
                  _layer((1, D_MODEL), l), _layer((1, D_MODEL), l)],
        out_specs=pl.BlockSpec((tm, D_MODEL), lambda i: (i, 0)),
        name="ffn",
        compiler_params=pltpu.CompilerParams(
            dimension_semantics=("arbitrary",), vmem_limit_bytes=VMEM_LIMIT),
    )(x2d, w1, w2, g, b)


def _s1_kernel(x_ref, w_in_ref, lnvg_ref, lnvb_ref, ws0_ref, bs0_ref, convw_ref, convb_ref,
               cst_ref, dtb_ref, alog_ref, e_ref,
               vn_ref, outa_ref, q_ref, k_ref, v_ref, z_ref, cnew_ref, xs_ref, bm_ref,
               xT_ref, cT_ref, dec_ref):
    xb = x_ref[...].astype(BF16)
    proj = _dot(xb, w_in_ref[...])
    u = jax.nn.gelu(proj[:, OFF_AU:OFF_AU + A_WIDTH])
    vn = _layer_norm(jax.nn.gelu(proj[:, OFF_AV:OFF_AV + A_WIDTH]), lnvg_ref[...], lnvb_ref[...])
    vn_ref[...] = vn
    outa_ref[...] = u * (ws0_ref[...] * vn + bs0_ref[...])
    q_ref[...] = proj[:, OFF_Q:OFF_Q + B_WIDTH]
    k_ref[...] = proj[:, OFF_K:OFF_K + KV_WIDTH]
    v_ref[...] = proj[:, OFF_V:OFF_V + KV_WIDTH]
    z_ref[...] = proj[:, OFF_Z:OFF_Z + C_WIDTH]
    xbc = proj[:, OFF_XBC:OFF_DT]
    acc = convb_ref[...] + xbc * convw_ref[CONV_W - 1:CONV_W, :]
    for i in range(CONV_W - 1):
        acc = acc + cst_ref[i] * convw_ref[i:i + 1, :]
    cnew_ref[0] = cst_ref[1]
    cnew_ref[1] = cst_ref[2]
    cnew_ref[2] = xbc
    act = _silu(acc)
    xs = act[:, 0:C_WIDTH]
    xs_ref[...] = xs
    bm_ref[...] = act[:, C_WIDTH:C_WIDTH + 2 * D_STATE]
    cT_ref[...] = act[:, C_WIDTH + 2 * D_STATE:CONV_DIM].T
    dt = jax.nn.softplus(proj[:, OFF_DT:D_IN_PAD] + dtb_ref[...])
    dec_ref[...] = jnp.exp(dt * (-jnp.exp(alog_ref[...])))
    dt_exp = _dot_exact_rhs(dt, e_ref[...])
    xT_ref[...] = (xs * dt_exp).T


S2_TB = SUBLANES


def _s2_kernel(sinks_ref, q_ref, kn_ref, vn_ref, kst_ref, vst_ref, h_ref, xT_ref, bm_ref,
               cT_ref, dec_ref, ob_ref, hnew_ref, yT_ref, *, layer, step):
    i = step
    tb = S2_TB
    lane = lax.broadcasted_iota(jnp.int32, (tb, LANES), 1)
    lo = lane < HEAD_DIM

    q01 = q_ref[:, 0:LANES]
    q23 = q_ref[:, LANES:2 * LANES]
    qh = jnp.concatenate([
        jnp.where(lo, q01, 0.0),
        jnp.where(lo, pltpu.roll(q01, HEAD_DIM, 1), 0.0),
        jnp.where(lo, 0.0, pltpu.roll(q23, HEAD_DIM, 1)),
        jnp.where(lo, 0.0, q23)], axis=0)
    qhb = qh.astype(BF16)
    rowi = lax.broadcasted_iota(jnp.int32, (4 * tb, LANES), 0)
    coli = lax.broadcasted_iota(jnp.int32, (4 * tb, LANES), 1)
    tok = jnp.bitwise_and(rowi, tb - 1)
    s = jnp.zeros((4 * tb, LANES), F32)
    for j in range(tb):
        sj = _dot_nt(qhb, kst_ref[j].astype(BF16))
        s = jnp.where(tok == j, sj, s)
    s = jnp.where(coli >= 1, s * ATT_SCALE, NEG)
    kn4 = jnp.concatenate([kn_ref[...]] * 4, axis=0)
    vn4 = jnp.concatenate([vn_ref[...]] * 4, axis=0)
    s_new = jnp.sum(qh * kn4, axis=-1, keepdims=True) * ATT_SCALE
    head = jnp.right_shift(rowi[:, 0:1], 3)
    sink = jnp.where(head == 0, sinks_ref[layer, 0],
                     jnp.where(head == 1, sinks_ref[layer, 1],
                               jnp.where(head == 2, sinks_ref[layer, 2], sinks_ref[layer, 3])))
    m = jnp.maximum(jnp.maximum(jnp.max(s, axis=-1, keepdims=True), s_new), sink)
    p = jnp.exp(s - m)
    p_new = jnp.exp(s_new - m)
    inv = 1.0 / (jnp.sum(p, axis=-1, keepdims=True) + p_new + jnp.exp(sink - m))
    p = p * inv
    o = (p_new * inv) * vn4
    for j in range(tb):
        pj = jnp.where(tok == j, p, 0.0).astype(BF16)
        o = o + _dot(pj, vst_ref[j].astype(BF16))
    ob_ref[:, 0:LANES] = jnp.where(lo, o[0:tb], pltpu.roll(o[tb:2 * tb], HEAD_DIM, 1))
    ob_ref[:, LANES:2 * LANES] = jnp.where(lo, pltpu.roll(o[2 * tb:3 * tb], HEAD_DIM, 1), o[3 * tb:])

    @pl.when(i == 0)
    def _():
        yT_ref[...] = jnp.zeros_like(yT_ref)

    r8 = lax.broadcasted_iota(jnp.int32, (tb, D_STATE), 0)
    c128 = lax.broadcasted_iota(jnp.int32, (LANES, LANES), 1)
    hp_g = C_WIDTH // 2
    b0 = pl.multiple_of(i * tb, tb)
    zrows = jnp.zeros((LANES - tb, tb * D_STATE), BF16)
    to_front = jnp.bitwise_and(LANES - b0, LANES - 1)
    for g in range(2):
        gs = slice(g * hp_g, (g + 1) * hp_g)
        ns = slice(g * D_STATE, (g + 1) * D_STATE)
        bblk = bm_ref[pl.ds(b0, tb), ns]
        bdiag = jnp.concatenate([jnp.where(r8 == j, bblk, 0.0) for j in range(tb)], axis=1)
        rhs = jnp.concatenate([bdiag.astype(BF16), zrows], axis=0)
        xg = pltpu.roll(xT_ref[gs, :], to_front, 1).astype(BF16)
        upd = _dot(xg, rhs)
        hns = []
        cms = []
        for j in range(tb):
            drow = dec_ref[pl.ds(b0 + j, 1), :]
            parts = []
            for hh in range(4):
                h = g * 4 + hh
                rs = slice(h * HEAD_DIM, (h + 1) * HEAD_DIM)
                hn = (h_ref[j, rs, :] * jnp.broadcast_to(drow[:, h:h + 1], (HEAD_DIM, D_STATE))
                      + upd[hh * HEAD_DIM:(hh + 1) * HEAD_DIM, j * D_STATE:(j + 1) * D_STATE])
                hnew_ref[j, rs, :] = hn
                parts.append(hn)
            hns.append(jnp.concatenate(parts, axis=0).astype(BF16))
            cms.append(jnp.where(c128 == b0 + j, cT_ref[ns, :], 0.0).astype(BF16))
        yT_ref[gs, :] += _dot(jnp.concatenate(hns, axis=1), jnp.concatenate(cms, axis=0))


def _s3_kernel(yT_ref, xs_ref, z_ref, dskip_ref, gnw_ref, outa_ref, outb_ref, x_ref, w_out_ref,
               g_ref, b_ref, x1_ref):
    out_c = _gated_rmsnorm(yT_ref[...].T, xs_ref[...], _silu(z_ref[...]), dskip_ref[...], gnw_ref[...])
    mix = jnp.concatenate([outa_ref[...], outb_ref[...], out_c], axis=1).astype(BF16)
    x1_ref[...] = _layer_norm(ALPHA * x_ref[...] + _dot(mix, w_out_ref[...]), g_ref[...], b_ref[...])


def _sample_kernel(sinks_ref, x_ref, w_in_ref, lnvg_ref, lnvb_ref, ws0_ref, bs0_ref, convw_ref,
                   convb_ref, cst_ref, dtb_ref, alog_ref, e_ref, kst_ref, vst_ref, h_ref,
                   dskip_ref, gnw_ref, w_out_ref, ln1g_ref, ln1b_ref, w1_ref, w2_ref, ln2g_ref, ln2b_ref,
                   ys_ref, vns_ref, ksm_ref, vsm_ref, csm_ref, hnew_ref,
                   res_scr, outa_scr, q_scr, z_scr, xs_scr, bm_scr, xT_scr, cT_scr, dec_scr,
                   outb_scr, yT_scr, x1_scr):
    l = pl.program_id(0)
    i = pl.program_id(1)
    last = pl.num_programs(1) - 1

    @pl.when(jnp.logical_and(l == 0, i == 0))
    def _():
        res_scr[...] = x_ref[...]

    @pl.when(i == 0)
    def _():
        _s1_kernel(res_scr, w_in_ref, lnvg_ref, lnvb_ref, ws0_ref, bs0_ref, convw_ref, convb_ref,
                   cst_ref, dtb_ref, alog_ref, e_ref,
                   vns_ref, outa_scr, q_scr, ksm_ref, vsm_ref, z_scr, csm_ref, xs_scr, bm_scr,
                   xT_scr, cT_scr, dec_scr)

    rows = pl.ds(pl.multiple_of(i * S2_TB, S2_TB), S2_TB)
    _s2_kernel(sinks_ref, q_scr.at[rows], ksm_ref.at[rows], vsm_ref.at[rows], kst_ref, vst_ref,
               h_ref, xT_scr, bm_scr, cT_scr, dec_scr, outb_scr.at[rows], hnew_ref, yT_scr,
               layer=l, step=i)

    @pl.when(i == last)
    def _():
        _s3_kernel(yT_scr, xs_scr, z_scr, dskip_ref, gnw_ref, outa_scr, outb_scr, res_scr,
                   w_out_ref, ln1g_ref, ln1b_ref, x1_scr)
        res_scr[...] = _ffn(x1_scr[...], w1_ref, w2_ref, ln2g_ref, ln2b_ref)
        ys_ref[...] = res_scr[...]


def _sample_call(x, sinks, w_in, lnvg, lnvb, ws0, bs0, convw, convb, cst, dtb, alog, emat,
                 kst, vst, hst, dskip, gnw, w_out, ln1g, ln1b, w1, w2, ln2g, ln2b):
    n = x.shape[0]
    tb = S2_TB

    def per_layer(shape, single_buffer=False):
        k = len(shape)
        mode = pl.Buffered(1) if single_buffer else None
        return pl.BlockSpec((None,) + tuple(shape), lambda l, i: (l,) + (0,) * k, pipeline_mode=mode)

    def per_block(shape):
        k = len(shape) - 1
        return pl.BlockSpec((None,) + tuple(shape), lambda l, i: (l, i) + (0,) * k)

    in_specs = [
        pl.BlockSpec(memory_space=pltpu.SMEM),
        _full((n, D_MODEL)),
        per_layer((D_MODEL, D_IN_PAD), True),
        per_layer((1, A_WIDTH)), per_layer((1, A_WIDTH)), per_layer((1, A_WIDTH)), per_layer((1, A_WIDTH)),
        per_layer((CONV_W, CONV_DIM)), per_layer((1, CONV_DIM)), per_layer((CONV_W - 1, n, CONV_DIM)),
        per_layer((1, LANES)), per_layer((1, LANES)), _full((LANES, C_WIDTH)),
        per_block((tb, CHUNK, KV_WIDTH)), per_block((tb, CHUNK, KV_WIDTH)),
        per_block((tb, C_WIDTH, D_STATE)),
        per_layer((1, C_WIDTH)), per_layer((1, C_WIDTH)),
        per_layer((D_MODEL, D_MODEL), True), per_layer((1, D_MODEL)), per_layer((1, D_MODEL)),
        per_layer((D_MODEL, D_FF), True), per_layer((D_FF, D_MODEL), True),
        per_layer((1, D_MODEL)), per_layer((1, D_MODEL)),
    ]
    out_shape = (
        jax.ShapeDtypeStruct((n, D_MODEL), F32),
        jax.ShapeDtypeStruct((DEPTH, n, A_WIDTH), F32),
        jax.ShapeDtypeStruct((DEPTH, n, KV_WIDTH), F32),
        jax.ShapeDtypeStruct((DEPTH, n, KV_WIDTH), F32),
        jax.ShapeDtypeStruct((DEPTH, CONV_W - 1, n, CONV_DIM), F32),
        jax.ShapeDtypeStruct((DEPTH, n, C_WIDTH, D_STATE), F32),
    )
    out_specs = (
        _full((n, D_MODEL)),
        per_layer((n, A_WIDTH)), per_layer((n, KV_WIDTH)), per_layer((n, KV_WIDTH)),
        per_layer((CONV_W - 1, n, CONV_DIM)),
        per_block((tb, C_WIDTH, D_STATE)),
    )
    scratch = [
        pltpu.VMEM((n, D_MODEL), F32),
        pltpu.VMEM((n, A_WIDTH), F32),
        pltpu.VMEM((n, B_WIDTH), F32),
        pltpu.VMEM((n, C_WIDTH), F32),
        pltpu.VMEM((n, C_WIDTH), F32),
        pltpu.VMEM((n, 2 * D_STATE), F32),
        pltpu.VMEM((C_WIDTH, n), F32),
        pltpu.VMEM((2 * D_STATE, n), F32),
        pltpu.VMEM((n, LANES), F32),
        pltpu.VMEM((n, B_WIDTH), F32),
        pltpu.VMEM((C_WIDTH, n), F32),
        pltpu.VMEM((n, D_MODEL), F32),
    ]
    return pl.pallas_call(
        _sample_kernel, out_shape=out_shape, grid=(DEPTH, n // tb), in_specs=in_specs,
        out_specs=out_specs, scratch_shapes=scratch, name="sample_step",
        compiler_params=pltpu.CompilerParams(
            dimension_semantics=("arbitrary", "arbitrary"), vmem_limit_bytes=VMEM_LIMIT),
    )(sinks, x, w_in, lnvg, lnvb, ws0, bs0, convw, convb, cst, dtb, alog, emat, kst, vst, hst,
      dskip, gnw, w_out, ln1g, ln1b, w1, w2, ln2g, ln2b)


PROMPT_TL = 1024
PROMPT_TM = 1024


def kernel(x_prompt, x_sample, state_attn_k, state_attn_v, state_conv, state_ssm, w_in, ln_v_g,
           ln_v_b, w_s, b_s, sinks, conv_w, conv_b, dt_bias, a_log, d_skip, gn_w, w_out, ln1_g,
           ln1_b, w1, w2, ln2_g, ln2_b):
    bsz, seq, _ = x_prompt.shape
    n_s = x_sample.shape[0]
    d_in = w_in.shape[-1]

    w_in_b = jnp.pad(w_in, ((0, 0), (0, 0), (0, D_IN_PAD - d_in))).astype(BF16)
    w_out_b = w_out.astype(BF16)
    w1_b = w1.astype(BF16)
    w2_b = w2.astype(BF16)
    pad_h = ((0, 0), (0, LANES - SSM_HEADS))
    dtb_p = jnp.pad(dt_bias, pad_h)[:, None, :]
    alog_p = jnp.pad(a_log, pad_h)[:, None, :]
    dskip_e = jnp.repeat(d_skip, HEAD_DIM, axis=-1)[:, None, :]
    bs_e = jnp.repeat(jnp.swapaxes(b_s, 1, 2), HEAD_DIM, axis=-1)
    ws0_e = jnp.repeat(w_s[:, :, 0, 0], HEAD_DIM, axis=-1)[:, None, :]
    bs0_e = jnp.repeat(b_s[:, :, 0], HEAD_DIM, axis=-1)[:, None, :]
    emat = (lax.broadcasted_iota(jnp.int32, (LANES, C_WIDTH), 0)
            == lax.broadcasted_iota(jnp.int32, (LANES, C_WIDTH), 1) // HEAD_DIM).astype(BF16)
    row = lambda a: a[:, None, :]
    lnvg, lnvb, convb, gnw = row(ln_v_g), row(ln_v_b), row(conv_b), row(gn_w)
    ln1g, ln1b, ln2g, ln2b = row(ln1_g), row(ln1_b), row(ln2_g), row(ln2_b)
    kst = state_attn_k.reshape(DEPTH, n_s, CHUNK, KV_WIDTH)
    vst = state_attn_v.reshape(DEPTH, n_s, CHUNK, KV_WIDTH)
    hst = state_ssm.reshape(DEPTH, n_s, C_WIDTH, D_STATE)
    cst = jnp.swapaxes(state_conv, 1, 2)

    ys, vns, ksm, vsm, csm, hsm = _sample_call(
        x_sample.reshape(n_s, D_MODEL), sinks, w_in_b, lnvg, lnvb, ws0_e, bs0_e, conv_w, convb, cst,
        dtb_p, alog_p, emat, kst, vst, hst, dskip_e, gnw, w_out_b, ln1g, ln1b, w1_b, w2_b, ln2g, ln2b)

    yp = x_prompt
    kp, vp, cp, hp = [], [], [], []
    for l in range(DEPTH):
        x1, k_l, v_l, c_l, h_l = _mixer_call(
            l, yp, sinks, w_in_b, lnvg, lnvb, w_s, bs_e, conv_w, convb,
            dt_bias[:, :, None], a_log[:, :, None], dskip_e, gnw, w_out_b, ln1g, ln1b, tl=PROMPT_TL)
        yp = _ffn_call(l, x1.reshape(bsz * seq, D_MODEL), w1_b, w2_b, ln2g, ln2b,
                       tm=PROMPT_TM).reshape(bsz, seq, D_MODEL)
        kp.append(k_l); vp.append(v_l); cp.append(c_l); hp.append(h_l)

    kv_p = (DEPTH, bsz, CHUNK, 2, HEAD_DIM)
    kv_s = (DEPTH, n_s, 1, 2, HEAD_DIM)
    ssm_shape = (SSM_HEADS, HEAD_DIM, D_STATE)
    return (yp, ys.reshape(n_s, 1, D_MODEL),
            jnp.stack(kp).reshape(kv_p), jnp.stack(vp).reshape(kv_p),
            jnp.stack(cp), jnp.stack(hp).reshape((DEPTH, bsz) + ssm_shape),
            ksm.reshape(kv_s), vsm.reshape(kv_s),
            jnp.swapaxes(csm, 1, 2),
            hsm.reshape((DEPTH, n_s) + ssm_shape),
            vns.reshape(DEPTH, n_s, 1, A_WIDTH))
```

```python
import functools

import jax
import jax.numpy as jnp
from jax import lax
from jax.experimental import pallas as pl
from jax.experimental.pallas import tpu as pltpu

F32 = jnp.float32
BF16 = jnp.bfloat16

D_MODEL = 1024
DEPTH = 4
HEAD_DIM = 64
A_WIDTH = 256
B_WIDTH = 256
KV_WIDTH = 128
C_WIDTH = 512
CONV_DIM = 1024
CONV_W = 4
SSM_HEADS = 8
D_STATE = 128
D_FF = 4096
CHUNK = 128
ALPHA = (2 * DEPTH) ** 0.25
LN_EPS = 1e-5
RMS_EPS = 1e-6
ATT_SCALE = HEAD_DIM ** -0.5
LOG2E = 1.4426950408889634
Q_SCALE = ATT_SCALE * LOG2E
NEG = -1e30

OFF_AU, OFF_AV, OFF_Q, OFF_K, OFF_V, OFF_Z = 0, 256, 512, 768, 896, 1024
OFF_XBC = 1536
OFF_DT = 2560
D_IN_PAD = 2688
LANES = 128
SUBLANES = 8
VMEM_LIMIT = 56 * 1024 * 1024

NT_DIMS = (((1,), (1,)), ((), ()))
PIECE = 256


def _dot(a, b):
    return jnp.dot(a, b, preferred_element_type=F32)


def _dot_nt(a, b):
    return lax.dot_general(a, b, NT_DIMS, preferred_element_type=F32)


def _layer_norm(x, g, b):
    mu = jnp.mean(x, axis=-1, keepdims=True)
    xc = x - mu
    var = jnp.mean(xc * xc, axis=-1, keepdims=True)
    return xc * lax.rsqrt(var + LN_EPS) * g + b


def _silu(x):
    return x * jax.nn.sigmoid(x)


def _split3(x):
    hi = x.astype(BF16)
    r = x - hi.astype(F32)
    mid = r.astype(BF16)
    lo = (r - mid.astype(F32)).astype(BF16)
    return hi, mid, lo


def _dot_exact_rhs(x, m_bf16):
    hi, mid, lo = _split3(x)
    return _dot(hi, m_bf16) + _dot(mid, m_bf16) + _dot(lo, m_bf16)


def _gated_rmsnorm(y, xs, gate, dskip, gnw):
    y2 = (y + dskip * xs) * gate
    ms = jnp.mean(y2 * y2, axis=-1, keepdims=True)
    return y2 * lax.rsqrt(ms + RMS_EPS) * gnw


def _mixer_kernel(sinks_ref, x_ref, w_in_ref, lnvg_ref, lnvb_ref, ws_ref, bs_ref,
                  convw_ref, convb_ref, dtb_ref, alog_ref, dskip_ref, gnw_ref,
                  w_out_ref, ln1g_ref, ln1b_ref,
                  x1_ref, klast_ref, vlast_ref, convnew_ref, ssm_ref,
                  proj_ref, xbc_ref, act_ref, dtr_ref, mix_ref, kprev_ref, vprev_ref,
                  h_ref, wstk_ref, xb_ref, mo_ref, *, tl, layer):
    t = pl.program_id(1)
    nt = pl.num_programs(1)

    row_i = lax.broadcasted_iota(jnp.int32, (CHUNK, CHUNK), 0)
    col_i = lax.broadcasted_iota(jnp.int32, (CHUNK, CHUNK), 1)
    causal = col_i <= row_i
    lane_lo = col_i < HEAD_DIM

    @pl.when(t == 0)
    def _init():
        kprev_ref[...] = jnp.zeros_like(kprev_ref)
        vprev_ref[...] = jnp.zeros_like(vprev_ref)
        h_ref[...] = jnp.zeros_like(h_ref)
        xbc_ref[0:SUBLANES, :] = jnp.zeros((SUBLANES, CONV_DIM), F32)
        for h in range(4):
            wstk_ref[h * CHUNK:(h + 1) * CHUNK, :] = jnp.where(causal, ws_ref[h], 0.0).astype(BF16)

    half = tl // 2

    def project_pieces(hf):
        rows = slice(hf * half, (hf + 1) * half)

        def cast():
            xb_ref[rows, :] = x_ref[0, rows, :].astype(BF16)

        epilogue = {
            OFF_AU: jax.nn.gelu,
            OFF_AV: lambda r: _layer_norm(jax.nn.gelu(r), lnvg_ref[...], lnvb_ref[...]),
            OFF_Q: lambda r: r * Q_SCALE,
            OFF_Z: _silu,
            OFF_Z + PIECE: _silu,
        }

        def main(c0):
            def run():
                res = _dot(xb_ref[rows, :], w_in_ref[:, c0:c0 + PIECE])
                proj_ref[rows, c0:c0 + PIECE] = epilogue.get(c0, lambda r: r)(res)
            return run

        def xbc(c0):
            def run():
                cols = slice(c0, c0 + PIECE)
                res = _dot(xb_ref[rows, :], w_in_ref[:, OFF_XBC + c0:OFF_XBC + c0 + PIECE])
                xh = jnp.concatenate([xbc_ref[hf * half:hf * half + SUBLANES, cols], res], axis=0)
                xbc_ref[SUBLANES + hf * half:SUBLANES + (hf + 1) * half, cols] = res
                acc = xh * convw_ref[0:1, cols]
                for i in range(1, CONV_W):
                    acc = pltpu.roll(acc, 1, 0) + xh * convw_ref[i:i + 1, cols]
                act_ref[rows, cols] = _silu(acc[SUBLANES:] + convb_ref[:, cols])
            return run

        def dt():
            dtr_ref[rows, :] = _dot(xb_ref[rows, :], w_in_ref[:, OFF_DT:D_IN_PAD])

        return ([cast] + [main(c0) for c0 in range(0, OFF_XBC, PIECE)]
                + [xbc(c0) for c0 in range(0, CONV_DIM, PIECE)] + [dt])

    def out_pieces(hf):
        rows = slice(hf * half, (hf + 1) * half)

        def part(c0):
            def run():
                mo_ref[rows, c0:c0 + PIECE] = _dot(mix_ref[rows, :], w_out_ref[:, c0:c0 + PIECE])
            return run
        return [part(c0) for c0 in range(0, D_MODEL, PIECE)]

    def norm_pieces(hf):
        def piece(r0):
            def run():
                rows = slice(r0, r0 + CHUNK)
                x1_ref[0, rows, :] = _layer_norm(ALPHA * x_ref[0, rows, :] + mo_ref[rows, :],
                                                 ln1g_ref[...], ln1b_ref[...])
            return run
        return [piece(r0) for r0 in range(hf * half, (hf + 1) * half, CHUNK)]

    triu_b = jnp.where(row_i <= col_i, 1.0, 0.0).astype(BF16)
    a_col = -jnp.exp(alog_ref[...])
    zpad = jnp.zeros((CHUNK - SSM_HEADS, CHUNK), F32)
    lane256 = lax.broadcasted_iota(jnp.int32, (CHUNK, A_WIDTH), 1)
    r256 = lax.broadcasted_iota(jnp.int32, (2 * CHUNK, 2 * CHUNK), 0)
    c256 = lax.broadcasted_iota(jnp.int32, (2 * CHUNK, 2 * CHUNK), 1)
    t256 = jnp.bitwise_and(r256, CHUNK - 1)
    win_mask = (c256 > t256) & (c256 <= t256 + CHUNK)
    rcol = lax.broadcasted_iota(jnp.int32, (2 * CHUNK, 1), 0)
    sink_a = jnp.where(rcol < CHUNK, sinks_ref[layer, 0], sinks_ref[layer, 3]) * LOG2E
    sink_b = jnp.where(rcol < CHUNK, sinks_ref[layer, 1], sinks_ref[layer, 2]) * LOG2E

    def chunk_a(c):
        rows = slice(c * CHUNK, (c + 1) * CHUNK)
        u = proj_ref[rows, OFF_AU:OFF_AU + A_WIDTH]
        vn = proj_ref[rows, OFF_AV:OFF_AV + A_WIDTH]
        pa = _dot(wstk_ref[...], vn.astype(BF16))
        mix_a = jnp.where(
            lane256 < HEAD_DIM, pa[0:CHUNK],
            jnp.where(lane256 < 2 * HEAD_DIM, pa[CHUNK:2 * CHUNK],
                      jnp.where(lane256 < 3 * HEAD_DIM, pa[2 * CHUNK:3 * CHUNK], pa[3 * CHUNK:])))
        mix_ref[rows, 0:A_WIDTH] = (u * (mix_a + bs_ref[...])).astype(BF16)

    def chunk_b(c):
        rows = slice(c * CHUNK, (c + 1) * CHUNK)
        q01 = proj_ref[rows, OFF_Q:OFF_Q + LANES]
        q23 = proj_ref[rows, OFF_Q + LANES:OFF_Q + 2 * LANES]
        kc = proj_ref[rows, OFF_K:OFF_K + KV_WIDTH]
        vc = proj_ref[rows, OFF_V:OFF_V + KV_WIDTH]
        if c == 0:
            kp, vp = kprev_ref[...], vprev_ref[...]
        else:
            prev = slice((c - 1) * CHUNK, c * CHUNK)
            kp = proj_ref[prev, OFF_K:OFF_K + KV_WIDTH]
            vp = proj_ref[prev, OFF_V:OFF_V + KV_WIDTH]
        k2 = jnp.concatenate([kp, kc], axis=0)
        v2 = jnp.concatenate([vp, vc], axis=0)
        k2b = k2.astype(BF16)
        v2b = v2.astype(BF16)
        k2rb = pltpu.roll(k2, HEAD_DIM, 1).astype(BF16)
        v2rb = pltpu.roll(v2, HEAD_DIM, 1).astype(BF16)
        q_a = jnp.concatenate([jnp.where(lane_lo, q01, 0.0), jnp.where(lane_lo, 0.0, q23)], axis=0)
        q_b = jnp.concatenate([jnp.where(lane_lo, 0.0, q01), jnp.where(lane_lo, q23, 0.0)], axis=0)
        vis = win_mask & (c256 >= jnp.where(t == 0, CHUNK, 0)) if c == 0 else win_mask

        def attend(qm, kk, vv, sink2):
            s = jnp.where(vis, _dot_nt(qm.astype(BF16), kk), NEG)
            m = jnp.maximum(jnp.max(s, axis=-1, keepdims=True), sink2)
            p = jnp.exp2(s - m)
            den = jnp.sum(p, axis=-1, keepdims=True) + jnp.exp2(sink2 - m)
            return _dot(p.astype(BF16), vv) * (1.0 / den)

        o_a = attend(q_a, k2b, v2b, sink_a)
        o_b = attend(q_b, k2rb, v2rb, sink_b)
        mix_ref[rows, A_WIDTH:A_WIDTH + LANES] = jnp.where(
            lane_lo, o_a[0:CHUNK], o_b[0:CHUNK]).astype(BF16)
        mix_ref[rows, A_WIDTH + LANES:A_WIDTH + 2 * LANES] = jnp.where(
            lane_lo, o_b[CHUNK:], o_a[CHUNK:]).astype(BF16)

    def chunk_c(c):
        rows = slice(c * CHUNK, (c + 1) * CHUNK)
        xs = act_ref[rows, 0:C_WIDTH]
        bm = act_ref[rows, C_WIDTH:C_WIDTH + 2 * D_STATE]
        cm = act_ref[rows, C_WIDTH + 2 * D_STATE:CONV_DIM]
        gate = proj_ref[rows, OFF_Z:OFF_Z + C_WIDTH]
        dt_row = jax.nn.softplus(dtr_ref[rows, :].T[0:SSM_HEADS] + dtb_ref[...])
        cum_row = _dot_exact_rhs(dt_row * a_col, triu_b) * LOG2E
        cum = jnp.concatenate([cum_row, zpad], axis=0).T
        toend_row = jnp.exp2(cum_row[:, CHUNK - 1:CHUNK] - cum_row) * dt_row
        ecl = jnp.exp2(cum_row[:, CHUNK - 1:CHUNK])
        xt = xs.T
        yts = []
        for g in range(2):
            bg = bm[:, g * D_STATE:(g + 1) * D_STATE]
            cg = cm[:, g * D_STATE:(g + 1) * D_STATE]
            cb = _dot_nt(cg.astype(BF16), bg.astype(BF16))
            xw = []
            for hh in range(4):
                h = g * 4 + hh
                hs = slice(h * HEAD_DIM, (h + 1) * HEAD_DIM)
                cc = jnp.broadcast_to(cum[:, h:h + 1], (CHUNK, CHUNK))
                dec = jnp.exp2(jnp.where(causal, cc - cum_row[h:h + 1, :], NEG))
                m_h = cb * dec * dt_row[h:h + 1, :]
                ce_h = cg * jnp.exp2(cc)
                l_h = jnp.concatenate([m_h, ce_h], axis=1).astype(BF16)
                r_h = jnp.concatenate([xt[hs], h_ref[hs, :]], axis=1).astype(BF16)
                yts.append(_dot_nt(r_h, l_h))
                xw.append(xt[hs] * toend_row[h:h + 1, :])
            gs = slice(g * 4 * HEAD_DIM, (g + 1) * 4 * HEAD_DIM)
            upd = _dot(jnp.concatenate(xw, axis=0).astype(BF16), bg.astype(BF16))
            keep = jnp.concatenate(
                [jnp.broadcast_to(ecl[g * 4 + hh:g * 4 + hh + 1, :], (HEAD_DIM, D_STATE))
                 for hh in range(4)], axis=0)
            h_ref[gs, :] = h_ref[gs, :] * keep + upd
        y = jnp.concatenate(yts, axis=0).T
        out_c = _gated_rmsnorm(y, xs, gate, dskip_ref[...], gnw_ref[...])
        mix_ref[rows, A_WIDTH + B_WIDTH:D_MODEL] = out_c.astype(BF16)

    cph = half // CHUNK

    def chunks(hf):
        for c in range(hf * cph, (hf + 1) * cph):
            chunk_a(c)
            chunk_b(c)
            chunk_c(c)

    for hf in range(2):
        for piece in project_pieces(hf):
            piece()
    for hf in range(2):
        chunks(hf)
        for piece in out_pieces(hf):
            piece()
    for hf in range(2):
        for piece in norm_pieces(hf):
            piece()

    last = slice(tl - CHUNK, tl)
    kprev_ref[...] = proj_ref[last, OFF_K:OFF_K + KV_WIDTH]
    vprev_ref[...] = proj_ref[last, OFF_V:OFF_V + KV_WIDTH]
    tail = xbc_ref[SUBLANES + tl - (CONV_W - 1):SUBLANES + tl, :]
    xbc_ref[SUBLANES - (CONV_W - 1):SUBLANES, :] = tail

    @pl.when(t == nt - 1)
    def _final():
        klast_ref[0] = kprev_ref[...]
        vlast_ref[0] = vprev_ref[...]
        convnew_ref[0] = tail
        ssm_ref[0] = h_ref[...]


def _full(shape):
    n = len(shape)
    return pl.BlockSpec(shape, lambda *_: (0,) * n)


def _layer(shape, l, single_buffer=False):
    n = len(shape)
    mode = pl.Buffered(1) if single_buffer else None
    return pl.BlockSpec((None,) + tuple(shape), lambda *_: (l,) + (0,) * n, pipeline_mode=mode)


def _mixer_call(l, x, sinks, w_in, lnvg, lnvb, ws, bs_exp, convw, convb, dtb, alog, dskip, gnw,
                w_out, ln1g, ln1b, *, tl):
    bsz, seq, _ = x.shape
    nt = seq // tl
    kern = functools.partial(_mixer_kernel, tl=tl, layer=l)
    in_specs = [
        pl.BlockSpec(memory_space=pltpu.SMEM),
        pl.BlockSpec((1, tl, D_MODEL), lambda b, t: (b, t, 0)),
        _layer((D_MODEL, D_IN_PAD), l, single_buffer=True),
        _layer((1, A_WIDTH), l), _layer((1, A_WIDTH), l),
        _layer((4, CHUNK, CHUNK), l), _layer((CHUNK, A_WIDTH), l),
        _layer((CONV_W, CONV_DIM), l), _layer((1, CONV_DIM), l),
        _layer((SSM_HEADS, 1), l), _layer((SSM_HEADS, 1), l),
        _layer((1, C_WIDTH), l), _layer((1, C_WIDTH), l),
        _layer((D_MODEL, D_MODEL), l, single_buffer=True),
        _layer((1, D_MODEL), l), _layer((1, D_MODEL), l),
    ]
    out_shape = (
        jax.ShapeDtypeStruct((bsz, seq, D_MODEL), F32),
        jax.ShapeDtypeStruct((bsz, CHUNK, KV_WIDTH), F32),
        jax.ShapeDtypeStruct((bsz, CHUNK, KV_WIDTH), F32),
        jax.ShapeDtypeStruct((bsz, CONV_W - 1, CONV_DIM), F32),
        jax.ShapeDtypeStruct((bsz, C_WIDTH, D_STATE), F32),
    )
    out_specs = (
        pl.BlockSpec((1, tl, D_MODEL), lambda b, t: (b, t, 0)),
        pl.BlockSpec((1, CHUNK, KV_WIDTH), lambda b, t: (b, 0, 0)),
        pl.BlockSpec((1, CHUNK, KV_WIDTH), lambda b, t: (b, 0, 0)),
        pl.BlockSpec((1, CONV_W - 1, CONV_DIM), lambda b, t: (b, 0, 0)),
        pl.BlockSpec((1, C_WIDTH, D_STATE), lambda b, t: (b, 0, 0)),
    )
    scratch = [
        pltpu.VMEM((tl, OFF_XBC), F32),
        pltpu.VMEM((SUBLANES + tl, CONV_DIM), F32),
        pltpu.VMEM((tl, CONV_DIM), F32),
        pltpu.VMEM((tl, LANES), F32),
        pltpu.VMEM((tl, D_MODEL), BF16),
        pltpu.VMEM((CHUNK, KV_WIDTH), F32),
        pltpu.VMEM((CHUNK, KV_WIDTH), F32),
        pltpu.VMEM((C_WIDTH, D_STATE), F32),
        pltpu.VMEM((4 * CHUNK, CHUNK), BF16),
        pltpu.VMEM((tl, D_MODEL), BF16),
        pltpu.VMEM((tl, D_MODEL), F32),
    ]
    return pl.pallas_call(
        kern, out_shape=out_shape, grid=(bsz, nt), in_specs=in_specs, out_specs=out_specs,
        scratch_shapes=scratch, name="prompt_mixer",
        compiler_params=pltpu.CompilerParams(
            dimension_semantics=("arbitrary", "arbitrary"), vmem_limit_bytes=VMEM_LIMIT),
    )(sinks, x, w_in, lnvg, lnvb, ws, bs_exp, convw, convb, dtb, alog, dskip, gnw, w_out, ln1g, ln1b)


FF_CHUNK = 1024


def _ffn(x, w1_ref, w2_ref, g_ref, b_ref):
    xb = x.astype(BF16)
    acc = jnp.zeros(x.shape, F32)
    for c in range(D_FF // FF_CHUNK):
        sl = slice(c * FF_CHUNK, (c + 1) * FF_CHUNK)
        h = jnp.maximum(_dot(xb, w1_ref[:, sl]), 0.0)
        acc = acc + _dot((h * h).astype(BF16), w2_ref[sl, :])
    return _layer_norm(ALPHA * x + acc, g_ref[...], b_ref[...])


def _ffn_kernel(x_ref, w1_ref, w2_ref, g_ref, b_ref, o_ref):
    o_ref[...] = _ffn(x_ref[...], w1_ref, w2_ref, g_ref, b_ref)


def _ffn_call(l, x2d, w1, w2, g, b, *, tm):
    rows = x2d.shape[0]
    return pl.pallas_call(
        _ffn_kernel, out_shape=jax.ShapeDtypeStruct((rows, D_MODEL), F32),
        grid=(rows // tm,),
        in_specs=[pl.BlockSpec((tm, D_MODEL), lambda i: (i, 0)),
                  _layer((D_MODEL, D_FF), l, single_buffer=True),
                  _layer((D_FF, D_MODEL), l, single_buffer=True),
                  _layer((1, D_MODEL), l), _layer((1, D_MODEL), l)],
        out_specs=pl.BlockSpec((tm, D_MODEL), lambda i: (i, 0)),
        name="ffn",
        compiler_params=pltpu.CompilerParams(
            dimension_semantics=("arbitrary",), vmem_limit_bytes=VMEM_LIMIT),
    )(x2d, w1, w2, g, b)


def _s1_kernel(x_ref, w_in_ref, lnvg_ref, lnvb_ref, ws0_ref, bs0_ref, convw_ref, convb_ref,
               cst_ref, dtb_ref, alog_ref, e_ref,
               vn_ref, outa_ref, q_ref, k_ref, v_ref, z_ref, cnew_ref, xs_ref, bm_ref,
               xT_ref, cT_ref, dec_ref):
    xb = x_ref[...].astype(BF16)
    proj = _dot(xb, w_in_ref[...])
    u = jax.nn.gelu(proj[:, OFF_AU:OFF_AU + A_WIDTH])
    vn = _layer_norm(jax.nn.gelu(proj[:, OFF_AV:OFF_AV + A_WIDTH]), lnvg_ref[...], lnvb_ref[...])
    vn_ref[...] = vn
    outa_ref[...] = u * (ws0_ref[...] * vn + bs0_ref[...])
    q_ref[...] = proj[:, OFF_Q:OFF_Q + B_WIDTH]
    k_ref[...] = proj[:, OFF_K:OFF_K + KV_WIDTH]
    v_ref[...] = proj[:, OFF_V:OFF_V + KV_WIDTH]
    z_ref[...] = proj[:, OFF_Z:OFF_Z + C_WIDTH]
    xbc = proj[:, OFF_XBC:OFF_DT]
    acc = convb_ref[...] + xbc * convw_ref[CONV_W - 1:CONV_W, :]
    for i in range(CONV_W - 1):
        acc = acc + cst_ref[i] * convw_ref[i:i + 1, :]
    cnew_ref[0] = cst_ref[1]
    cnew_ref[1] = cst_ref[2]
    cnew_ref[2] = xbc
    act = _silu(acc)
    xs = act[:, 0:C_WIDTH]
    xs_ref[...] = xs
    bm_ref[...] = act[:, C_WIDTH:C_WIDTH + 2 * D_STATE]
    cT_ref[...] = act[:, C_WIDTH + 2 * D_STATE:CONV_DIM].T
    dt = jax.nn.softplus(proj[:, OFF_DT:D_IN_PAD] + dtb_ref[...])
    dec_ref[...] = jnp.exp(dt * (-jnp.exp(alog_ref[...])))
    dt_exp = _dot_exact_rhs(dt, e_ref[...])
    xT_ref[...] = (xs * dt_exp).T


S2_TB = 2 * SUBLANES


def _s2_kernel(sinks_ref, q_ref, kn_ref, vn_ref, kst_ref, vst_ref, h_ref, xT_ref, bm_ref,
               cT_ref, dec_ref, ob_ref, hnew_ref, yT_ref, *, layer, step):
    i = step
    tb = S2_TB
    lane = lax.broadcasted_iota(jnp.int32, (tb, LANES), 1)
    lo = lane < HEAD_DIM

    q01 = q_ref[:, 0:LANES]
    q23 = q_ref[:, LANES:2 * LANES]
    qh = jnp.concatenate([
        jnp.where(lo, q01, 0.0),
        jnp.where(lo, pltpu.roll(q01, HEAD_DIM, 1), 0.0),
        jnp.where(lo, 0.0, pltpu.roll(q23, HEAD_DIM, 1)),
        jnp.where(lo, 0.0, q23)], axis=0)
    qhb = qh.astype(BF16)
    rowi = lax.broadcasted_iota(jnp.int32, (4 * tb, LANES), 0)
    coli = lax.broadcasted_iota(jnp.int32, (4 * tb, LANES), 1)
    tok = jnp.bitwise_and(rowi, tb - 1)
    s = jnp.zeros((4 * tb, LANES), F32)
    for j in range(tb):
        sj = _dot_nt(qhb, kst_ref[j].astype(BF16))
        s = jnp.where(tok == j, sj, s)
    s = jnp.where(coli >= 1, s * ATT_SCALE, NEG)
    kn4 = jnp.concatenate([kn_ref[...]] * 4, axis=0)
    vn4 = jnp.concatenate([vn_ref[...]] * 4, axis=0)
    s_new = jnp.sum(qh * kn4, axis=-1, keepdims=True) * ATT_SCALE
    head = jnp.right_shift(rowi[:, 0:1], tb.bit_length() - 1)
    sink = jnp.where(head == 0, sinks_ref[layer, 0],
                     jnp.where(head == 1, sinks_ref[layer, 1],
                               jnp.where(head == 2, sinks_ref[layer, 2], sinks_ref[layer, 3])))
    m = jnp.maximum(jnp.maximum(jnp.max(s, axis=-1, keepdims=True), s_new), sink)
    p = jnp.exp(s - m)
    p_new = jnp.exp(s_new - m)
    inv = 1.0 / (jnp.sum(p, axis=-1, keepdims=True) + p_new + jnp.exp(sink - m))
    p = p * inv
    o = (p_new * inv) * vn4
    for j in range(tb):
        pj = jnp.where(tok == j, p, 0.0).astype(BF16)
        o = o + _dot(pj, vst_ref[j].astype(BF16))
    ob_ref[:, 0:LANES] = jnp.where(lo, o[0:tb], pltpu.roll(o[tb:2 * tb], HEAD_DIM, 1))
    ob_ref[:, LANES:2 * LANES] = jnp.where(lo, pltpu.roll(o[2 * tb:3 * tb], HEAD_DIM, 1), o[3 * tb:])

    @pl.when(i == 0)
    def _():
        yT_ref[...] = jnp.zeros_like(yT_ref)

    r8 = lax.broadcasted_iota(jnp.int32, (tb, D_STATE), 0)
    c128 = lax.broadcasted_iota(jnp.int32, (LANES, LANES), 1)
    hp_g = C_WIDTH // 2
    b0 = pl.multiple_of(i * tb, tb)
    zrows = jnp.zeros((LANES - tb, tb * D_STATE), BF16)
    to_front = jnp.bitwise_and(LANES - b0, LANES - 1)
    for g in range(2):
        gs = slice(g * hp_g, (g + 1) * hp_g)
        ns = slice(g * D_STATE, (g + 1) * D_STATE)
        bblk = bm_ref[pl.ds(b0, tb), ns]
        bdiag = jnp.concatenate([jnp.where(r8 == j, bblk, 0.0) for j in range(tb)], axis=1)
        rhs = jnp.concatenate([bdiag.astype(BF16), zrows], axis=0)
        xg = pltpu.roll(xT_ref[gs, :], to_front, 1).astype(BF16)
        upd = _dot(xg, rhs)
        hns = []
        cms = []
        for j in range(tb):
            drow = dec_ref[pl.ds(b0 + j, 1), :]
            parts = []
            for hh in range(4):
                h = g * 4 + hh
                rs = slice(h * HEAD_DIM, (h + 1) * HEAD_DIM)
                hn = (h_ref[j, rs, :] * jnp.broadcast_to(drow[:, h:h + 1], (HEAD_DIM, D_STATE))
                      + upd[hh * HEAD_DIM:(hh + 1) * HEAD_DIM, j * D_STATE:(j + 1) * D_STATE])
                hnew_ref[j, rs, :] = hn
                parts.append(hn)
            hns.append(jnp.concatenate(parts, axis=0).astype(BF16))
            cms.append(jnp.where(c128 == b0 + j, cT_ref[ns, :], 0.0).astype(BF16))
        yT_ref[gs, :] += _dot(jnp.concatenate(hns, axis=1), jnp.concatenate(cms, axis=0))


def _s3_kernel(yT_ref, xs_ref, z_ref, dskip_ref, gnw_ref, outa_ref, outb_ref, x_ref, w_out_ref,
               g_ref, b_ref, x1_ref):
    out_c = _gated_rmsnorm(yT_ref[...].T, xs_ref[...], _silu(z_ref[...]), dskip_ref[...], gnw_ref[...])
    mix = jnp.concatenate([outa_ref[...], outb_ref[...], out_c], axis=1).astype(BF16)
    x1_ref[...] = _layer_norm(ALPHA * x_ref[...] + _dot(mix, w_out_ref[...]), g_ref[...], b_ref[...])


def _sample_kernel(sinks_ref, x_ref, w_in_ref, lnvg_ref, lnvb_ref, ws0_ref, bs0_ref, convw_ref,
                   convb_ref, cst_ref, dtb_ref, alog_ref, e_ref, kst_ref, vst_ref, h_ref,
                   dskip_ref, gnw_ref, w_out_ref, ln1g_ref, ln1b_ref, w1_ref, w2_ref, ln2g_ref, ln2b_ref,
                   ys_ref, vns_ref, ksm_ref, vsm_ref, csm_ref, hnew_ref,
                   res_scr, outa_scr, q_scr, z_scr, xs_scr, bm_scr, xT_scr, cT_scr, dec_scr,
                   outb_scr, yT_scr, x1_scr):
    l = pl.program_id(0)
    i = pl.program_id(1)
    last = pl.num_programs(1) - 1

    @pl.when(jnp.logical_and(l == 0, i == 0))
    def _():
        res_scr[...] = x_ref[...]

    @pl.when(i == 0)
    def _():
        _s1_kernel(res_scr, w_in_ref, lnvg_ref, lnvb_ref, ws0_ref, bs0_ref, convw_ref, convb_ref,
                   cst_ref, dtb_ref, alog_ref, e_ref,
                   vns_ref, outa_scr, q_scr, ksm_ref, vsm_ref, z_scr, csm_ref, xs_scr, bm_scr,
                   xT_scr, cT_scr, dec_scr)

    rows = pl.ds(pl.multiple_of(i * S2_TB, S2_TB), S2_TB)
    _s2_kernel(sinks_ref, q_scr.at[rows], ksm_ref.at[rows], vsm_ref.at[rows], kst_ref, vst_ref,
               h_ref, xT_scr, bm_scr, cT_scr, dec_scr, outb_scr.at[rows], hnew_ref, yT_scr,
               layer=l, step=i)

    @pl.when(i == last)
    def _():
        _s3_kernel(yT_scr, xs_scr, z_scr, dskip_ref, gnw_ref, outa_scr, outb_scr, res_scr,
                   w_out_ref, ln1g_ref, ln1b_ref, x1_scr)
        res_scr[...] = _ffn(x1_scr[...], w1_ref, w2_ref, ln2g_ref, ln2b_ref)
        ys_ref[...] = res_scr[...]


def _sample_call(x, sinks, w_in, lnvg, lnvb, ws0, bs0, convw, convb, cst, dtb, alog, emat,
                 kst, vst, hst, dskip, gnw, w_out, ln1g, ln1b, w1, w2, ln2g, ln2b):
    n = x.shape[0]
    tb = S2_TB

    def per_layer(shape, single_buffer=False):
        k = len(shape)
        mode = pl.Buffered(1) if single_buffer else None
        return pl.BlockSpec((None,) + tuple(shape), lambda l, i: (l,) + (0,) * k, pipeline_mode=mode)

    def per_block(shape):
        k = len(shape) - 1
        return pl.BlockSpec((None,) + tuple(shape), lambda l, i: (l, i) + (0,) * k)

    in_specs = [
        pl.BlockSpec(memory_space=pltpu.SMEM),
        _full((n, D_MODEL)),
        per_layer((D_MODEL, D_IN_PAD), True),
        per_layer((1, A_WIDTH)), per_layer((1, A_WIDTH)), per_layer((1, A_WIDTH)), per_layer((1, A_WIDTH)),
        per_layer((CONV_W, CONV_DIM)), per_layer((1, CONV_DIM)), per_layer((CONV_W - 1, n, CONV_DIM)),
        per_layer((1, LANES)), per_layer((1, LANES)), _full((LANES, C_WIDTH)),
        per_block((tb, CHUNK, KV_WIDTH)), per_block((tb, CHUNK, KV_WIDTH)),
        per_block((tb, C_WIDTH, D_STATE)),
        per_layer((1, C_WIDTH)), per_layer((1, C_WIDTH)),
        per_layer((D_MODEL, D_MODEL), True), per_layer((1, D_MODEL)), per_layer((1, D_MODEL)),
        per_layer((D_MODEL, D_FF), True), per_layer((D_FF, D_MODEL), True),
        per_layer((1, D_MODEL)), per_layer((1, D_MODEL)),
    ]
    out_shape = (
        jax.ShapeDtypeStruct((n, D_MODEL), F32),
        jax.ShapeDtypeStruct((DEPTH, n, A_WIDTH), F32),
        jax.ShapeDtypeStruct((DEPTH, n, KV_WIDTH), F32),
        jax.ShapeDtypeStruct((DEPTH, n, KV_WIDTH), F32),
        jax.ShapeDtypeStruct((DEPTH, CONV_W - 1, n, CONV_DIM), F32),
        jax.ShapeDtypeStruct((DEPTH, n, C_WIDTH, D_STATE), F32),
    )
    out_specs = (
        _full((n, D_MODEL)),
        per_layer((n, A_WIDTH)), per_layer((n, KV_WIDTH)), per_layer((n, KV_WIDTH)),
        per_layer((CONV_W - 1, n, CONV_DIM)),
        per_block((tb, C_WIDTH, D_STATE)),
    )
    scratch = [
        pltpu.VMEM((n, D_MODEL), F32),
        pltpu.VMEM((n, A_WIDTH), F32),
        pltpu.VMEM((n, B_WIDTH), F32),
        pltpu.VMEM((n, C_WIDTH), F32),
        pltpu.VMEM((n, C_WIDTH), F32),
        pltpu.VMEM((n, 2 * D_STATE), F32),
        pltpu.VMEM((C_WIDTH, n), F32),
        pltpu.VMEM((2 * D_STATE, n), F32),
        pltpu.VMEM((n, LANES), F32),
        pltpu.VMEM((n, B_WIDTH), F32),
        pltpu.VMEM((C_WIDTH, n), F32),
        pltpu.VMEM((n, D_MODEL), F32),
    ]
    return pl.pallas_call(
        _sample_kernel, out_shape=out_shape, grid=(DEPTH, n // tb), in_specs=in_specs,
        out_specs=out_specs, scratch_shapes=scratch, name="sample_step",
        compiler_params=pltpu.CompilerParams(
            dimension_semantics=("arbitrary", "arbitrary"), vmem_limit_bytes=VMEM_LIMIT),
    )(sinks, x, w_in, lnvg, lnvb, ws0, bs0, convw, convb, cst, dtb, alog, emat, kst, vst, hst,
      dskip, gnw, w_out, ln1g, ln1b, w1, w2, ln2g, ln2b)


PROMPT_TL = 1024
PROMPT_TM = 1024


def kernel(x_prompt, x_sample, state_attn_k, state_attn_v, state_conv, state_ssm, w_in, ln_v_g,
           ln_v_b, w_s, b_s, sinks, conv_w, conv_b, dt_bias, a_log, d_skip, gn_w, w_out, ln1_g,
           ln1_b, w1, w2, ln2_g, ln2_b):
    bsz, seq, _ = x_prompt.shape
    n_s = x_sample.shape[0]
    d_in = w_in.shape[-1]

    w_in_b = jnp.pad(w_in, ((0, 0), (0, 0), (0, D_IN_PAD - d_in))).astype(BF16)
    w_out_b = w_out.astype(BF16)
    w1_b = w1.astype(BF16)
    w2_b = w2.astype(BF16)
    pad_h = ((0, 0), (0, LANES - SSM_HEADS))
    dtb_p = jnp.pad(dt_bias, pad_h)[:, None, :]
    alog_p = jnp.pad(a_log, pad_h)[:, None, :]
    dskip_e = jnp.repeat(d_skip, HEAD_DIM, axis=-1)[:, None, :]
    bs_e = jnp.repeat(jnp.swapaxes(b_s, 1, 2), HEAD_DIM, axis=-1)
    ws0_e = jnp.repeat(w_s[:, :, 0, 0], HEAD_DIM, axis=-1)[:, None, :]
    bs0_e = jnp.repeat(b_s[:, :, 0], HEAD_DIM, axis=-1)[:, None, :]
    emat = (lax.broadcasted_iota(jnp.int32, (LANES, C_WIDTH), 0)
            == lax.broadcasted_iota(jnp.int32, (LANES, C_WIDTH), 1) // HEAD_DIM).astype(BF16)
    row = lambda a: a[:, None, :]
    lnvg, lnvb, convb, gnw = row(ln_v_g), row(ln_v_b), row(conv_b), row(gn_w)
    ln1g, ln1b, ln2g, ln2b = row(ln1_g), row(ln1_b), row(ln2_g), row(ln2_b)
    kst = state_attn_k.reshape(DEPTH, n_s, CHUNK, KV_WIDTH)
    vst = state_attn_v.reshape(DEPTH, n_s, CHUNK, KV_WIDTH)
    hst = state_ssm.reshape(DEPTH, n_s, C_WIDTH, D_STATE)
    cst = jnp.swapaxes(state_conv, 1, 2)

    ys, vns, ksm, vsm, csm, hsm = _sample_call(
        x_sample.reshape(n_s, D_MODEL), sinks, w_in_b, lnvg, lnvb, ws0_e, bs0_e, conv_w, convb, cst,
        dtb_p, alog_p, emat, kst, vst, hst, dskip_e, gnw, w_out_b, ln1g, ln1b, w1_b, w2_b, ln2g, ln2b)

    yp = x_prompt
    kp, vp, cp, hp = [], [], [], []
    for l in range(DEPTH):
        x1, k_l, v_l, c_l, h_l = _mixer_call(
            l, yp, sinks, w_in_b, lnvg, lnvb, w_s, bs_e, conv_w, convb,
            dt_bias[:, :, None], a_log[:, :, None], dskip_e, gnw, w_out_b, ln1g, ln1b, tl=PROMPT_TL)
        yp = _ffn_call(l, x1.reshape(bsz * seq, D_MODEL), w1_b, w2_b, ln2g, ln2b,
                       tm=PROMPT_TM).reshape(bsz, seq, D_MODEL)
        kp.append(k_l); vp.append(v_l); cp.append(c_l); hp.append(h_l)

    kv_p = (DEPTH, bsz, CHUNK, 2, HEAD_DIM)
    kv_s = (DEPTH, n_s, 1, 2, HEAD_DIM)
    ssm_shape = (SSM_HEADS, HEAD_DIM, D_STATE)
    return (yp, ys.reshape(n_s, 1, D_MODEL),
            jnp.stack(kp).reshape(kv_p), jnp.stack(vp).reshape(kv_p),
            jnp.stack(cp), jnp.stack(hp).reshape((DEPTH, bsz) + ssm_shape),
            ksm.reshape(kv_s), vsm.reshape(kv_s),
            jnp.swapaxes(csm, 1, 2),
            hsm.reshape((DEPTH, n_s) + ssm_shape),
            vns.reshape(DEPTH, n_s, 1, A_WIDTH))
```

```python
import functools

import jax
import jax.numpy as jnp
from jax import lax
from jax.experimental import pallas as pl
from jax.experimental.pallas import tpu as pltpu

F32 = jnp.float32
BF16 = jnp.bfloat16

D_MODEL = 1024
DEPTH = 4
HEAD_DIM = 64
A_WIDTH = 256
B_WIDTH = 256
KV_WIDTH = 128
C_WIDTH = 512
CONV_DIM = 1024
CONV_W = 4
SSM_HEADS = 8
D_STATE = 128
D_FF = 4096
CHUNK = 128
ALPHA = (2 * DEPTH) ** 0.25
LN_EPS = 1e-5
RMS_EPS = 1e-6
ATT_SCALE = HEAD_DIM ** -0.5
LOG2E = 1.4426950408889634
Q_SCALE = ATT_SCALE * LOG2E
NEG = -1e30

OFF_AU, OFF_AV, OFF_Q, OFF_K, OFF_V, OFF_Z = 0, 256, 512, 768, 896, 1024
OFF_XBC = 1536
OFF_DT = 2560
D_IN_PAD = 2688
LANES = 128
SUBLANES = 8
VMEM_LIMIT = 56 * 1024 * 1024

NT_DIMS = (((1,), (1,)), ((), ()))
PIECE = 256


def _dot(a, b):
    return jnp.dot(a, b, preferred_element_type=F32)


def _dot_nt(a, b):
    return lax.dot_general(a, b, NT_DIMS, preferred_element_type=F32)


def _layer_norm(x, g, b):
    mu = jnp.mean(x, axis=-1, keepdims=True)
    xc = x - mu
    var = jnp.mean(xc * xc, axis=-1, keepdims=True)
    return xc * lax.rsqrt(var + LN_EPS) * g + b


def _silu(x):
    return x * jax.nn.sigmoid(x)


def _split3(x):
    hi = x.astype(BF16)
    r = x - hi.astype(F32)
    mid = r.astype(BF16)
    lo = (r - mid.astype(F32)).astype(BF16)
    return hi, mid, lo


def _dot_exact_rhs(x, m_bf16):
    hi, mid, lo = _split3(x)
    return _dot(hi, m_bf16) + _dot(mid, m_bf16) + _dot(lo, m_bf16)


def _gated_rmsnorm(y, xs, gate, dskip, gnw):
    y2 = (y + dskip * xs) * gate
    ms = jnp.mean(y2 * y2, axis=-1, keepdims=True)
    return y2 * lax.rsqrt(ms + RMS_EPS) * gnw


def _mixer_kernel(sinks_ref, x_ref, w_in_ref, lnvg_ref, lnvb_ref, ws_ref, bs_ref,
                  convw_ref, convb_ref, dtb_ref, alog_ref, dskip_ref, gnw_ref,
                  w_out_ref, ln1g_ref, ln1b_ref,
                  x1_ref, klast_ref, vlast_ref, convnew_ref, ssm_ref,
                  proj_ref, xbc_ref, act_ref, dtr_ref, mix_ref, kprev_ref, vprev_ref,
                  h_ref, wstk_ref, xb_ref, mo_ref, *, tl, layer):
    t = pl.program_id(1)
    nt = pl.num_programs(1)

    row_i = lax.broadcasted_iota(jnp.int32, (CHUNK, CHUNK), 0)
    col_i = lax.broadcasted_iota(jnp.int32, (CHUNK, CHUNK), 1)
    causal = col_i <= row_i
    lane_lo = col_i < HEAD_DIM

    @pl.when(t == 0)
    def _init():
        kprev_ref[...] = jnp.zeros_like(kprev_ref)
        vprev_ref[...] = jnp.zeros_like(vprev_ref)
        h_ref[...] = jnp.zeros_like(h_ref)
        xbc_ref[0:SUBLANES, :] = jnp.zeros((SUBLANES, CONV_DIM), F32)
        for h in range(4):
            wstk_ref[h * CHUNK:(h + 1) * CHUNK, :] = jnp.where(causal, ws_ref[h], 0.0).astype(BF16)

    half = tl // 2

    def project_pieces(hf):
        rows = slice(hf * half, (hf + 1) * half)

        def cast():
            xb_ref[rows, :] = x_ref[0, rows, :].astype(BF16)

        epilogue = {
            OFF_AU: jax.nn.gelu,
            OFF_AV: lambda r: _layer_norm(jax.nn.gelu(r), lnvg_ref[...], lnvb_ref[...]),
            OFF_Q: lambda r: r * Q_SCALE,
            OFF_Z: _silu,
            OFF_Z + PIECE: _silu,
        }

        def main(c0):
            def run():
                res = _dot(xb_ref[rows, :], w_in_ref[:, c0:c0 + PIECE])
                proj_ref[rows, c0:c0 + PIECE] = epilogue.get(c0, lambda r: r)(res)
            return run

        def xbc(c0):
            def run():
                cols = slice(c0, c0 + PIECE)
                res = _dot(xb_ref[rows, :], w_in_ref[:, OFF_XBC + c0:OFF_XBC + c0 + PIECE])
                xh = jnp.concatenate([xbc_ref[hf * half:hf * half + SUBLANES, cols], res], axis=0)
                xbc_ref[SUBLANES + hf * half:SUBLANES + (hf + 1) * half, cols] = res
                acc = xh * convw_ref[0:1, cols]
                for i in range(1, CONV_W):
                    acc = pltpu.roll(acc, 1, 0) + xh * convw_ref[i:i + 1, cols]
                act_ref[rows, cols] = _silu(acc[SUBLANES:] + convb_ref[:, cols])
            return run

        def dt():
            dtr_ref[rows, :] = _dot(xb_ref[rows, :], w_in_ref[:, OFF_DT:D_IN_PAD])

        return ([cast] + [main(c0) for c0 in range(0, OFF_XBC, PIECE)]
                + [xbc(c0) for c0 in range(0, CONV_DIM, PIECE)] + [dt])

    def out_pieces(hf):
        rows = slice(hf * half, (hf + 1) * half)

        def part(c0):
            def run():
                mo_ref[rows, c0:c0 + PIECE] = _dot(mix_ref[rows, :], w_out_ref[:, c0:c0 + PIECE])
            return run
        return [part(c0) for c0 in range(0, D_MODEL, PIECE)]

    def norm_pieces(hf):
        def piece(r0):
            def run():
                rows = slice(r0, r0 + CHUNK)
                x1_ref[0, rows, :] = _layer_norm(ALPHA * x_ref[0, rows, :] + mo_ref[rows, :],
                                                 ln1g_ref[...], ln1b_ref[...])
            return run
        return [piece(r0) for r0 in range(hf * half, (hf + 1) * half, CHUNK)]

    triu_b = jnp.where(row_i <= col_i, 1.0, 0.0).astype(BF16)
    a_col = -jnp.exp(alog_ref[...])
    zpad = jnp.zeros((CHUNK - SSM_HEADS, CHUNK), F32)
    lane256 = lax.broadcasted_iota(jnp.int32, (CHUNK, A_WIDTH), 1)
    r256 = lax.broadcasted_iota(jnp.int32, (2 * CHUNK, 2 * CHUNK), 0)
    c256 = lax.broadcasted_iota(jnp.int32, (2 * CHUNK, 2 * CHUNK), 1)
    t256 = jnp.bitwise_and(r256, CHUNK - 1)
    win_mask = (c256 > t256) & (c256 <= t256 + CHUNK)
    rcol = lax.broadcasted_iota(jnp.int32, (2 * CHUNK, 1), 0)
    sink_a = jnp.where(rcol < CHUNK, sinks_ref[layer, 0], sinks_ref[layer, 3]) * LOG2E
    sink_b = jnp.where(rcol < CHUNK, sinks_ref[layer, 1], sinks_ref[layer, 2]) * LOG2E

    def chunk_a(c):
        rows = slice(c * CHUNK, (c + 1) * CHUNK)
        u = proj_ref[rows, OFF_AU:OFF_AU + A_WIDTH]
        vn = proj_ref[rows, OFF_AV:OFF_AV + A_WIDTH]
        pa = _dot(wstk_ref[...], vn.astype(BF16))
        mix_a = jnp.where(
            lane256 < HEAD_DIM, pa[0:CHUNK],
            jnp.where(lane256 < 2 * HEAD_DIM, pa[CHUNK:2 * CHUNK],
                      jnp.where(lane256 < 3 * HEAD_DIM, pa[2 * CHUNK:3 * CHUNK], pa[3 * CHUNK:])))
        mix_ref[rows, 0:A_WIDTH] = (u * (mix_a + bs_ref[...])).astype(BF16)

    def chunk_b(c):
        rows = slice(c * CHUNK, (c + 1) * CHUNK)
        q01 = proj_ref[rows, OFF_Q:OFF_Q + LANES]
        q23 = proj_ref[rows, OFF_Q + LANES:OFF_Q + 2 * LANES]
        kc = proj_ref[rows, OFF_K:OFF_K + KV_WIDTH]
        vc = proj_ref[rows, OFF_V:OFF_V + KV_WIDTH]
        if c == 0:
            kp, vp = kprev_ref[...], vprev_ref[...]
        else:
            prev = slice((c - 1) * CHUNK, c * CHUNK)
            kp = proj_ref[prev, OFF_K:OFF_K + KV_WIDTH]
            vp = proj_ref[prev, OFF_V:OFF_V + KV_WIDTH]
        k2 = jnp.concatenate([kp, kc], axis=0)
        v2 = jnp.concatenate([vp, vc], axis=0)
        k2b = k2.astype(BF16)
        v2b = v2.astype(BF16)
        k2rb = pltpu.roll(k2, HEAD_DIM, 1).astype(BF16)
        v2rb = pltpu.roll(v2, HEAD_DIM, 1).astype(BF16)
        q_a = jnp.concatenate([jnp.where(lane_lo, q01, 0.0), jnp.where(lane_lo, 0.0, q23)], axis=0)
        q_b = jnp.concatenate([jnp.where(lane_lo, 0.0, q01), jnp.where(lane_lo, q23, 0.0)], axis=0)
        vis = win_mask & (c256 >= jnp.where(t == 0, CHUNK, 0)) if c == 0 else win_mask

        def attend(qm, kk, vv, sink2):
            s = jnp.where(vis, _dot_nt(qm.astype(BF16), kk), NEG)
            m = jnp.maximum(jnp.max(s, axis=-1, keepdims=True), sink2)
            p = jnp.exp2(s - m)
            den = jnp.sum(p, axis=-1, keepdims=True) + jnp.exp2(sink2 - m)
            return _dot(p.astype(BF16), vv) * (1.0 / den)

        o_a = attend(q_a, k2b, v2b, sink_a)
        o_b = attend(q_b, k2rb, v2rb, sink_b)
        mix_ref[rows, A_WIDTH:A_WIDTH + LANES] = jnp.where(
            lane_lo, o_a[0:CHUNK], o_b[0:CHUNK]).astype(BF16)
        mix_ref[rows, A_WIDTH + LANES:A_WIDTH + 2 * LANES] = jnp.where(
            lane_lo, o_b[CHUNK:], o_a[CHUNK:]).astype(BF16)

    def chunk_c(c):
        rows = slice(c * CHUNK, (c + 1) * CHUNK)
        xs = act_ref[rows, 0:C_WIDTH]
        bm = act_ref[rows, C_WIDTH:C_WIDTH + 2 * D_STATE]
        cm = act_ref[rows, C_WIDTH + 2 * D_STATE:CONV_DIM]
        gate = proj_ref[rows, OFF_Z:OFF_Z + C_WIDTH]
        dt_row = jax.nn.softplus(dtr_ref[rows, :].T[0:SSM_HEADS] + dtb_ref[...])
        cum_row = _dot_exact_rhs(dt_row * a_col, triu_b) * LOG2E
        cum = jnp.concatenate([cum_row, zpad], axis=0).T
        toend_row = jnp.exp2(cum_row[:, CHUNK - 1:CHUNK] - cum_row) * dt_row
        ecl = jnp.exp2(cum_row[:, CHUNK - 1:CHUNK])
        xt = xs.T
        yts = []
        for g in range(2):
            bg = bm[:, g * D_STATE:(g + 1) * D_STATE]
            cg = cm[:, g * D_STATE:(g + 1) * D_STATE]
            cb = _dot_nt(cg.astype(BF16), bg.astype(BF16))
            xw = []
            for hh in range(4):
                h = g * 4 + hh
                hs = slice(h * HEAD_DIM, (h + 1) * HEAD_DIM)
                cc = jnp.broadcast_to(cum[:, h:h + 1], (CHUNK, CHUNK))
                dec = jnp.exp2(jnp.where(causal, cc - cum_row[h:h + 1, :], NEG))
                m_h = cb * dec * dt_row[h:h + 1, :]
                ce_h = cg * jnp.exp2(cc)
                l_h = jnp.concatenate([m_h, ce_h], axis=1).astype(BF16)
                r_h = jnp.concatenate([xt[hs], h_ref[hs, :]], axis=1).astype(BF16)
                yts.append(_dot_nt(r_h, l_h))
                xw.append(xt[hs] * toend_row[h:h + 1, :])
            gs = slice(g * 4 * HEAD_DIM, (g + 1) * 4 * HEAD_DIM)
            upd = _dot(jnp.concatenate(xw, axis=0).astype(BF16), bg.astype(BF16))
            keep = jnp.concatenate(
                [jnp.broadcast_to(ecl[g * 4 + hh:g * 4 + hh + 1, :], (HEAD_DIM, D_STATE))
                 for hh in range(4)], axis=0)
            h_ref[gs, :] = h_ref[gs, :] * keep + upd
        y = jnp.concatenate(yts, axis=0).T
        out_c = _gated_rmsnorm(y, xs, gate, dskip_ref[...], gnw_ref[...])
        mix_ref[rows, A_WIDTH + B_WIDTH:D_MODEL] = out_c.astype(BF16)

    cph = half // CHUNK

    def chunks(hf):
        for c in range(hf * cph, (hf + 1) * cph):
            chunk_a(c)
            chunk_b(c)
            chunk_c(c)

    for hf in range(2):
        for piece in project_pieces(hf):
            piece()
    for hf in range(2):
        chunks(hf)
        for piece in out_pieces(hf):
            piece()
    for hf in range(2):
        for piece in norm_pieces(hf):
            piece()

    last = slice(tl - CHUNK, tl)
    kprev_ref[...] = proj_ref[last, OFF_K:OFF_K + KV_WIDTH]
    vprev_ref[...] = proj_ref[last, OFF_V:OFF_V + KV_WIDTH]
    tail = xbc_ref[SUBLANES + tl - (CONV_W - 1):SUBLANES + tl, :]
    xbc_ref[SUBLANES - (CONV_W - 1):SUBLANES, :] = tail

    @pl.when(t == nt - 1)
    def _final():
        klast_ref[0] = kprev_ref[...]
        vlast_ref[0] = vprev_ref[...]
        convnew_ref[0] = tail
        ssm_ref[0] = h_ref[...]


def _full(shape):
    n = len(shape)
    return pl.BlockSpec(shape, lambda *_: (0,) * n)


def _layer(shape, l, single_buffer=False):
    n = len(shape)
    mode = pl.Buffered(1) if single_buffer else None
    return pl.BlockSpec((None,) + tuple(shape), lambda *_: (l,) + (0,) * n, pipeline_mode=mode)


def _mixer_call(l, x, sinks, w_in, lnvg, lnvb, ws, bs_exp, convw, convb, dtb, alog, dskip, gnw,
                w_out, ln1g, ln1b, *, tl):
    bsz, seq, _ = x.shape
    nt = seq // tl
    kern = functools.partial(_mixer_kernel, tl=tl, layer=l)
    in_specs = [
        pl.BlockSpec(memory_space=pltpu.SMEM),
        pl.BlockSpec((1, tl, D_MODEL), lambda b, t: (b, t, 0)),
        _layer((D_MODEL, D_IN_PAD), l, single_buffer=True),
        _layer((1, A_WIDTH), l), _layer((1, A_WIDTH), l),
        _layer((4, CHUNK, CHUNK), l), _layer((CHUNK, A_WIDTH), l),
        _layer((CONV_W, CONV_DIM), l), _layer((1, CONV_DIM), l),
        _layer((SSM_HEADS, 1), l), _layer((SSM_HEADS, 1), l),
        _layer((1, C_WIDTH), l), _layer((1, C_WIDTH), l),
        _layer((D_MODEL, D_MODEL), l, single_buffer=True),
        _layer((1, D_MODEL), l), _layer((1, D_MODEL), l),
    ]
    out_shape = (
        jax.ShapeDtypeStruct((bsz, seq, D_MODEL), F32),
        jax.ShapeDtypeStruct((bsz, CHUNK, KV_WIDTH), F32),
        jax.ShapeDtypeStruct((bsz, CHUNK, KV_WIDTH), F32),
        jax.ShapeDtypeStruct((bsz, CONV_W - 1, CONV_DIM), F32),
        jax.ShapeDtypeStruct((bsz, C_WIDTH, D_STATE), F32),
    )
    out_specs = (
        pl.BlockSpec((1, tl, D_MODEL), lambda b, t: (b, t, 0)),
        pl.BlockSpec((1, CHUNK, KV_WIDTH), lambda b, t: (b, 0, 0)),
        pl.BlockSpec((1, CHUNK, KV_WIDTH), lambda b, t: (b, 0, 0)),
        pl.BlockSpec((1, CONV_W - 1, CONV_DIM), lambda b, t: (b, 0, 0)),
        pl.BlockSpec((1, C_WIDTH, D_STATE), lambda b, t: (b, 0, 0)),
    )
    scratch = [
        pltpu.VMEM((tl, OFF_XBC), F32),
        pltpu.VMEM((SUBLANES + tl, CONV_DIM), F32),
        pltpu.VMEM((tl, CONV_DIM), F32),
        pltpu.VMEM((tl, LANES), F32),
        pltpu.VMEM((tl, D_MODEL), BF16),
        pltpu.VMEM((CHUNK, KV_WIDTH), F32),
        pltpu.VMEM((CHUNK, KV_WIDTH), F32),
        pltpu.VMEM((C_WIDTH, D_STATE), F32),
        pltpu.VMEM((4 * CHUNK, CHUNK), BF16),
        pltpu.VMEM((tl, D_MODEL), BF16),
        pltpu.VMEM((tl, D_MODEL), F32),
    ]
    return pl.pallas_call(
        kern, out_shape=out_shape, grid=(bsz, nt), in_specs=in_specs, out_specs=out_specs,
        scratch_shapes=scratch, name="prompt_mixer",
        compiler_params=pltpu.CompilerParams(
            dimension_semantics=("arbitrary", "arbitrary"), vmem_limit_bytes=VMEM_LIMIT),
    )(sinks, x, w_in, lnvg, lnvb, ws, bs_exp, convw, convb, dtb, alog, dskip, gnw, w_out, ln1g, ln1b)


FF_CHUNK = 1024


def _ffn(x, w1_ref, w2_ref, g_ref, b_ref):
    xb = x.astype(BF16)
    acc = jnp.zeros(x.shape, F32)
    for c in range(D_FF // FF_CHUNK):
        sl = slice(c * FF_CHUNK, (c + 1) * FF_CHUNK)
        h = jnp.maximum(_dot(xb, w1_ref[:, sl]), 0.0)
        acc = acc + _dot((h * h).astype(BF16), w2_ref[sl, :])
    return _layer_norm(ALPHA * x + acc, g_ref[...], b_ref[...])


def _ffn_kernel(x_ref, w1_ref, w2_ref, g_ref, b_ref, o_ref):
    o_ref[...] = _ffn(x_ref[...], w1_ref, w2_ref, g_ref, b_ref)


def _ffn_call(l, x2d, w1, w2, g, b, *, tm):
    rows = x2d.shape[0]
    return pl.pallas_call(
        _ffn_kernel, out_shape=jax.ShapeDtypeStruct((rows, D_MODEL), F32),
        grid=(rows // tm,),
        in_specs=[pl.BlockSpec((tm, D_MODEL), lambda i: (i, 0)),
                  _layer((D_MODEL, D_FF), l, single_buffer=True),
                  _layer((D_FF, D_MODEL), l, single_buffer=True),
                  _layer((1, D_MODEL), l), _layer((1, D_MODEL), l)],
        out_specs=pl.BlockSpec((tm, D_MODEL), lambda i: (i, 0)),
        name="ffn",
        compiler_params=pltpu.CompilerParams(
            dimension_semantics=("arbitrary",), vmem_limit_bytes=VMEM_LIMIT),
    )(x2d, w1, w2, g, b)


def _s1_kernel(x_ref, w_in_ref, lnvg_ref, lnvb_ref, ws0_ref, bs0_ref, convw_ref, convb_ref,
               cst_ref, dtb_ref, alog_ref, e_ref,
               vn_ref, outa_ref, q_ref, k_ref, v_ref, z_ref, cnew_ref, xs_ref, bm_ref,
               xT_ref, cT_ref, dec_ref):
    xb = x_ref[...].astype(BF16)
    proj = _dot(xb, w_in_ref[...])
    u = jax.nn.gelu(proj[:, OFF_AU:OFF_AU + A_WIDTH])
    vn = _layer_norm(jax.nn.gelu(proj[:, OFF_AV:OFF_AV + A_WIDTH]), lnvg_ref[...], lnvb_ref[...])
    vn_ref[...] = vn
    outa_ref[...] = u * (ws0_ref[...] * vn + bs0_ref[...])
    q_ref[...] = proj[:, OFF_Q:OFF_Q + B_WIDTH]
    k_ref[...] = proj[:, OFF_K:OFF_K + KV_WIDTH]
    v_ref[...] = proj[:, OFF_V:OFF_V + KV_WIDTH]
    z_ref[...] = proj[:, OFF_Z:OFF_Z + C_WIDTH]
    xbc = proj[:, OFF_XBC:OFF_DT]
    acc = convb_ref[...] + xbc * convw_ref[CONV_W - 1:CONV_W, :]
    for i in range(CONV_W - 1):
        acc = acc + cst_ref[i] * convw_ref[i:i + 1, :]
    cnew_ref[0] = cst_ref[1]
    cnew_ref[1] = cst_ref[2]
    cnew_ref[2] = xbc
    act = _silu(acc)
    xs = act[:, 0:C_WIDTH]
    xs_ref[...] = xs
    bm_ref[...] = act[:, C_WIDTH:C_WIDTH + 2 * D_STATE]
    cT_ref[...] = act[:, C_WIDTH + 2 * D_STATE:CONV_DIM].T
    dt = jax.nn.softplus(proj[:, OFF_DT:D_IN_PAD] + dtb_ref[...])
    dec_ref[...] = jnp.exp(dt * (-jnp.exp(alog_ref[...])))
    dt_exp = _dot_exact_rhs(dt, e_ref[...])
    xT_ref[...] = (xs * dt_exp).T


S2_TB = 2 * SUBLANES


def _s2_kernel(sinks_ref, q_ref, kn_ref, vn_ref, kst_ref, vst_ref, h_ref, xT_ref, bm_ref,
               cT_ref, dec_ref, ob_ref, hnew_ref, yT_ref, *, layer, step):
    i = step
    tb = S2_TB
    lane = lax.broadcasted_iota(jnp.int32, (tb, LANES), 1)
    lo = lane < HEAD_DIM

    q01 = q_ref[:, 0:LANES]
    q23 = q_ref[:, LANES:2 * LANES]
    qh = jnp.concatenate([
        jnp.where(lo, q01, 0.0),
        jnp.where(lo, pltpu.roll(q01, HEAD_DIM, 1), 0.0),
        jnp.where(lo, 0.0, pltpu.roll(q23, HEAD_DIM, 1)),
        jnp.where(lo, 0.0, q23)], axis=0)
    qhb = qh.astype(BF16)
    rowi = lax.broadcasted_iota(jnp.int32, (4 * tb, LANES), 0)
    coli = lax.broadcasted_iota(jnp.int32, (4 * tb, LANES), 1)
    tok = jnp.bitwise_and(rowi, tb - 1)
    s = jnp.zeros((4 * tb, LANES), F32)
    for j in range(tb):
        sj = _dot_nt(qhb, kst_ref[j])
        s = jnp.where(tok == j, sj, s)
    s = jnp.where(coli >= 1, s * ATT_SCALE, NEG)
    kn4 = jnp.concatenate([kn_ref[...]] * 4, axis=0)
    vn4 = jnp.concatenate([vn_ref[...]] * 4, axis=0)
    s_new = jnp.sum(qh * kn4, axis=-1, keepdims=True) * ATT_SCALE
    head = jnp.right_shift(rowi[:, 0:1], tb.bit_length() - 1)
    sink = jnp.where(head == 0, sinks_ref[layer, 0],
                     jnp.where(head == 1, sinks_ref[layer, 1],
                               jnp.where(head == 2, sinks_ref[layer, 2], sinks_ref[layer, 3])))
    m = jnp.maximum(jnp.maximum(jnp.max(s, axis=-1, keepdims=True), s_new), sink)
    p = jnp.exp(s - m)
    p_new = jnp.exp(s_new - m)
    inv = 1.0 / (jnp.sum(p, axis=-1, keepdims=True) + p_new + jnp.exp(sink - m))
    p = p * inv
    o = (p_new * inv) * vn4
    for j in range(tb):
        pj = jnp.where(tok == j, p, 0.0).astype(BF16)
        o = o + _dot(pj, vst_ref[j])
    ob_ref[:, 0:LANES] = jnp.where(lo, o[0:tb], pltpu.roll(o[tb:2 * tb], HEAD_DIM, 1))
    ob_ref[:, LANES:2 * LANES] = jnp.where(lo, pltpu.roll(o[2 * tb:3 * tb], HEAD_DIM, 1), o[3 * tb:])

    @pl.when(i == 0)
    def _():
        yT_ref[...] = jnp.zeros_like(yT_ref)

    r8 = lax.broadcasted_iota(jnp.int32, (tb, D_STATE), 0)
    c128 = lax.broadcasted_iota(jnp.int32, (LANES, LANES), 1)
    hp_g = C_WIDTH // 2
    b0 = pl.multiple_of(i * tb, tb)
    zrows = jnp.zeros((LANES - tb, tb * D_STATE), BF16)
    to_front = jnp.bitwise_and(LANES - b0, LANES - 1)
    for g in range(2):
        gs = slice(g * hp_g, (g + 1) * hp_g)
        ns = slice(g * D_STATE, (g + 1) * D_STATE)
        bblk = bm_ref[pl.ds(b0, tb), ns]
        bdiag = jnp.concatenate([jnp.where(r8 == j, bblk, 0.0) for j in range(tb)], axis=1)
        rhs = jnp.concatenate([bdiag.astype(BF16), zrows], axis=0)
        xg = pltpu.roll(xT_ref[gs, :], to_front, 1).astype(BF16)
        upd = _dot(xg, rhs)
        hns = []
        cms = []
        for j in range(tb):
            drow = dec_ref[pl.ds(b0 + j, 1), :]
            parts = []
            for hh in range(4):
                h = g * 4 + hh
                rs = slice(h * HEAD_DIM, (h + 1) * HEAD_DIM)
                hn = (h_ref[j, rs, :] * jnp.broadcast_to(drow[:, h:h + 1], (HEAD_DIM, D_STATE))
                      + upd[hh * HEAD_DIM:(hh + 1) * HEAD_DIM, j * D_STATE:(j + 1) * D_STATE])
                hnew_ref[j, rs, :] = hn
                parts.append(hn)
            hns.append(jnp.concatenate(parts, axis=0).astype(BF16))
            cms.append(jnp.where(c128 == b0 + j, cT_ref[ns, :], 0.0).astype(BF16))
        yT_ref[gs, :] += _dot(jnp.concatenate(hns, axis=1), jnp.concatenate(cms, axis=0))


def _s3_kernel(yT_ref, xs_ref, z_ref, dskip_ref, gnw_ref, outa_ref, outb_ref, x_ref, w_out_ref,
               g_ref, b_ref, x1_ref):
    out_c = _gated_rmsnorm(yT_ref[...].T, xs_ref[...], _silu(z_ref[...]), dskip_ref[...], gnw_ref[...])
    mix = jnp.concatenate([outa_ref[...], outb_ref[...], out_c], axis=1).astype(BF16)
    x1_ref[...] = _layer_norm(ALPHA * x_ref[...] + _dot(mix, w_out_ref[...]), g_ref[...], b_ref[...])


def _sample_kernel(sinks_ref, x_ref, w_in_ref, lnvg_ref, lnvb_ref, ws0_ref, bs0_ref, convw_ref,
                   convb_ref, cst_ref, dtb_ref, alog_ref, e_ref, kst_ref, vst_ref, h_ref,
                   dskip_ref, gnw_ref, w_out_ref, ln1g_ref, ln1b_ref, w1_ref, w2_ref, ln2g_ref, ln2b_ref,
                   ys_ref, vns_ref, ksm_ref, vsm_ref, csm_ref, hnew_ref,
                   res_scr, outa_scr, q_scr, z_scr, xs_scr, bm_scr, xT_scr, cT_scr, dec_scr,
                   outb_scr, yT_scr, x1_scr):
    l = pl.program_id(0)
    i = pl.program_id(1)
    last = pl.num_programs(1) - 1

    @pl.when(jnp.logical_and(l == 0, i == 0))
    def _():
        res_scr[...] = x_ref[...]

    @pl.when(i == 0)
    def _():
        _s1_kernel(res_scr, w_in_ref, lnvg_ref, lnvb_ref, ws0_ref, bs0_ref, convw_ref, convb_ref,
                   cst_ref, dtb_ref, alog_ref, e_ref,
                   vns_ref, outa_scr, q_scr, ksm_ref, vsm_ref, z_scr, csm_ref, xs_scr, bm_scr,
                   xT_scr, cT_scr, dec_scr)

    rows = pl.ds(pl.multiple_of(i * S2_TB, S2_TB), S2_TB)
    _s2_kernel(sinks_ref, q_scr.at[rows], ksm_ref.at[rows], vsm_ref.at[rows], kst_ref, vst_ref,
               h_ref, xT_scr, bm_scr, cT_scr, dec_scr, outb_scr.at[rows], hnew_ref, yT_scr,
               layer=l, step=i)

    @pl.when(i == last)
    def _():
        _s3_kernel(yT_scr, xs_scr, z_scr, dskip_ref, gnw_ref, outa_scr, outb_scr, res_scr,
                   w_out_ref, ln1g_ref, ln1b_ref, x1_scr)
        res_scr[...] = _ffn(x1_scr[...], w1_ref, w2_ref, ln2g_ref, ln2b_ref)
        ys_ref[...] = res_scr[...]


def _sample_call(x, sinks, w_in, lnvg, lnvb, ws0, bs0, convw, convb, cst, dtb, alog, emat,
                 kst, vst, hst, dskip, gnw, w_out, ln1g, ln1b, w1, w2, ln2g, ln2b):
    n = x.shape[0]
    tb = S2_TB

    def per_layer(shape, single_buffer=False):
        k = len(shape)
        mode = pl.Buffered(1) if single_buffer else None
        return pl.BlockSpec((None,) + tuple(shape), lambda l, i: (l,) + (0,) * k, pipeline_mode=mode)

    def per_block(shape):
        k = len(shape) - 1
        return pl.BlockSpec((None,) + tuple(shape), lambda l, i: (l, i) + (0,) * k)

    in_specs = [
        pl.BlockSpec(memory_space=pltpu.SMEM),
        _full((n, D_MODEL)),
        per_layer((D_MODEL, D_IN_PAD), True),
        per_layer((1, A_WIDTH)), per_layer((1, A_WIDTH)), per_layer((1, A_WIDTH)), per_layer((1, A_WIDTH)),
        per_layer((CONV_W, CONV_DIM)), per_layer((1, CONV_DIM)), per_layer((CONV_W - 1, n, CONV_DIM)),
        per_layer((1, LANES)), per_layer((1, LANES)), _full((LANES, C_WIDTH)),
        per_block((tb, CHUNK, KV_WIDTH)), per_block((tb, CHUNK, KV_WIDTH)),
        per_block((tb, C_WIDTH, D_STATE)),
        per_layer((1, C_WIDTH)), per_layer((1, C_WIDTH)),
        per_layer((D_MODEL, D_MODEL), True), per_layer((1, D_MODEL)), per_layer((1, D_MODEL)),
        per_layer((D_MODEL, D_FF), True), per_layer((D_FF, D_MODEL), True),
        per_layer((1, D_MODEL)), per_layer((1, D_MODEL)),
    ]
    out_shape = (
        jax.ShapeDtypeStruct((n, D_MODEL), F32),
        jax.ShapeDtypeStruct((DEPTH, n, A_WIDTH), F32),
        jax.ShapeDtypeStruct((DEPTH, n, KV_WIDTH), F32),
        jax.ShapeDtypeStruct((DEPTH, n, KV_WIDTH), F32),
        jax.ShapeDtypeStruct((DEPTH, CONV_W - 1, n, CONV_DIM), F32),
        jax.ShapeDtypeStruct((DEPTH, n, C_WIDTH, D_STATE), F32),
    )
    out_specs = (
        _full((n, D_MODEL)),
        per_layer((n, A_WIDTH)), per_layer((n, KV_WIDTH)), per_layer((n, KV_WIDTH)),
        per_layer((CONV_W - 1, n, CONV_DIM)),
        per_block((tb, C_WIDTH, D_STATE)),
    )
    scratch = [
        pltpu.VMEM((n, D_MODEL), F32),
        pltpu.VMEM((n, A_WIDTH), F32),
        pltpu.VMEM((n, B_WIDTH), F32),
        pltpu.VMEM((n, C_WIDTH), F32),
        pltpu.VMEM((n, C_WIDTH), F32),
        pltpu.VMEM((n, 2 * D_STATE), F32),
        pltpu.VMEM((C_WIDTH, n), F32),
        pltpu.VMEM((2 * D_STATE, n), F32),
        pltpu.VMEM((n, LANES), F32),
        pltpu.VMEM((n, B_WIDTH), F32),
        pltpu.VMEM((C_WIDTH, n), F32),
        pltpu.VMEM((n, D_MODEL), F32),
    ]
    return pl.pallas_call(
        _sample_kernel, out_shape=out_shape, grid=(DEPTH, n // tb), in_specs=in_specs,
        out_specs=out_specs, scratch_shapes=scratch, name="sample_step",
        compiler_params=pltpu.CompilerParams(
            dimension_semantics=("arbitrary", "arbitrary"), vmem_limit_bytes=VMEM_LIMIT),
    )(sinks, x, w_in, lnvg, lnvb, ws0, bs0, convw, convb, cst, dtb, alog, emat, kst, vst, hst,
      dskip, gnw, w_out, ln1g, ln1b, w1, w2, ln2g, ln2b)


PROMPT_TL = 1024
PROMPT_TM = 1024


def kernel(x_prompt, x_sample, state_attn_k, state_attn_v, state_conv, state_ssm, w_in, ln_v_g,
           ln_v_b, w_s, b_s, sinks, conv_w, conv_b, dt_bias, a_log, d_skip, gn_w, w_out, ln1_g,
           ln1_b, w1, w2, ln2_g, ln2_b):
    bsz, seq, _ = x_prompt.shape
    n_s = x_sample.shape[0]
    d_in = w_in.shape[-1]

    w_in_b = jnp.pad(w_in, ((0, 0), (0, 0), (0, D_IN_PAD - d_in))).astype(BF16)
    w_out_b = w_out.astype(BF16)
    w1_b = w1.astype(BF16)
    w2_b = w2.astype(BF16)
    pad_h = ((0, 0), (0, LANES - SSM_HEADS))
    dtb_p = jnp.pad(dt_bias, pad_h)[:, None, :]
    alog_p = jnp.pad(a_log, pad_h)[:, None, :]
    dskip_e = jnp.repeat(d_skip, HEAD_DIM, axis=-1)[:, None, :]
    bs_e = jnp.repeat(jnp.swapaxes(b_s, 1, 2), HEAD_DIM, axis=-1)
    ws0_e = jnp.repeat(w_s[:, :, 0, 0], HEAD_DIM, axis=-1)[:, None, :]
    bs0_e = jnp.repeat(b_s[:, :, 0], HEAD_DIM, axis=-1)[:, None, :]
    emat = (lax.broadcasted_iota(jnp.int32, (LANES, C_WIDTH), 0)
            == lax.broadcasted_iota(jnp.int32, (LANES, C_WIDTH), 1) // HEAD_DIM).astype(BF16)
    row = lambda a: a[:, None, :]
    lnvg, lnvb, convb, gnw = row(ln_v_g), row(ln_v_b), row(conv_b), row(gn_w)
    ln1g, ln1b, ln2g, ln2b = row(ln1_g), row(ln1_b), row(ln2_g), row(ln2_b)
    kst = state_attn_k.reshape(DEPTH, n_s, CHUNK, KV_WIDTH).astype(BF16)
    vst = state_attn_v.reshape(DEPTH, n_s, CHUNK, KV_WIDTH).astype(BF16)
    hst = state_ssm.reshape(DEPTH, n_s, C_WIDTH, D_STATE)
    cst = jnp.swapaxes(state_conv, 1, 2)

    ys, vns, ksm, vsm, csm, hsm = _sample_call(
        x_sample.reshape(n_s, D_MODEL), sinks, w_in_b, lnvg, lnvb, ws0_e, bs0_e, conv_w, convb, cst,
        dtb_p, alog_p, emat, kst, vst, hst, dskip_e, gnw, w_out_b, ln1g, ln1b, w1_b, w2_b, ln2g, ln2b)

    yp = x_prompt
    kp, vp, cp, hp = [], [], [], []
    for l in range(DEPTH):
        x1, k_l, v_l, c_l, h_l = _mixer_call(
            l, yp, sinks, w_in_b, lnvg, lnvb, w_s, bs_e, conv_w, convb,
            dt_bias[:, :, None], a_log[:, :, None], dskip_e, gnw, w_out_b, ln1g, ln1b, tl=PROMPT_TL)
        yp = _ffn_call(l, x1.reshape(bsz * seq, D_MODEL), w1_b, w2_b, ln2g, ln2b,
                       tm=PROMPT_TM).reshape(bsz, seq, D_MODEL)
        kp.append(k_l); vp.append(v_l); cp.append(c_l); hp.append(h_l)

    kv_p = (DEPTH, bsz, CHUNK, 2, HEAD_DIM)
    kv_s = (DEPTH, n_s, 1, 2, HEAD_DIM)
    ssm_shape = (SSM_HEADS, HEAD_DIM, D_STATE)
    return (yp, ys.reshape(n_s, 1, D_MODEL),
            jnp.stack(kp).reshape(kv_p), jnp.stack(vp).reshape(kv_p),
            jnp.stack(cp), jnp.stack(hp).reshape((DEPTH, bsz) + ssm_shape),
            ksm.reshape(kv_s), vsm.reshape(kv_s),
            jnp.swapaxes(csm, 1, 2),
            hsm.reshape((DEPTH, n_s) + ssm_shape),
            vns.reshape(DEPTH, n_s, 1, A_WIDTH))
```

```python
import functools

import jax
import jax.numpy as jnp
from jax import lax
from jax.experimental import pallas as pl
from jax.experimental.pallas import tpu as pltpu

F32 = jnp.float32
BF16 = jnp.bfloat16

D_MODEL = 1024
DEPTH = 4
HEAD_DIM = 64
A_WIDTH = 256
B_WIDTH = 256
KV_WIDTH = 128
C_WIDTH = 512
CONV_DIM = 1024
CONV_W = 4
SSM_HEADS = 8
D_STATE = 128
D_FF = 4096
CHUNK = 128
ALPHA = (2 * DEPTH) ** 0.25
LN_EPS = 1e-5
RMS_EPS = 1e-6
ATT_SCALE = HEAD_DIM ** -0.5
LOG2E = 1.4426950408889634
Q_SCALE = ATT_SCALE * LOG2E
NEG = -1e30

OFF_AU, OFF_AV, OFF_Q, OFF_K, OFF_V, OFF_Z = 0, 256, 512, 768, 896, 1024
OFF_XBC = 1536
OFF_DT = 2560
D_IN_PAD = 2688
LANES = 128
SUBLANES = 8
VMEM_LIMIT = 56 * 1024 * 1024

NT_DIMS = (((1,), (1,)), ((), ()))
PIECE = 256


def _dot(a, b):
    return jnp.dot(a, b, preferred_element_type=F32)


def _dot_nt(a, b):
    return lax.dot_general(a, b, NT_DIMS, preferred_element_type=F32)


def _layer_norm(x, g, b):
    mu = jnp.mean(x, axis=-1, keepdims=True)
    xc = x - mu
    var = jnp.mean(xc * xc, axis=-1, keepdims=True)
    return xc * lax.rsqrt(var + LN_EPS) * g + b


def _silu(x):
    return x * jax.nn.sigmoid(x)


def _split3(x):
    hi = x.astype(BF16)
    r = x - hi.astype(F32)
    mid = r.astype(BF16)
    lo = (r - mid.astype(F32)).astype(BF16)
    return hi, mid, lo


def _dot_exact_rhs(x, m_bf16):
    hi, mid, lo = _split3(x)
    return _dot(hi, m_bf16) + _dot(mid, m_bf16) + _dot(lo, m_bf16)


def _gated_rmsnorm(y, xs, gate, dskip, gnw):
    y2 = (y + dskip * xs) * gate
    ms = jnp.mean(y2 * y2, axis=-1, keepdims=True)
    return y2 * lax.rsqrt(ms + RMS_EPS) * gnw


def _mixer_kernel(sinks_ref, x_ref, w_in_ref, lnvg_ref, lnvb_ref, ws_ref, bs_ref,
                  convw_ref, convb_ref, dtb_ref, alog_ref, dskip_ref, gnw_ref,
                  w_out_ref, ln1g_ref, ln1b_ref,
                  x1_ref, klast_ref, vlast_ref, convnew_ref, ssm_ref,
                  proj_ref, xbc_ref, act_ref, dtr_ref, mix_ref, kprev_ref, vprev_ref,
                  h_ref, wstk_ref, xb_ref, mo_ref, *, tl, layer):
    t = pl.program_id(1)
    nt = pl.num_programs(1)

    row_i = lax.broadcasted_iota(jnp.int32, (CHUNK, CHUNK), 0)
    col_i = lax.broadcasted_iota(jnp.int32, (CHUNK, CHUNK), 1)
    causal = col_i <= row_i
    lane_lo = col_i < HEAD_DIM

    @pl.when(t == 0)
    def _init():
        kprev_ref[...] = jnp.zeros_like(kprev_ref)
        vprev_ref[...] = jnp.zeros_like(vprev_ref)
        h_ref[...] = jnp.zeros_like(h_ref)
        xbc_ref[0:SUBLANES, :] = jnp.zeros((SUBLANES, CONV_DIM), F32)
        for h in range(4):
            wstk_ref[h * CHUNK:(h + 1) * CHUNK, :] = jnp.where(causal, ws_ref[h], 0.0).astype(BF16)

    half = tl // 2

    def project_pieces(hf):
        rows = slice(hf * half, (hf + 1) * half)

        def cast():
            xb_ref[rows, :] = x_ref[0, rows, :].astype(BF16)

        epilogue = {
            OFF_AU: jax.nn.gelu,
            OFF_AV: lambda r: _layer_norm(jax.nn.gelu(r), lnvg_ref[...], lnvb_ref[...]),
            OFF_Q: lambda r: r * Q_SCALE,
            OFF_Z: _silu,
            OFF_Z + PIECE: _silu,
        }

        def main(c0):
            def run():
                res = _dot(xb_ref[rows, :], w_in_ref[:, c0:c0 + PIECE])
                proj_ref[rows, c0:c0 + PIECE] = epilogue.get(c0, lambda r: r)(res)
            return run

        def xbc(c0):
            def run():
                cols = slice(c0, c0 + PIECE)
                res = _dot(xb_ref[rows, :], w_in_ref[:, OFF_XBC + c0:OFF_XBC + c0 + PIECE])
                xh = jnp.concatenate([xbc_ref[hf * half:hf * half + SUBLANES, cols], res], axis=0)
                xbc_ref[SUBLANES + hf * half:SUBLANES + (hf + 1) * half, cols] = res
                acc = xh * convw_ref[0:1, cols]
                for i in range(1, CONV_W):
                    acc = pltpu.roll(acc, 1, 0) + xh * convw_ref[i:i + 1, cols]
                act_ref[rows, cols] = _silu(acc[SUBLANES:] + convb_ref[:, cols])
            return run

        def dt():
            dtr_ref[rows, :] = _dot(xb_ref[rows, :], w_in_ref[:, OFF_DT:D_IN_PAD])

        return ([cast] + [main(c0) for c0 in range(0, OFF_XBC, PIECE)]
                + [xbc(c0) for c0 in range(0, CONV_DIM, PIECE)] + [dt])

    def out_pieces(hf):
        rows = slice(hf * half, (hf + 1) * half)

        def part(c0):
            def run():
                mo_ref[rows, c0:c0 + PIECE] = _dot(mix_ref[rows, :], w_out_ref[:, c0:c0 + PIECE])
            return run
        return [part(c0) for c0 in range(0, D_MODEL, PIECE)]

    def norm_pieces(hf):
        def piece(r0):
            def run():
                rows = slice(r0, r0 + CHUNK)
                x1_ref[0, rows, :] = _layer_norm(ALPHA * x_ref[0, rows, :] + mo_ref[rows, :],
                                                 ln1g_ref[...], ln1b_ref[...])
            return run
        return [piece(r0) for r0 in range(hf * half, (hf + 1) * half, CHUNK)]

    triu_b = jnp.where(row_i <= col_i, 1.0, 0.0).astype(BF16)
    a_col = -jnp.exp(alog_ref[...])
    zpad = jnp.zeros((CHUNK - SSM_HEADS, CHUNK), F32)
    lane256 = lax.broadcasted_iota(jnp.int32, (CHUNK, A_WIDTH), 1)
    r256 = lax.broadcasted_iota(jnp.int32, (2 * CHUNK, 2 * CHUNK), 0)
    c256 = lax.broadcasted_iota(jnp.int32, (2 * CHUNK, 2 * CHUNK), 1)
    t256 = jnp.bitwise_and(r256, CHUNK - 1)
    win_mask = (c256 > t256) & (c256 <= t256 + CHUNK)
    rcol = lax.broadcasted_iota(jnp.int32, (2 * CHUNK, 1), 0)
    sink_a = jnp.where(rcol < CHUNK, sinks_ref[layer, 0], sinks_ref[layer, 3]) * LOG2E
    sink_b = jnp.where(rcol < CHUNK, sinks_ref[layer, 1], sinks_ref[layer, 2]) * LOG2E

    def chunk_a(c):
        rows = slice(c * CHUNK, (c + 1) * CHUNK)
        u = proj_ref[rows, OFF_AU:OFF_AU + A_WIDTH]
        vn = proj_ref[rows, OFF_AV:OFF_AV + A_WIDTH]
        pa = _dot(wstk_ref[...], vn.astype(BF16))
        mix_a = jnp.where(
            lane256 < HEAD_DIM, pa[0:CHUNK],
            jnp.where(lane256 < 2 * HEAD_DIM, pa[CHUNK:2 * CHUNK],
                      jnp.where(lane256 < 3 * HEAD_DIM, pa[2 * CHUNK:3 * CHUNK], pa[3 * CHUNK:])))
        mix_ref[rows, 0:A_WIDTH] = (u * (mix_a + bs_ref[...])).astype(BF16)

    def chunk_b(c):
        rows = slice(c * CHUNK, (c + 1) * CHUNK)
        q01 = proj_ref[rows, OFF_Q:OFF_Q + LANES]
        q23 = proj_ref[rows, OFF_Q + LANES:OFF_Q + 2 * LANES]
        kc = proj_ref[rows, OFF_K:OFF_K + KV_WIDTH]
        vc = proj_ref[rows, OFF_V:OFF_V + KV_WIDTH]
        if c == 0:
            kp, vp = kprev_ref[...], vprev_ref[...]
        else:
            prev = slice((c - 1) * CHUNK, c * CHUNK)
            kp = proj_ref[prev, OFF_K:OFF_K + KV_WIDTH]
            vp = proj_ref[prev, OFF_V:OFF_V + KV_WIDTH]
        k2 = jnp.concatenate([kp, kc], axis=0)
        v2 = jnp.concatenate([vp, vc], axis=0)
        k2b = k2.astype(BF16)
        v2b = v2.astype(BF16)
        k2rb = pltpu.roll(k2, HEAD_DIM, 1).astype(BF16)
        v2rb = pltpu.roll(v2, HEAD_DIM, 1).astype(BF16)
        q_a = jnp.concatenate([jnp.where(lane_lo, q01, 0.0), jnp.where(lane_lo, 0.0, q23)], axis=0)
        q_b = jnp.concatenate([jnp.where(lane_lo, 0.0, q01), jnp.where(lane_lo, q23, 0.0)], axis=0)
        vis = win_mask & (c256 >= jnp.where(t == 0, CHUNK, 0)) if c == 0 else win_mask

        def attend(qm, kk, vv, sink2):
            s = jnp.where(vis, _dot_nt(qm.astype(BF16), kk), NEG)
            m = jnp.maximum(jnp.max(s, axis=-1, keepdims=True), sink2)
            p = jnp.exp2(s - m)
            den = jnp.sum(p, axis=-1, keepdims=True) + jnp.exp2(sink2 - m)
            return _dot(p.astype(BF16), vv) * (1.0 / den)

        o_a = attend(q_a, k2b, v2b, sink_a)
        o_b = attend(q_b, k2rb, v2rb, sink_b)
        mix_ref[rows, A_WIDTH:A_WIDTH + LANES] = jnp.where(
            lane_lo, o_a[0:CHUNK], o_b[0:CHUNK]).astype(BF16)
        mix_ref[rows, A_WIDTH + LANES:A_WIDTH + 2 * LANES] = jnp.where(
            lane_lo, o_b[CHUNK:], o_a[CHUNK:]).astype(BF16)

    def chunk_c(c):
        rows = slice(c * CHUNK, (c + 1) * CHUNK)
        xs = act_ref[rows, 0:C_WIDTH]
        bm = act_ref[rows, C_WIDTH:C_WIDTH + 2 * D_STATE]
        cm = act_ref[rows, C_WIDTH + 2 * D_STATE:CONV_DIM]
        gate = proj_ref[rows, OFF_Z:OFF_Z + C_WIDTH]
        dt_row = jax.nn.softplus(dtr_ref[rows, :].T[0:SSM_HEADS] + dtb_ref[...])
        cum_row = _dot_exact_rhs(dt_row * a_col, triu_b) * LOG2E
        cum = jnp.concatenate([cum_row, zpad], axis=0).T
        toend_row = jnp.exp2(cum_row[:, CHUNK - 1:CHUNK] - cum_row) * dt_row
        ecl = jnp.exp2(cum_row[:, CHUNK - 1:CHUNK])
        xt = xs.T
        yts = []
        for g in range(2):
            bg = bm[:, g * D_STATE:(g + 1) * D_STATE]
            cg = cm[:, g * D_STATE:(g + 1) * D_STATE]
            cb = _dot_nt(cg.astype(BF16), bg.astype(BF16))
            xw = []
            for hh in range(4):
                h = g * 4 + hh
                hs = slice(h * HEAD_DIM, (h + 1) * HEAD_DIM)
                cc = jnp.broadcast_to(cum[:, h:h + 1], (CHUNK, CHUNK))
                dec = jnp.exp2(jnp.where(causal, cc - cum_row[h:h + 1, :], NEG))
                m_h = cb * dec * dt_row[h:h + 1, :]
                ce_h = cg * jnp.exp2(cc)
                l_h = jnp.concatenate([m_h, ce_h], axis=1).astype(BF16)
                r_h = jnp.concatenate([xt[hs], h_ref[hs, :]], axis=1).astype(BF16)
                yts.append(_dot_nt(r_h, l_h))
                xw.append(xt[hs] * toend_row[h:h + 1, :])
            gs = slice(g * 4 * HEAD_DIM, (g + 1) * 4 * HEAD_DIM)
            upd = _dot(jnp.concatenate(xw, axis=0).astype(BF16), bg.astype(BF16))
            keep = jnp.concatenate(
                [jnp.broadcast_to(ecl[g * 4 + hh:g * 4 + hh + 1, :], (HEAD_DIM, D_STATE))
                 for hh in range(4)], axis=0)
            h_ref[gs, :] = h_ref[gs, :] * keep + upd
        y = jnp.concatenate(yts, axis=0).T
        out_c = _gated_rmsnorm(y, xs, gate, dskip_ref[...], gnw_ref[...])
        mix_ref[rows, A_WIDTH + B_WIDTH:D_MODEL] = out_c.astype(BF16)

    cph = half // CHUNK

    def chunks(hf):
        for c in range(hf * cph, (hf + 1) * cph):
            chunk_a(c)
            chunk_b(c)
            chunk_c(c)

    for hf in range(2):
        for piece in project_pieces(hf):
            piece()
    for hf in range(2):
        chunks(hf)
        for piece in out_pieces(hf):
            piece()
    for hf in range(2):
        for piece in norm_pieces(hf):
            piece()

    last = slice(tl - CHUNK, tl)
    kprev_ref[...] = proj_ref[last, OFF_K:OFF_K + KV_WIDTH]
    vprev_ref[...] = proj_ref[last, OFF_V:OFF_V + KV_WIDTH]
    tail = xbc_ref[SUBLANES + tl - (CONV_W - 1):SUBLANES + tl, :]
    xbc_ref[SUBLANES - (CONV_W - 1):SUBLANES, :] = tail

    @pl.when(t == nt - 1)
    def _final():
        klast_ref[0] = kprev_ref[...]
        vlast_ref[0] = vprev_ref[...]
        convnew_ref[0] = tail
        ssm_ref[0] = h_ref[...]


def _full(shape):
    n = len(shape)
    return pl.BlockSpec(shape, lambda *_: (0,) * n)


def _layer(shape, l, single_buffer=False):
    n = len(shape)
    mode = pl.Buffered(1) if single_buffer else None
    return pl.BlockSpec((None,) + tuple(shape), lambda *_: (l,) + (0,) * n, pipeline_mode=mode)


def _mixer_call(l, x, sinks, w_in, lnvg, lnvb, ws, bs_exp, convw, convb, dtb, alog, dskip, gnw,
                w_out, ln1g, ln1b, *, tl):
    bsz, seq, _ = x.shape
    nt = seq // tl
    kern = functools.partial(_mixer_kernel, tl=tl, layer=l)
    in_specs = [
        pl.BlockSpec(memory_space=pltpu.SMEM),
        pl.BlockSpec((1, tl, D_MODEL), lambda b, t: (b, t, 0)),
        _layer((D_MODEL, D_IN_PAD), l, single_buffer=True),
        _layer((1, A_WIDTH), l), _layer((1, A_WIDTH), l),
        _layer((4, CHUNK, CHUNK), l), _layer((CHUNK, A_WIDTH), l),
        _layer((CONV_W, CONV_DIM), l), _layer((1, CONV_DIM), l),
        _layer((SSM_HEADS, 1), l), _layer((SSM_HEADS, 1), l),
        _layer((1, C_WIDTH), l), _layer((1, C_WIDTH), l),
        _layer((D_MODEL, D_MODEL), l, single_buffer=True),
        _layer((1, D_MODEL), l), _layer((1, D_MODEL), l),
    ]
    out_shape = (
        jax.ShapeDtypeStruct((bsz, seq, D_MODEL), F32),
        jax.ShapeDtypeStruct((bsz, CHUNK, KV_WIDTH), F32),
        jax.ShapeDtypeStruct((bsz, CHUNK, KV_WIDTH), F32),
        jax.ShapeDtypeStruct((bsz, CONV_W - 1, CONV_DIM), F32),
        jax.ShapeDtypeStruct((bsz, C_WIDTH, D_STATE), F32),
    )
    out_specs = (
        pl.BlockSpec((1, tl, D_MODEL), lambda b, t: (b, t, 0)),
        pl.BlockSpec((1, CHUNK, KV_WIDTH), lambda b, t: (b, 0, 0)),
        pl.BlockSpec((1, CHUNK, KV_WIDTH), lambda b, t: (b, 0, 0)),
        pl.BlockSpec((1, CONV_W - 1, CONV_DIM), lambda b, t: (b, 0, 0)),
        pl.BlockSpec((1, C_WIDTH, D_STATE), lambda b, t: (b, 0, 0)),
    )
    scratch = [
        pltpu.VMEM((tl, OFF_XBC), F32),
        pltpu.VMEM((SUBLANES + tl, CONV_DIM), F32),
        pltpu.VMEM((tl, CONV_DIM), F32),
        pltpu.VMEM((tl, LANES), F32),
        pltpu.VMEM((tl, D_MODEL), BF16),
        pltpu.VMEM((CHUNK, KV_WIDTH), F32),
        pltpu.VMEM((CHUNK, KV_WIDTH), F32),
        pltpu.VMEM((C_WIDTH, D_STATE), F32),
        pltpu.VMEM((4 * CHUNK, CHUNK), BF16),
        pltpu.VMEM((tl, D_MODEL), BF16),
        pltpu.VMEM((tl, D_MODEL), F32),
    ]
    return pl.pallas_call(
        kern, out_shape=out_shape, grid=(bsz, nt), in_specs=in_specs, out_specs=out_specs,
        scratch_shapes=scratch, name="prompt_mixer",
        compiler_params=pltpu.CompilerParams(
            dimension_semantics=("arbitrary", "arbitrary"), vmem_limit_bytes=VMEM_LIMIT),
    )(sinks, x, w_in, lnvg, lnvb, ws, bs_exp, convw, convb, dtb, alog, dskip, gnw, w_out, ln1g, ln1b)


FF_CHUNK = 1024


def _ffn(x, w1_ref, w2_ref, g_ref, b_ref):
    xb = x.astype(BF16)
    acc = jnp.zeros(x.shape, F32)
    for c in range(D_FF // FF_CHUNK):
        sl = slice(c * FF_CHUNK, (c + 1) * FF_CHUNK)
        h = jnp.maximum(_dot(xb, w1_ref[:, sl]), 0.0)
        acc = acc + _dot((h * h).astype(BF16), w2_ref[sl, :])
    return _layer_norm(ALPHA * x + acc, g_ref[...], b_ref[...])


def _ffn_kernel(x_ref, w1_ref, w2_ref, g_ref, b_ref, o_ref):
    o_ref[...] = _ffn(x_ref[...], w1_ref, w2_ref, g_ref, b_ref)


def _ffn_call(l, x2d, w1, w2, g, b, *, tm):
    rows = x2d.shape[0]
    return pl.pallas_call(
        _ffn_kernel, out_shape=jax.ShapeDtypeStruct((rows, D_MODEL), F32),
        grid=(rows // tm,),
        in_specs=[pl.BlockSpec((tm, D_MODEL), lambda i: (i, 0)),
                  _layer((D_MODEL, D_FF), l, single_buffer=True),
                  _layer((D_FF, D_MODEL), l, single_buffer=True),
                  _layer((1, D_MODEL), l), _layer((1, D_MODEL), l)],
        out_specs=pl.BlockSpec((tm, D_MODEL), lambda i: (i, 0)),
        name="ffn",
        compiler_params=pltpu.CompilerParams(
            dimension_semantics=("arbitrary",), vmem_limit_bytes=VMEM_LIMIT),
    )(x2d, w1, w2, g, b)


def _s1_kernel(x_ref, w_in_ref, lnvg_ref, lnvb_ref, ws0_ref, bs0_ref, convw_ref, convb_ref,
               cst_ref, dtb_ref, alog_ref, e_ref,
               vn_ref, outa_ref, q_ref, k_ref, v_ref, z_ref, cnew_ref, xs_ref, bm_ref,
               xT_ref, cT_ref, dec_ref):
    xb = x_ref[...].astype(BF16)
    proj = _dot(xb, w_in_ref[...])
    u = jax.nn.gelu(proj[:, OFF_AU:OFF_AU + A_WIDTH])
    vn = _layer_norm(jax.nn.gelu(proj[:, OFF_AV:OFF_AV + A_WIDTH]), lnvg_ref[...], lnvb_ref[...])
    vn_ref[...] = vn
    outa_ref[...] = u * (ws0_ref[...] * vn + bs0_ref[...])
    q_ref[...] = proj[:, OFF_Q:OFF_Q + B_WIDTH]
    k_ref[...] = proj[:, OFF_K:OFF_K + KV_WIDTH]
    v_ref[...] = proj[:, OFF_V:OFF_V + KV_WIDTH]
    z_ref[...] = proj[:, OFF_Z:OFF_Z + C_WIDTH]
    xbc = proj[:, OFF_XBC:OFF_DT]
    acc = convb_ref[...] + xbc * convw_ref[CONV_W - 1:CONV_W, :]
    for i in range(CONV_W - 1):
        acc = acc + cst_ref[i] * convw_ref[i:i + 1, :]
    cnew_ref[0] = cst_ref[1]
    cnew_ref[1] = cst_ref[2]
    cnew_ref[2] = xbc
    act = _silu(acc)
    xs = act[:, 0:C_WIDTH]
    xs_ref[...] = xs
    bm_ref[...] = act[:, C_WIDTH:C_WIDTH + 2 * D_STATE]
    cT_ref[...] = act[:, C_WIDTH + 2 * D_STATE:CONV_DIM].T
    dt = jax.nn.softplus(proj[:, OFF_DT:D_IN_PAD] + dtb_ref[...])
    dec_ref[...] = jnp.exp(dt * (-jnp.exp(alog_ref[...])))
    dt_exp = _dot_exact_rhs(dt, e_ref[...])
    xT_ref[...] = (xs * dt_exp).T


S2_TB = 2 * SUBLANES


def _s2_kernel(sinks_ref, q_ref, kn_ref, vn_ref, kst_ref, vst_ref, h_ref, xT_ref, bm_ref,
               cT_ref, dec_ref, ob_ref, hnew_ref, yT_ref, *, layer, step):
    i = step
    tb = S2_TB
    lane = lax.broadcasted_iota(jnp.int32, (tb, LANES), 1)
    lo = lane < HEAD_DIM

    q01 = q_ref[:, 0:LANES]
    q23 = q_ref[:, LANES:2 * LANES]
    qh = jnp.concatenate([
        jnp.where(lo, q01, 0.0),
        jnp.where(lo, pltpu.roll(q01, HEAD_DIM, 1), 0.0),
        jnp.where(lo, 0.0, pltpu.roll(q23, HEAD_DIM, 1)),
        jnp.where(lo, 0.0, q23)], axis=0)
    qhb = qh.astype(BF16)
    rowi = lax.broadcasted_iota(jnp.int32, (4 * tb, LANES), 0)
    coli = lax.broadcasted_iota(jnp.int32, (4 * tb, LANES), 1)
    tok = jnp.bitwise_and(rowi, tb - 1)
    s = jnp.zeros((4 * tb, LANES), F32)
    for j in range(tb):
        sj = _dot_nt(qhb, kst_ref[j])
        s = jnp.where(tok == j, sj, s)
    s = jnp.where(coli >= 1, s * ATT_SCALE, NEG)
    kn4 = jnp.concatenate([kn_ref[...]] * 4, axis=0)
    vn4 = jnp.concatenate([vn_ref[...]] * 4, axis=0)
    s_new = jnp.sum(qh * kn4, axis=-1, keepdims=True) * ATT_SCALE
    head = jnp.right_shift(rowi[:, 0:1], tb.bit_length() - 1)
    sink = jnp.where(head == 0, sinks_ref[layer, 0],
                     jnp.where(head == 1, sinks_ref[layer, 1],
                               jnp.where(head == 2, sinks_ref[layer, 2], sinks_ref[layer, 3])))
    m = jnp.maximum(jnp.maximum(jnp.max(s, axis=-1, keepdims=True), s_new), sink)
    p = jnp.exp(s - m)
    p_new = jnp.exp(s_new - m)
    inv = 1.0 / (jnp.sum(p, axis=-1, keepdims=True) + p_new + jnp.exp(sink - m))
    p = p * inv
    o = (p_new * inv) * vn4
    for j in range(tb):
        pj = jnp.where(tok == j, p, 0.0).astype(BF16)
        o = o + _dot(pj, vst_ref[j])
    ob_ref[:, 0:LANES] = jnp.where(lo, o[0:tb], pltpu.roll(o[tb:2 * tb], HEAD_DIM, 1))
    ob_ref[:, LANES:2 * LANES] = jnp.where(lo, pltpu.roll(o[2 * tb:3 * tb], HEAD_DIM, 1), o[3 * tb:])

    @pl.when(i == 0)
    def _():
        yT_ref[...] = jnp.zeros_like(yT_ref)

    r8 = lax.broadcasted_iota(jnp.int32, (tb, D_STATE), 0)
    c128 = lax.broadcasted_iota(jnp.int32, (LANES, LANES), 1)
    hp_g = C_WIDTH // 2
    b0 = pl.multiple_of(i * tb, tb)
    zrows = jnp.zeros((LANES - tb, tb * D_STATE), BF16)
    to_front = jnp.bitwise_and(LANES - b0, LANES - 1)
    for g in range(2):
        gs = slice(g * hp_g, (g + 1) * hp_g)
        ns = slice(g * D_STATE, (g + 1) * D_STATE)
        bblk = bm_ref[pl.ds(b0, tb), ns]
        bdiag = jnp.concatenate([jnp.where(r8 == j, bblk, 0.0) for j in range(tb)], axis=1)
        rhs = jnp.concatenate([bdiag.astype(BF16), zrows], axis=0)
        xg = pltpu.roll(xT_ref[gs, :], to_front, 1).astype(BF16)
        upd = _dot(xg, rhs)
        hns = []
        cms = []
        for j in range(tb):
            drow = dec_ref[pl.ds(b0 + j, 1), :]
            parts = []
            for hh in range(4):
                h = g * 4 + hh
                rs = slice(h * HEAD_DIM, (h + 1) * HEAD_DIM)
                hn = (h_ref[j, rs, :] * jnp.broadcast_to(drow[:, h:h + 1], (HEAD_DIM, D_STATE))
                      + upd[hh * HEAD_DIM:(hh + 1) * HEAD_DIM, j * D_STATE:(j + 1) * D_STATE])
                hnew_ref[j, rs, :] = hn
                parts.append(hn)
            hns.append(jnp.concatenate(parts, axis=0).astype(BF16))
            cms.append(jnp.where(c128 == b0 + j, cT_ref[ns, :], 0.0).astype(BF16))
        yT_ref[gs, :] += _dot(jnp.concatenate(hns, axis=1), jnp.concatenate(cms, axis=0))


def _s3_kernel(yT_ref, xs_ref, z_ref, dskip_ref, gnw_ref, outa_ref, outb_ref, x_ref, w_out_ref,
               g_ref, b_ref, x1_ref):
    out_c = _gated_rmsnorm(yT_ref[...].T, xs_ref[...], _silu(z_ref[...]), dskip_ref[...], gnw_ref[...])
    mix = jnp.concatenate([outa_ref[...], outb_ref[...], out_c], axis=1).astype(BF16)
    x1_ref[...] = _layer_norm(ALPHA * x_ref[...] + _dot(mix, w_out_ref[...]), g_ref[...], b_ref[...])


def _sample_kernel(sinks_ref, x_ref, w_in_ref, lnvg_ref, lnvb_ref, ws0_ref, bs0_ref, convw_ref,
                   convb_ref, cst_ref, dtb_ref, alog_ref, e_ref, kst_ref, vst_ref, h_ref,
                   dskip_ref, gnw_ref, w_out_ref, ln1g_ref, ln1b_ref, w1_ref, w2_ref, ln2g_ref, ln2b_ref,
                   ys_ref, vns_ref, ksm_ref, vsm_ref, csm_ref, hnew_ref,
                   res_scr, outa_scr, q_scr, z_scr, xs_scr, bm_scr, xT_scr, cT_scr, dec_scr,
                   outb_scr, yT_scr, x1_scr):
    l = pl.program_id(0)
    i = pl.program_id(1)
    last = pl.num_programs(1) - 1

    @pl.when(jnp.logical_and(l == 0, i == 0))
    def _():
        res_scr[...] = x_ref[...]

    @pl.when(i == 0)
    def _():
        _s1_kernel(res_scr, w_in_ref, lnvg_ref, lnvb_ref, ws0_ref, bs0_ref, convw_ref, convb_ref,
                   cst_ref, dtb_ref, alog_ref, e_ref,
                   vns_ref, outa_scr, q_scr, ksm_ref, vsm_ref, z_scr, csm_ref, xs_scr, bm_scr,
                   xT_scr, cT_scr, dec_scr)

    rows = pl.ds(pl.multiple_of(i * S2_TB, S2_TB), S2_TB)
    _s2_kernel(sinks_ref, q_scr.at[rows], ksm_ref.at[rows], vsm_ref.at[rows], kst_ref, vst_ref,
               h_ref, xT_scr, bm_scr, cT_scr, dec_scr, outb_scr.at[rows], hnew_ref, yT_scr,
               layer=l, step=i)

    @pl.when(i == last)
    def _():
        _s3_kernel(yT_scr, xs_scr, z_scr, dskip_ref, gnw_ref, outa_scr, outb_scr, res_scr,
                   w_out_ref, ln1g_ref, ln1b_ref, x1_scr)
        res_scr[...] = _ffn(x1_scr[...], w1_ref, w2_ref, ln2g_ref, ln2b_ref)
        ys_ref[...] = res_scr[...]


def _sample_call(x, sinks, w_in, lnvg, lnvb, ws0, bs0, convw, convb, cst, dtb, alog, emat,
                 kst, vst, hst, dskip, gnw, w_out, ln1g, ln1b, w1, w2, ln2g, ln2b):
    n = x.shape[0]
    tb = S2_TB

    def per_layer(shape, single_buffer=False):
        k = len(shape)
        mode = pl.Buffered(1) if single_buffer else None
        return pl.BlockSpec((None,) + tuple(shape), lambda l, i: (l,) + (0,) * k, pipeline_mode=mode)

    def per_block(shape):
        k = len(shape) - 1
        return pl.BlockSpec((None,) + tuple(shape), lambda l, i: (l, i) + (0,) * k)

    in_specs = [
        pl.BlockSpec(memory_space=pltpu.SMEM),
        _full((n, D_MODEL)),
        per_layer((D_MODEL, D_IN_PAD), True),
        per_layer((1, A_WIDTH)), per_layer((1, A_WIDTH)), per_layer((1, A_WIDTH)), per_layer((1, A_WIDTH)),
        per_layer((CONV_W, CONV_DIM)), per_layer((1, CONV_DIM)), per_layer((CONV_W - 1, n, CONV_DIM)),
        per_layer((1, LANES)), per_layer((1, LANES)), _full((LANES, C_WIDTH)),
        per_block((tb, CHUNK, KV_WIDTH)), per_block((tb, CHUNK, KV_WIDTH)),
        per_block((tb, C_WIDTH, D_STATE)),
        per_layer((1, C_WIDTH)), per_layer((1, C_WIDTH)),
        per_layer((D_MODEL, D_MODEL), True), per_layer((1, D_MODEL)), per_layer((1, D_MODEL)),
        per_layer((D_MODEL, D_FF), True), per_layer((D_FF, D_MODEL), True),
        per_layer((1, D_MODEL)), per_layer((1, D_MODEL)),
    ]
    out_shape = (
        jax.ShapeDtypeStruct((n, D_MODEL), F32),
        jax.ShapeDtypeStruct((DEPTH, n, A_WIDTH), F32),
        jax.ShapeDtypeStruct((DEPTH, n, KV_WIDTH), F32),
        jax.ShapeDtypeStruct((DEPTH, n, KV_WIDTH), F32),
        jax.ShapeDtypeStruct((DEPTH, CONV_W - 1, n, CONV_DIM), F32),
        jax.ShapeDtypeStruct((DEPTH, n, C_WIDTH, D_STATE), F32),
    )
    out_specs = (
        _full((n, D_MODEL)),
        per_layer((n, A_WIDTH)), per_layer((n, KV_WIDTH)), per_layer((n, KV_WIDTH)),
        per_layer((CONV_W - 1, n, CONV_DIM)),
        per_block((tb, C_WIDTH, D_STATE)),
    )
    scratch = [
        pltpu.VMEM((n, D_MODEL), F32),
        pltpu.VMEM((n, A_WIDTH), F32),
        pltpu.VMEM((n, B_WIDTH), F32),
        pltpu.VMEM((n, C_WIDTH), F32),
        pltpu.VMEM((n, C_WIDTH), F32),
        pltpu.VMEM((n, 2 * D_STATE), F32),
        pltpu.VMEM((C_WIDTH, n), F32),
        pltpu.VMEM((2 * D_STATE, n), F32),
        pltpu.VMEM((n, LANES), F32),
        pltpu.VMEM((n, B_WIDTH), F32),
        pltpu.VMEM((C_WIDTH, n), F32),
        pltpu.VMEM((n, D_MODEL), F32),
    ]
    return pl.pallas_call(
        _sample_kernel, out_shape=out_shape, grid=(DEPTH, n // tb), in_specs=in_specs,
        out_specs=out_specs, scratch_shapes=scratch, name="sample_step",
        compiler_params=pltpu.CompilerParams(
            dimension_semantics=("arbitrary", "arbitrary"), vmem_limit_bytes=VMEM_LIMIT),
    )(sinks, x, w_in, lnvg, lnvb, ws0, bs0, convw, convb, cst, dtb, alog, emat, kst, vst, hst,
      dskip, gnw, w_out, ln1g, ln1b, w1, w2, ln2g, ln2b)


PROMPT_TL = 1024
PROMPT_TM = 1024


def kernel(x_prompt, x_sample, state_attn_k, state_attn_v, state_conv, state_ssm, w_in, ln_v_g,
           ln_v_b, w_s, b_s, sinks, conv_w, conv_b, dt_bias, a_log, d_skip, gn_w, w_out, ln1_g,
           ln1_b, w1, w2, ln2_g, ln2_b):
    bsz, seq, _ = x_prompt.shape
    n_s = x_sample.shape[0]
    d_in = w_in.shape[-1]

    w_in_b = jnp.pad(w_in, ((0, 0), (0, 0), (0, D_IN_PAD - d_in))).astype(BF16)
    w_out_b = w_out.astype(BF16)
    w1_b = w1.astype(BF16)
    w2_b = w2.astype(BF16)
    pad_h = ((0, 0), (0, LANES - SSM_HEADS))
    dtb_p = jnp.pad(dt_bias, pad_h)[:, None, :]
    alog_p = jnp.pad(a_log, pad_h)[:, None, :]
    dskip_e = jnp.repeat(d_skip, HEAD_DIM, axis=-1)[:, None, :]
    bs_e = jnp.repeat(jnp.swapaxes(b_s, 1, 2), HEAD_DIM, axis=-1)
    ws0_e = jnp.repeat(w_s[:, :, 0, 0], HEAD_DIM, axis=-1)[:, None, :]
    bs0_e = jnp.repeat(b_s[:, :, 0], HEAD_DIM, axis=-1)[:, None, :]
    emat = (lax.broadcasted_iota(jnp.int32, (LANES, C_WIDTH), 0)
            == lax.broadcasted_iota(jnp.int32, (LANES, C_WIDTH), 1) // HEAD_DIM).astype(BF16)
    row = lambda a: a[:, None, :]
    lnvg, lnvb, convb, gnw = row(ln_v_g), row(ln_v_b), row(conv_b), row(gn_w)
    ln1g, ln1b, ln2g, ln2b = row(ln1_g), row(ln1_b), row(ln2_g), row(ln2_b)
    kst = state_attn_k.astype(BF16).reshape(DEPTH, n_s, CHUNK, KV_WIDTH)
    vst = state_attn_v.astype(BF16).reshape(DEPTH, n_s, CHUNK, KV_WIDTH)
    hst = state_ssm.reshape(DEPTH, n_s, C_WIDTH, D_STATE)
    cst = jnp.swapaxes(state_conv, 1, 2)

    ys, vns, ksm, vsm, csm, hsm = _sample_call(
        x_sample.reshape(n_s, D_MODEL), sinks, w_in_b, lnvg, lnvb, ws0_e, bs0_e, conv_w, convb, cst,
        dtb_p, alog_p, emat, kst, vst, hst, dskip_e, gnw, w_out_b, ln1g, ln1b, w1_b, w2_b, ln2g, ln2b)

    yp = x_prompt
    kp, vp, cp, hp = [], [], [], []
    for l in range(DEPTH):
        x1, k_l, v_l, c_l, h_l = _mixer_call(
            l, yp, sinks, w_in_b, lnvg, lnvb, w_s, bs_e, conv_w, convb,
            dt_bias[:, :, None], a_log[:, :, None], dskip_e, gnw, w_out_b, ln1g, ln1b, tl=PROMPT_TL)
        yp = _ffn_call(l, x1.reshape(bsz * seq, D_MODEL), w1_b, w2_b, ln2g, ln2b,
                       tm=PROMPT_TM).reshape(bsz, seq, D_MODEL)
        kp.append(k_l); vp.append(v_l); cp.append(c_l); hp.append(h_l)

    kv_p = (DEPTH, bsz, CHUNK, 2, HEAD_DIM)
    kv_s = (DEPTH, n_s, 1, 2, HEAD_DIM)
    ssm_shape = (SSM_HEADS, HEAD_DIM, D_STATE)
    return (yp, ys.reshape(n_s, 1, D_MODEL),
            jnp.stack(kp).reshape(kv_p), jnp.stack(vp).reshape(kv_p),
            jnp.stack(cp), jnp.stack(hp).reshape((DEPTH, bsz) + ssm_shape),
            ksm.reshape(kv_s), vsm.reshape(kv_s),
            jnp.swapaxes(csm, 1, 2),
            hsm.reshape((DEPTH, n_s) + ssm_shape),
            vns.reshape(DEPTH, n_s, 1, A_WIDTH))
```

```python
import functools

import jax
import jax.numpy as jnp
from jax import lax
from jax.experimental import pallas as pl
from jax.experimental.pallas import tpu as pltpu

F32 = jnp.float32
BF16 = jnp.bfloat16

D_MODEL = 1024
DEPTH = 4
HEAD_DIM = 64
A_WIDTH = 256
B_WIDTH = 256
KV_WIDTH = 128
C_WIDTH = 512
CONV_DIM = 1024
CONV_W = 4
SSM_HEADS = 8
D_STATE = 128
D_FF = 4096
CHUNK = 128
ALPHA = (2 * DEPTH) ** 0.25
LN_EPS = 1e-5
RMS_EPS = 1e-6
ATT_SCALE = HEAD_DIM ** -0.5
LOG2E = 1.4426950408889634
Q_SCALE = ATT_SCALE * LOG2E
NEG = -1e30

OFF_AU, OFF_AV, OFF_Q, OFF_K, OFF_V, OFF_Z = 0, 256, 512, 768, 896, 1024
OFF_XBC = 1536
OFF_DT = 2560
D_IN_PAD = 2688
LANES = 128
SUBLANES = 8
VMEM_LIMIT = 56 * 1024 * 1024

NT_DIMS = (((1,), (1,)), ((), ()))
PIECE = 256


def _dot(a, b):
    return jnp.dot(a, b, preferred_element_type=F32)


def _dot_nt(a, b):
    return lax.dot_general(a, b, NT_DIMS, preferred_element_type=F32)


def _layer_norm(x, g, b):
    mu = jnp.mean(x, axis=-1, keepdims=True)
    xc = x - mu
    var = jnp.mean(xc * xc, axis=-1, keepdims=True)
    return xc * lax.rsqrt(var + LN_EPS) * g + b


def _silu(x):
    return x * jax.nn.sigmoid(x)


def _split3(x):
    hi = x.astype(BF16)
    r = x - hi.astype(F32)
    mid = r.astype(BF16)
    lo = (r - mid.astype(F32)).astype(BF16)
    return hi, mid, lo


def _dot_exact_rhs(x, m_bf16):
    hi, mid, lo = _split3(x)
    return _dot(hi, m_bf16) + _dot(mid, m_bf16) + _dot(lo, m_bf16)


def _gated_rmsnorm(y, xs, gate, dskip, gnw):
    y2 = (y + dskip * xs) * gate
    ms = jnp.mean(y2 * y2, axis=-1, keepdims=True)
    return y2 * lax.rsqrt(ms + RMS_EPS) * gnw


def _mixer_kernel(sinks_ref, x_ref, w_in_ref, lnvg_ref, lnvb_ref, ws_ref, bs_ref,
                  convw_ref, convb_ref, dtb_ref, alog_ref, dskip_ref, gnw_ref,
                  w_out_ref, ln1g_ref, ln1b_ref,
                  x1_ref, klast_ref, vlast_ref, convnew_ref, ssm_ref,
                  proj_ref, xbc_ref, act_ref, dtr_ref, mix_ref, kprev_ref, vprev_ref,
                  h_ref, wstk_ref, xb_ref, mo_ref, *, tl, layer):
    t = pl.program_id(1)
    nt = pl.num_programs(1)

    row_i = lax.broadcasted_iota(jnp.int32, (CHUNK, CHUNK), 0)
    col_i = lax.broadcasted_iota(jnp.int32, (CHUNK, CHUNK), 1)
    causal = col_i <= row_i
    lane_lo = col_i < HEAD_DIM

    @pl.when(t == 0)
    def _init():
        kprev_ref[...] = jnp.zeros_like(kprev_ref)
        vprev_ref[...] = jnp.zeros_like(vprev_ref)
        h_ref[...] = jnp.zeros_like(h_ref)
        xbc_ref[0:SUBLANES, :] = jnp.zeros((SUBLANES, CONV_DIM), F32)
        for h in range(4):
            wstk_ref[h * CHUNK:(h + 1) * CHUNK, :] = jnp.where(causal, ws_ref[h], 0.0).astype(BF16)

    half = tl // 2

    def project_pieces(hf):
        rows = slice(hf * half, (hf + 1) * half)

        def cast():
            xb_ref[rows, :] = x_ref[0, rows, :].astype(BF16)

        epilogue = {
            OFF_AU: jax.nn.gelu,
            OFF_AV: lambda r: _layer_norm(jax.nn.gelu(r), lnvg_ref[...], lnvb_ref[...]),
            OFF_Q: lambda r: r * Q_SCALE,
            OFF_Z: _silu,
            OFF_Z + PIECE: _silu,
        }

        def main(c0):
            def run():
                res = _dot(xb_ref[rows, :], w_in_ref[:, c0:c0 + PIECE])
                proj_ref[rows, c0:c0 + PIECE] = epilogue.get(c0, lambda r: r)(res)
            return run

        def xbc(c0):
            def run():
                cols = slice(c0, c0 + PIECE)
                res = _dot(xb_ref[rows, :], w_in_ref[:, OFF_XBC + c0:OFF_XBC + c0 + PIECE])
                xh = jnp.concatenate([xbc_ref[hf * half:hf * half + SUBLANES, cols], res], axis=0)
                xbc_ref[SUBLANES + hf * half:SUBLANES + (hf + 1) * half, cols] = res
                acc = xh * convw_ref[0:1, cols]
                for i in range(1, CONV_W):
                    acc = pltpu.roll(acc, 1, 0) + xh * convw_ref[i:i + 1, cols]
                act_ref[rows, cols] = _silu(acc[SUBLANES:] + convb_ref[:, cols])
            return run

        def dt():
            dtr_ref[rows, :] = _dot(xb_ref[rows, :], w_in_ref[:, OFF_DT:D_IN_PAD])

        return ([cast] + [main(c0) for c0 in range(0, OFF_XBC, PIECE)]
                + [xbc(c0) for c0 in range(0, CONV_DIM, PIECE)] + [dt])

    def out_pieces(hf):
        rows = slice(hf * half, (hf + 1) * half)

        def part(c0):
            def run():
                mo_ref[rows, c0:c0 + PIECE] = _dot(mix_ref[rows, :], w_out_ref[:, c0:c0 + PIECE])
            return run
        return [part(c0) for c0 in range(0, D_MODEL, PIECE)]

    def norm_pieces(hf):
        def piece(r0):
            def run():
                rows = slice(r0, r0 + CHUNK)
                x1_ref[0, rows, :] = _layer_norm(ALPHA * x_ref[0, rows, :] + mo_ref[rows, :],
                                                 ln1g_ref[...], ln1b_ref[...])
            return run
        return [piece(r0) for r0 in range(hf * half, (hf + 1) * half, CHUNK)]

    triu_b = jnp.where(row_i <= col_i, 1.0, 0.0).astype(BF16)
    a_col = -jnp.exp(alog_ref[...])
    zpad = jnp.zeros((CHUNK - SSM_HEADS, CHUNK), F32)
    lane256 = lax.broadcasted_iota(jnp.int32, (CHUNK, A_WIDTH), 1)
    t256 = lax.broadcasted_iota(jnp.int32, (CHUNK, 2 * CHUNK), 0)
    c256 = lax.broadcasted_iota(jnp.int32, (CHUNK, 2 * CHUNK), 1)
    win_mask = (c256 > t256) & (c256 <= t256 + CHUNK)
    sink2 = [sinks_ref[layer, h] * LOG2E for h in range(4)]

    def chunk_a(c):
        rows = slice(c * CHUNK, (c + 1) * CHUNK)
        u = proj_ref[rows, OFF_AU:OFF_AU + A_WIDTH]
        vn = proj_ref[rows, OFF_AV:OFF_AV + A_WIDTH]
        pa = _dot(wstk_ref[...], vn.astype(BF16))
        mix_a = jnp.where(
            lane256 < HEAD_DIM, pa[0:CHUNK],
            jnp.where(lane256 < 2 * HEAD_DIM, pa[CHUNK:2 * CHUNK],
                      jnp.where(lane256 < 3 * HEAD_DIM, pa[2 * CHUNK:3 * CHUNK], pa[3 * CHUNK:])))
        mix_ref[rows, 0:A_WIDTH] = (u * (mix_a + bs_ref[...])).astype(BF16)

    def chunk_b(c):
        rows = slice(c * CHUNK, (c + 1) * CHUNK)
        q01 = proj_ref[rows, OFF_Q:OFF_Q + LANES]
        q23 = proj_ref[rows, OFF_Q + LANES:OFF_Q + 2 * LANES]
        kc = proj_ref[rows, OFF_K:OFF_K + KV_WIDTH]
        vc = proj_ref[rows, OFF_V:OFF_V + KV_WIDTH]
        if c == 0:
            kp, vp = kprev_ref[...], vprev_ref[...]
        else:
            prev = slice((c - 1) * CHUNK, c * CHUNK)
            kp = proj_ref[prev, OFF_K:OFF_K + KV_WIDTH]
            vp = proj_ref[prev, OFF_V:OFF_V + KV_WIDTH]
        k2 = jnp.concatenate([kp, kc], axis=0)
        v2 = jnp.concatenate([vp, vc], axis=0)
        k2b = k2.astype(BF16)
        v2b = v2.astype(BF16)
        k2rb = pltpu.roll(k2, HEAD_DIM, 1).astype(BF16)
        v2rb = pltpu.roll(v2, HEAD_DIM, 1).astype(BF16)
        vis = win_mask & (c256 >= jnp.where(t == 0, CHUNK, 0)) if c == 0 else win_mask

        def attend(qm, kk, vv, sink):
            s = jnp.where(vis, _dot_nt(qm.astype(BF16), kk), NEG)
            m = jnp.maximum(jnp.max(s, axis=-1, keepdims=True), sink)
            p = jnp.exp2(s - m)
            den = jnp.sum(p, axis=-1, keepdims=True) + jnp.exp2(sink - m)
            return _dot(p.astype(BF16), vv) * (1.0 / den)

        o0 = attend(jnp.where(lane_lo, q01, 0.0), k2b, v2b, sink2[0])
        o1 = attend(jnp.where(lane_lo, 0.0, q01), k2rb, v2rb, sink2[1])
        o2 = attend(jnp.where(lane_lo, q23, 0.0), k2rb, v2rb, sink2[2])
        o3 = attend(jnp.where(lane_lo, 0.0, q23), k2b, v2b, sink2[3])
        mix_ref[rows, A_WIDTH:A_WIDTH + LANES] = jnp.where(lane_lo, o0, o1).astype(BF16)
        mix_ref[rows, A_WIDTH + LANES:A_WIDTH + 2 * LANES] = jnp.where(lane_lo, o2, o3).astype(BF16)

    def chunk_c(c):
        rows = slice(c * CHUNK, (c + 1) * CHUNK)
        xs = act_ref[rows, 0:C_WIDTH]
        bm = act_ref[rows, C_WIDTH:C_WIDTH + 2 * D_STATE]
        cm = act_ref[rows, C_WIDTH + 2 * D_STATE:CONV_DIM]
        gate = proj_ref[rows, OFF_Z:OFF_Z + C_WIDTH]
        dt_row = jax.nn.softplus(dtr_ref[rows, :].T[0:SSM_HEADS] + dtb_ref[...])
        cum_row = _dot_exact_rhs(dt_row * a_col, triu_b) * LOG2E
        cum = jnp.concatenate([cum_row, zpad], axis=0).T
        toend_row = jnp.exp2(cum_row[:, CHUNK - 1:CHUNK] - cum_row) * dt_row
        ecl = jnp.exp2(cum_row[:, CHUNK - 1:CHUNK])
        xt = xs.T
        yts = []
        for g in range(2):
            bg = bm[:, g * D_STATE:(g + 1) * D_STATE]
            cg = cm[:, g * D_STATE:(g + 1) * D_STATE]
            cb = _dot_nt(cg.astype(BF16), bg.astype(BF16))
            xw = []
            for hh in range(4):
                h = g * 4 + hh
                hs = slice(h * HEAD_DIM, (h + 1) * HEAD_DIM)
                cc = jnp.broadcast_to(cum[:, h:h + 1], (CHUNK, CHUNK))
                dec = jnp.exp2(jnp.where(causal, cc - cum_row[h:h + 1, :], NEG))
                m_h = cb * dec * dt_row[h:h + 1, :]
                ce_h = cg * jnp.exp2(cc)
                l_h = jnp.concatenate([m_h, ce_h], axis=1).astype(BF16)
                r_h = jnp.concatenate([xt[hs], h_ref[hs, :]], axis=1).astype(BF16)
                yts.append(_dot_nt(r_h, l_h))
                xw.append(xt[hs] * toend_row[h:h + 1, :])
            gs = slice(g * 4 * HEAD_DIM, (g + 1) * 4 * HEAD_DIM)
            upd = _dot(jnp.concatenate(xw, axis=0).astype(BF16), bg.astype(BF16))
            keep = jnp.concatenate(
                [jnp.broadcast_to(ecl[g * 4 + hh:g * 4 + hh + 1, :], (HEAD_DIM, D_STATE))
                 for hh in range(4)], axis=0)
            h_ref[gs, :] = h_ref[gs, :] * keep + upd
        y = jnp.concatenate(yts, axis=0).T
        out_c = _gated_rmsnorm(y, xs, gate, dskip_ref[...], gnw_ref[...])
        mix_ref[rows, A_WIDTH + B_WIDTH:D_MODEL] = out_c.astype(BF16)

    cph = half // CHUNK

    def chunks(hf):
        for c in range(hf * cph, (hf + 1) * cph):
            chunk_a(c)
            chunk_b(c)
            chunk_c(c)

    for hf in range(2):
        for piece in project_pieces(hf):
            piece()
    for hf in range(2):
        chunks(hf)
        for piece in out_pieces(hf):
            piece()
    for hf in range(2):
        for piece in norm_pieces(hf):
            piece()

    last = slice(tl - CHUNK, tl)
    kprev_ref[...] = proj_ref[last, OFF_K:OFF_K + KV_WIDTH]
    vprev_ref[...] = proj_ref[last, OFF_V:OFF_V + KV_WIDTH]
    tail = xbc_ref[SUBLANES + tl - (CONV_W - 1):SUBLANES + tl, :]
    xbc_ref[SUBLANES - (CONV_W - 1):SUBLANES, :] = tail

    @pl.when(t == nt - 1)
    def _final():
        klast_ref[0] = kprev_ref[...]
        vlast_ref[0] = vprev_ref[...]
        convnew_ref[0] = tail
        ssm_ref[0] = h_ref[...]


def _full(shape):
    n = len(shape)
    return pl.BlockSpec(shape, lambda *_: (0,) * n)


def _layer(shape, l, single_buffer=False):
    n = len(shape)
    mode = pl.Buffered(1) if single_buffer else None
    return pl.BlockSpec((None,) + tuple(shape), lambda *_: (l,) + (0,) * n, pipeline_mode=mode)


def _mixer_call(l, x, sinks, w_in, lnvg, lnvb, ws, bs_exp, convw, convb, dtb, alog, dskip, gnw,
                w_out, ln1g, ln1b, *, tl):
    bsz, seq, _ = x.shape
    nt = seq // tl
    kern = functools.partial(_mixer_kernel, tl=tl, layer=l)
    in_specs = [
        pl.BlockSpec(memory_space=pltpu.SMEM),
        pl.BlockSpec((1, tl, D_MODEL), lambda b, t: (b, t, 0)),
        _layer((D_MODEL, D_IN_PAD), l, single_buffer=True),
        _layer((1, A_WIDTH), l), _layer((1, A_WIDTH), l),
        _layer((4, CHUNK, CHUNK), l), _layer((CHUNK, A_WIDTH), l),
        _layer((CONV_W, CONV_DIM), l), _layer((1, CONV_DIM), l),
        _layer((SSM_HEADS, 1), l), _layer((SSM_HEADS, 1), l),
        _layer((1, C_WIDTH), l), _layer((1, C_WIDTH), l),
        _layer((D_MODEL, D_MODEL), l, single_buffer=True),
        _layer((1, D_MODEL), l), _layer((1, D_MODEL), l),
    ]
    out_shape = (
        jax.ShapeDtypeStruct((bsz, seq, D_MODEL), F32),
        jax.ShapeDtypeStruct((bsz, CHUNK, KV_WIDTH), F32),
        jax.ShapeDtypeStruct((bsz, CHUNK, KV_WIDTH), F32),
        jax.ShapeDtypeStruct((bsz, CONV_W - 1, CONV_DIM), F32),
        jax.ShapeDtypeStruct((bsz, C_WIDTH, D_STATE), F32),
    )
    out_specs = (
        pl.BlockSpec((1, tl, D_MODEL), lambda b, t: (b, t, 0)),
        pl.BlockSpec((1, CHUNK, KV_WIDTH), lambda b, t: (b, 0, 0)),
        pl.BlockSpec((1, CHUNK, KV_WIDTH), lambda b, t: (b, 0, 0)),
        pl.BlockSpec((1, CONV_W - 1, CONV_DIM), lambda b, t: (b, 0, 0)),
        pl.BlockSpec((1, C_WIDTH, D_STATE), lambda b, t: (b, 0, 0)),
    )
    scratch = [
        pltpu.VMEM((tl, OFF_XBC), F32),
        pltpu.VMEM((SUBLANES + tl, CONV_DIM), F32),
        pltpu.VMEM((tl, CONV_DIM), F32),
        pltpu.VMEM((tl, LANES), F32),
        pltpu.VMEM((tl, D_MODEL), BF16),
        pltpu.VMEM((CHUNK, KV_WIDTH), F32),
        pltpu.VMEM((CHUNK, KV_WIDTH), F32),
        pltpu.VMEM((C_WIDTH, D_STATE), F32),
        pltpu.VMEM((4 * CHUNK, CHUNK), BF16),
        pltpu.VMEM((tl, D_MODEL), BF16),
        pltpu.VMEM((tl, D_MODEL), F32),
    ]
    return pl.pallas_call(
        kern, out_shape=out_shape, grid=(bsz, nt), in_specs=in_specs, out_specs=out_specs,
        scratch_shapes=scratch, name="prompt_mixer",
        compiler_params=pltpu.CompilerParams(
            dimension_semantics=("arbitrary", "arbitrary"), vmem_limit_bytes=VMEM_LIMIT),
    )(sinks, x, w_in, lnvg, lnvb, ws, bs_exp, convw, convb, dtb, alog, dskip, gnw, w_out, ln1g, ln1b)


FF_CHUNK = 1024


def _ffn(x, w1_ref, w2_ref, g_ref, b_ref):
    xb = x.astype(BF16)
    acc = jnp.zeros(x.shape, F32)
    for c in range(D_FF // FF_CHUNK):
        sl = slice(c * FF_CHUNK, (c + 1) * FF_CHUNK)
        h = jnp.maximum(_dot(xb, w1_ref[:, sl]), 0.0)
        acc = acc + _dot((h * h).astype(BF16), w2_ref[sl, :])
    return _layer_norm(ALPHA * x + acc, g_ref[...], b_ref[...])


def _ffn_kernel(x_ref, w1_ref, w2_ref, g_ref, b_ref, o_ref):
    o_ref[...] = _ffn(x_ref[...], w1_ref, w2_ref, g_ref, b_ref)


def _ffn_call(l, x2d, w1, w2, g, b, *, tm):
    rows = x2d.shape[0]
    return pl.pallas_call(
        _ffn_kernel, out_shape=jax.ShapeDtypeStruct((rows, D_MODEL), F32),
        grid=(rows // tm,),
        in_specs=[pl.BlockSpec((tm, D_MODEL), lambda i: (i, 0)),
                  _layer((D_MODEL, D_FF), l, single_buffer=True),
                  _layer((D_FF, D_MODEL), l, single_buffer=True),
                  _layer((1, D_MODEL), l), _layer((1, D_MODEL), l)],
        out_specs=pl.BlockSpec((tm, D_MODEL), lambda i: (i, 0)),
        name="ffn",
        compiler_params=pltpu.CompilerParams(
            dimension_semantics=("arbitrary",), vmem_limit_bytes=VMEM_LIMIT),
    )(x2d, w1, w2, g, b)


def _s1_kernel(x_ref, w_in_ref, lnvg_ref, lnvb_ref, ws0_ref, bs0_ref, convw_ref, convb_ref,
               cst_ref, dtb_ref, alog_ref, e_ref,
               vn_ref, outa_ref, q_ref, k_ref, v_ref, z_ref, cnew_ref, xs_ref, bm_ref,
               xT_ref, cT_ref, dec_ref):
    xb = x_ref[...].astype(BF16)
    proj = _dot(xb, w_in_ref[...])
    u = jax.nn.gelu(proj[:, OFF_AU:OFF_AU + A_WIDTH])
    vn = _layer_norm(jax.nn.gelu(proj[:, OFF_AV:OFF_AV + A_WIDTH]), lnvg_ref[...], lnvb_ref[...])
    vn_ref[...] = vn
    outa_ref[...] = u * (ws0_ref[...] * vn + bs0_ref[...])
    q_ref[...] = proj[:, OFF_Q:OFF_Q + B_WIDTH]
    k_ref[...] = proj[:, OFF_K:OFF_K + KV_WIDTH]
    v_ref[...] = proj[:, OFF_V:OFF_V + KV_WIDTH]
    z_ref[...] = proj[:, OFF_Z:OFF_Z + C_WIDTH]
    xbc = proj[:, OFF_XBC:OFF_DT]
    acc = convb_ref[...] + xbc * convw_ref[CONV_W - 1:CONV_W, :]
    for i in range(CONV_W - 1):
        acc = acc + cst_ref[i] * convw_ref[i:i + 1, :]
    cnew_ref[0] = cst_ref[1]
    cnew_ref[1] = cst_ref[2]
    cnew_ref[2] = xbc
    act = _silu(acc)
    xs = act[:, 0:C_WIDTH]
    xs_ref[...] = xs
    bm_ref[...] = act[:, C_WIDTH:C_WIDTH + 2 * D_STATE]
    cT_ref[...] = act[:, C_WIDTH + 2 * D_STATE:CONV_DIM].T
    dt = jax.nn.softplus(proj[:, OFF_DT:D_IN_PAD] + dtb_ref[...])
    dec_ref[...] = jnp.exp(dt * (-jnp.exp(alog_ref[...])))
    dt_exp = _dot_exact_rhs(dt, e_ref[...])
    xT_ref[...] = (xs * dt_exp).T


S2_TB = 2 * SUBLANES


def _s2_kernel(sinks_ref, q_ref, kn_ref, vn_ref, kst_ref, vst_ref, h_ref, xT_ref, bm_ref,
               cT_ref, dec_ref, ob_ref, hnew_ref, yT_ref, *, layer, step):
    i = step
    tb = S2_TB
    lane = lax.broadcasted_iota(jnp.int32, (tb, LANES), 1)
    lo = lane < HEAD_DIM

    q01 = q_ref[:, 0:LANES]
    q23 = q_ref[:, LANES:2 * LANES]
    qh = jnp.concatenate([
        jnp.where(lo, q01, 0.0),
        jnp.where(lo, pltpu.roll(q01, HEAD_DIM, 1), 0.0),
        jnp.where(lo, 0.0, pltpu.roll(q23, HEAD_DIM, 1)),
        jnp.where(lo, 0.0, q23)], axis=0)
    qhb = qh.astype(BF16)
    rowi = lax.broadcasted_iota(jnp.int32, (4 * tb, LANES), 0)
    coli = lax.broadcasted_iota(jnp.int32, (4 * tb, LANES), 1)
    tok = jnp.bitwise_and(rowi, tb - 1)
    s = jnp.zeros((4 * tb, LANES), F32)
    for j in range(tb):
        sj = _dot_nt(qhb, kst_ref[j])
        s = jnp.where(tok == j, sj, s)
    s = jnp.where(coli >= 1, s * ATT_SCALE, NEG)
    kn4 = jnp.concatenate([kn_ref[...]] * 4, axis=0)
    vn4 = jnp.concatenate([vn_ref[...]] * 4, axis=0)
    s_new = jnp.sum(qh * kn4, axis=-1, keepdims=True) * ATT_SCALE
    head = jnp.right_shift(rowi[:, 0:1], tb.bit_length() - 1)
    sink = jnp.where(head == 0, sinks_ref[layer, 0],
                     jnp.where(head == 1, sinks_ref[layer, 1],
                               jnp.where(head == 2, sinks_ref[layer, 2], sinks_ref[layer, 3])))
    m = jnp.maximum(jnp.maximum(jnp.max(s, axis=-1, keepdims=True), s_new), sink)
    p = jnp.exp(s - m)
    p_new = jnp.exp(s_new - m)
    inv = 1.0 / (jnp.sum(p, axis=-1, keepdims=True) + p_new + jnp.exp(sink - m))
    p = p * inv
    o = (p_new * inv) * vn4
    for j in range(tb):
        pj = jnp.where(tok == j, p, 0.0).astype(BF16)
        o = o + _dot(pj, vst_ref[j])
    ob_ref[:, 0:LANES] = jnp.where(lo, o[0:tb], pltpu.roll(o[tb:2 * tb], HEAD_DIM, 1))
    ob_ref[:, LANES:2 * LANES] = jnp.where(lo, pltpu.roll(o[2 * tb:3 * tb], HEAD_DIM, 1), o[3 * tb:])

    @pl.when(i == 0)
    def _():
        yT_ref[...] = jnp.zeros_like(yT_ref)

    r8 = lax.broadcasted_iota(jnp.int32, (tb, D_STATE), 0)
    c128 = lax.broadcasted_iota(jnp.int32, (LANES, LANES), 1)
    hp_g = C_WIDTH // 2
    b0 = pl.multiple_of(i * tb, tb)
    zrows = jnp.zeros((LANES - tb, tb * D_STATE), BF16)
    to_front = jnp.bitwise_and(LANES - b0, LANES - 1)
    for g in range(2):
        gs = slice(g * hp_g, (g + 1) * hp_g)
        ns = slice(g * D_STATE, (g + 1) * D_STATE)
        bblk = bm_ref[pl.ds(b0, tb), ns]
        bdiag = jnp.concatenate([jnp.where(r8 == j, bblk, 0.0) for j in range(tb)], axis=1)
        rhs = jnp.concatenate([bdiag.astype(BF16), zrows], axis=0)
        xg = pltpu.roll(xT_ref[gs, :], to_front, 1).astype(BF16)
        upd = _dot(xg, rhs)
        hns = []
        cms = []
        for j in range(tb):
            drow = dec_ref[pl.ds(b0 + j, 1), :]
            parts = []
            for hh in range(4):
                h = g * 4 + hh
                rs = slice(h * HEAD_DIM, (h + 1) * HEAD_DIM)
                hn = (h_ref[j, rs, :] * jnp.broadcast_to(drow[:, h:h + 1], (HEAD_DIM, D_STATE))
                      + upd[hh * HEAD_DIM:(hh + 1) * HEAD_DIM, j * D_STATE:(j + 1) * D_STATE])
                hnew_ref[j, rs, :] = hn
                parts.append(hn)
            hns.append(jnp.concatenate(parts, axis=0).astype(BF16))
            cms.append(jnp.where(c128 == b0 + j, cT_ref[ns, :], 0.0).astype(BF16))
        yT_ref[gs, :] += _dot(jnp.concatenate(hns, axis=1), jnp.concatenate(cms, axis=0))


def _s3_kernel(yT_ref, xs_ref, z_ref, dskip_ref, gnw_ref, outa_ref, outb_ref, x_ref, w_out_ref,
               g_ref, b_ref, x1_ref):
    out_c = _gated_rmsnorm(yT_ref[...].T, xs_ref[...], _silu(z_ref[...]), dskip_ref[...], gnw_ref[...])
    mix = jnp.concatenate([outa_ref[...], outb_ref[...], out_c], axis=1).astype(BF16)
    x1_ref[...] = _layer_norm(ALPHA * x_ref[...] + _dot(mix, w_out_ref[...]), g_ref[...], b_ref[...])


def _sample_kernel(sinks_ref, x_ref, w_in_ref, lnvg_ref, lnvb_ref, ws0_ref, bs0_ref, convw_ref,
                   convb_ref, cst_ref, dtb_ref, alog_ref, e_ref, kst_ref, vst_ref, h_ref,
                   dskip_ref, gnw_ref, w_out_ref, ln1g_ref, ln1b_ref, w1_ref, w2_ref, ln2g_ref, ln2b_ref,
                   ys_ref, vns_ref, ksm_ref, vsm_ref, csm_ref, hnew_ref,
                   res_scr, outa_scr, q_scr, z_scr, xs_scr, bm_scr, xT_scr, cT_scr, dec_scr,
                   outb_scr, yT_scr, x1_scr):
    l = pl.program_id(0)
    i = pl.program_id(1)
    last = pl.num_programs(1) - 1

    @pl.when(jnp.logical_and(l == 0, i == 0))
    def _():
        res_scr[...] = x_ref[...]

    @pl.when(i == 0)
    def _():
        _s1_kernel(res_scr, w_in_ref, lnvg_ref, lnvb_ref, ws0_ref, bs0_ref, convw_ref, convb_ref,
                   cst_ref, dtb_ref, alog_ref, e_ref,
                   vns_ref, outa_scr, q_scr, ksm_ref, vsm_ref, z_scr, csm_ref, xs_scr, bm_scr,
                   xT_scr, cT_scr, dec_scr)

    rows = pl.ds(pl.multiple_of(i * S2_TB, S2_TB), S2_TB)
    _s2_kernel(sinks_ref, q_scr.at[rows], ksm_ref.at[rows], vsm_ref.at[rows], kst_ref, vst_ref,
               h_ref, xT_scr, bm_scr, cT_scr, dec_scr, outb_scr.at[rows], hnew_ref, yT_scr,
               layer=l, step=i)

    @pl.when(i == last)
    def _():
        _s3_kernel(yT_scr, xs_scr, z_scr, dskip_ref, gnw_ref, outa_scr, outb_scr, res_scr,
                   w_out_ref, ln1g_ref, ln1b_ref, x1_scr)
        res_scr[...] = _ffn(x1_scr[...], w1_ref, w2_ref, ln2g_ref, ln2b_ref)
        ys_ref[...] = res_scr[...]


def _sample_call(x, sinks, w_in, lnvg, lnvb, ws0, bs0, convw, convb, cst, dtb, alog, emat,
                 kst, vst, hst, dskip, gnw, w_out, ln1g, ln1b, w1, w2, ln2g, ln2b):
    n = x.shape[0]
    tb = S2_TB

    def per_layer(shape, single_buffer=False):
        k = len(shape)
        mode = pl.Buffered(1) if single_buffer else None
        return pl.BlockSpec((None,) + tuple(shape), lambda l, i: (l,) + (0,) * k, pipeline_mode=mode)

    def per_block(shape):
        k = len(shape) - 1
        return pl.BlockSpec((None,) + tuple(shape), lambda l, i: (l, i) + (0,) * k)

    in_specs = [
        pl.BlockSpec(memory_space=pltpu.SMEM),
        _full((n, D_MODEL)),
        per_layer((D_MODEL, D_IN_PAD), True),
        per_layer((1, A_WIDTH)), per_layer((1, A_WIDTH)), per_layer((1, A_WIDTH)), per_layer((1, A_WIDTH)),
        per_layer((CONV_W, CONV_DIM)), per_layer((1, CONV_DIM)), per_layer((CONV_W - 1, n, CONV_DIM)),
        per_layer((1, LANES)), per_layer((1, LANES)), _full((LANES, C_WIDTH)),
        per_block((tb, CHUNK, KV_WIDTH)), per_block((tb, CHUNK, KV_WIDTH)),
        per_block((tb, C_WIDTH, D_STATE)),
        per_layer((1, C_WIDTH)), per_layer((1, C_WIDTH)),
        per_layer((D_MODEL, D_MODEL), True), per_layer((1, D_MODEL)), per_layer((1, D_MODEL)),
        per_layer((D_MODEL, D_FF), True), per_layer((D_FF, D_MODEL), True),
        per_layer((1, D_MODEL)), per_layer((1, D_MODEL)),
    ]
    out_shape = (
        jax.ShapeDtypeStruct((n, D_MODEL), F32),
        jax.ShapeDtypeStruct((DEPTH, n, A_WIDTH), F32),
        jax.ShapeDtypeStruct((DEPTH, n, KV_WIDTH), F32),
        jax.ShapeDtypeStruct((DEPTH, n, KV_WIDTH), F32),
        jax.ShapeDtypeStruct((DEPTH, CONV_W - 1, n, CONV_DIM), F32),
        jax.ShapeDtypeStruct((DEPTH, n, C_WIDTH, D_STATE), F32),
    )
    out_specs = (
        _full((n, D_MODEL)),
        per_layer((n, A_WIDTH)), per_layer((n, KV_WIDTH)), per_layer((n, KV_WIDTH)),
        per_layer((CONV_W - 1, n, CONV_DIM)),
        per_block((tb, C_WIDTH, D_STATE)),
    )
    scratch = [
        pltpu.VMEM((n, D_MODEL), F32),
        pltpu.VMEM((n, A_WIDTH), F32),
        pltpu.VMEM((n, B_WIDTH), F32),
        pltpu.VMEM((n, C_WIDTH), F32),
        pltpu.VMEM((n, C_WIDTH), F32),
        pltpu.VMEM((n, 2 * D_STATE), F32),
        pltpu.VMEM((C_WIDTH, n), F32),
        pltpu.VMEM((2 * D_STATE, n), F32),
        pltpu.VMEM((n, LANES), F32),
        pltpu.VMEM((n, B_WIDTH), F32),
        pltpu.VMEM((C_WIDTH, n), F32),
        pltpu.VMEM((n, D_MODEL), F32),
    ]
    return pl.pallas_call(
        _sample_kernel, out_shape=out_shape, grid=(DEPTH, n // tb), in_specs=in_specs,
        out_specs=out_specs, scratch_shapes=scratch, name="sample_step",
        compiler_params=pltpu.CompilerParams(
            dimension_semantics=("arbitrary", "arbitrary"), vmem_limit_bytes=VMEM_LIMIT),
    )(sinks, x, w_in, lnvg, lnvb, ws0, bs0, convw, convb, cst, dtb, alog, emat, kst, vst, hst,
      dskip, gnw, w_out, ln1g, ln1b, w1, w2, ln2g, ln2b)


PROMPT_TL = 1024
PROMPT_TM = 1024


def kernel(x_prompt, x_sample, state_attn_k, state_attn_v, state_conv, state_ssm, w_in, ln_v_g,
           ln_v_b, w_s, b_s, sinks, conv_w, conv_b, dt_bias, a_log, d_skip, gn_w, w_out, ln1_g,
           ln1_b, w1, w2, ln2_g, ln2_b):
    bsz, seq, _ = x_prompt.shape
    n_s = x_sample.shape[0]
    d_in = w_in.shape[-1]

    w_in_b = jnp.pad(w_in, ((0, 0), (0, 0), (0, D_IN_PAD - d_in))).astype(BF16)
    w_out_b = w_out.astype(BF16)
    w1_b = w1.astype(BF16)
    w2_b = w2.astype(BF16)
    pad_h = ((0, 0), (0, LANES - SSM_HEADS))
    dtb_p = jnp.pad(dt_bias, pad_h)[:, None, :]
    alog_p = jnp.pad(a_log, pad_h)[:, None, :]
    dskip_e = jnp.repeat(d_skip, HEAD_DIM, axis=-1)[:, None, :]
    bs_e = jnp.repeat(jnp.swapaxes(b_s, 1, 2), HEAD_DIM, axis=-1)
    ws0_e = jnp.repeat(w_s[:, :, 0, 0], HEAD_DIM, axis=-1)[:, None, :]
    bs0_e = jnp.repeat(b_s[:, :, 0], HEAD_DIM, axis=-1)[:, None, :]
    emat = (lax.broadcasted_iota(jnp.int32, (LANES, C_WIDTH), 0)
            == lax.broadcasted_iota(jnp.int32, (LANES, C_WIDTH), 1) // HEAD_DIM).astype(BF16)
    row = lambda a: a[:, None, :]
    lnvg, lnvb, convb, gnw = row(ln_v_g), row(ln_v_b), row(conv_b), row(gn_w)
    ln1g, ln1b, ln2g, ln2b = row(ln1_g), row(ln1_b), row(ln2_g), row(ln2_b)
    kst = state_attn_k.reshape(DEPTH, n_s, CHUNK, KV_WIDTH).astype(BF16)
    vst = state_attn_v.reshape(DEPTH, n_s, CHUNK, KV_WIDTH).astype(BF16)
    hst = state_ssm.reshape(DEPTH, n_s, C_WIDTH, D_STATE)
    cst = jnp.swapaxes(state_conv, 1, 2)

    ys, vns, ksm, vsm, csm, hsm = _sample_call(
        x_sample.reshape(n_s, D_MODEL), sinks, w_in_b, lnvg, lnvb, ws0_e, bs0_e, conv_w, convb, cst,
        dtb_p, alog_p, emat, kst, vst, hst, dskip_e, gnw, w_out_b, ln1g, ln1b, w1_b, w2_b, ln2g, ln2b)

    yp = x_prompt
    kp, vp, cp, hp = [], [], [], []
    for l in range(DEPTH):
        x1, k_l, v_l, c_l, h_l = _mixer_call(
            l, yp, sinks, w_in_b, lnvg, lnvb, w_s, bs_e, conv_w, convb,
            dt_bias[:, :, None], a_log[:, :, None], dskip_e, gnw, w_out_b, ln1g, ln1b, tl=PROMPT_TL)
        yp = _ffn_call(l, x1.reshape(bsz * seq, D_MODEL), w1_b, w2_b, ln2g, ln2b,
                       tm=PROMPT_TM).reshape(bsz, seq, D_MODEL)
        kp.append(k_l); vp.append(v_l); cp.append(c_l); hp.append(h_l)

    kv_p = (DEPTH, bsz, CHUNK, 2, HEAD_DIM)
    kv_s = (DEPTH, n_s, 1, 2, HEAD_DIM)
    ssm_shape = (SSM_HEADS, HEAD_DIM, D_STATE)
    return (yp, ys.reshape(n_s, 1, D_MODEL),
            jnp.stack(kp).reshape(kv_p), jnp.stack(vp).reshape(kv_p),
            jnp.stack(cp), jnp.stack(hp).reshape((DEPTH, bsz) + ssm_shape),
            ksm.reshape(kv_s), vsm.reshape(kv_s),
            jnp.swapaxes(csm, 1, 2),
            hsm.reshape((DEPTH, n_s) + ssm_shape),
            vns.reshape(DEPTH, n_s, 1, A_WIDTH))
```

```python
import functools

import jax
import jax.numpy as jnp
from jax import lax
from jax.experimental import pallas as pl
from jax.experimental.pallas import tpu as pltpu

F32 = jnp.float32
BF16 = jnp.bfloat16

D_MODEL = 1024
DEPTH = 4
HEAD_DIM = 64
A_WIDTH = 256
B_WIDTH = 256
KV_WIDTH = 128
C_WIDTH = 512
CONV_DIM = 1024
CONV_W = 4
SSM_HEADS = 8
D_STATE = 128
D_FF = 4096
CHUNK = 128
ALPHA = (2 * DEPTH) ** 0.25
LN_EPS = 1e-5
RMS_EPS = 1e-6
ATT_SCALE = HEAD_DIM ** -0.5
LOG2E = 1.4426950408889634
Q_SCALE = ATT_SCALE * LOG2E
NEG = -1e30

OFF_AU, OFF_AV, OFF_Q, OFF_K, OFF_V, OFF_Z = 0, 256, 512, 768, 896, 1024
OFF_XBC = 1536
OFF_DT = 2560
D_IN_PAD = 2688
LANES = 128
SUBLANES = 8
VMEM_LIMIT = 56 * 1024 * 1024

NT_DIMS = (((1,), (1,)), ((), ()))
PIECE = 256


def _dot(a, b):
    return jnp.dot(a, b, preferred_element_type=F32)


def _dot_nt(a, b):
    return lax.dot_general(a, b, NT_DIMS, preferred_element_type=F32)


def _layer_norm(x, g, b):
    mu = jnp.mean(x, axis=-1, keepdims=True)
    xc = x - mu
    var = jnp.mean(xc * xc, axis=-1, keepdims=True)
    return xc * lax.rsqrt(var + LN_EPS) * g + b


def _silu(x):
    return x * jax.nn.sigmoid(x)


def _split3(x):
    hi = x.astype(BF16)
    r = x - hi.astype(F32)
    mid = r.astype(BF16)
    lo = (r - mid.astype(F32)).astype(BF16)
    return hi, mid, lo


def _dot_exact_rhs(x, m_bf16):
    hi, mid, lo = _split3(x)
    return _dot(hi, m_bf16) + _dot(mid, m_bf16) + _dot(lo, m_bf16)


def _gated_rmsnorm(y, xs, gate, dskip, gnw):
    y2 = (y + dskip * xs) * gate
    ms = jnp.mean(y2 * y2, axis=-1, keepdims=True)
    return y2 * lax.rsqrt(ms + RMS_EPS) * gnw


def _mixer_kernel(sinks_ref, x_ref, w_in_ref, lnvg_ref, lnvb_ref, ws_ref, bs_ref,
                  convw_ref, convb_ref, dtb_ref, alog_ref, dskip_ref, gnw_ref,
                  w_out_ref, ln1g_ref, ln1b_ref,
                  x1_ref, klast_ref, vlast_ref, convnew_ref, ssm_ref,
                  proj_ref, xbc_ref, act_ref, dtr_ref, mix_ref, kprev_ref, vprev_ref,
                  h_ref, wstk_ref, xb_ref, mo_ref, *, tl, layer):
    t = pl.program_id(1)
    nt = pl.num_programs(1)

    row_i = lax.broadcasted_iota(jnp.int32, (CHUNK, CHUNK), 0)
    col_i = lax.broadcasted_iota(jnp.int32, (CHUNK, CHUNK), 1)
    causal = col_i <= row_i
    lane_lo = col_i < HEAD_DIM

    @pl.when(t == 0)
    def _init():
        kprev_ref[...] = jnp.zeros_like(kprev_ref)
        vprev_ref[...] = jnp.zeros_like(vprev_ref)
        h_ref[...] = jnp.zeros_like(h_ref)
        xbc_ref[0:SUBLANES, :] = jnp.zeros((SUBLANES, CONV_DIM), F32)
        for h in range(4):
            wstk_ref[h * CHUNK:(h + 1) * CHUNK, :] = jnp.where(causal, ws_ref[h], 0.0).astype(BF16)

    half = tl // 2

    def project_pieces(hf):
        rows = slice(hf * half, (hf + 1) * half)

        def cast():
            xb_ref[rows, :] = x_ref[0, rows, :].astype(BF16)

        epilogue = {
            OFF_AU: jax.nn.gelu,
            OFF_AV: lambda r: _layer_norm(jax.nn.gelu(r), lnvg_ref[...], lnvb_ref[...]),
            OFF_Q: lambda r: r * Q_SCALE,
            OFF_Z: _silu,
            OFF_Z + PIECE: _silu,
        }

        def main(c0):
            def run():
                res = _dot(xb_ref[rows, :], w_in_ref[:, c0:c0 + PIECE])
                proj_ref[rows, c0:c0 + PIECE] = epilogue.get(c0, lambda r: r)(res)
            return run

        def xbc(c0):
            def run():
                cols = slice(c0, c0 + PIECE)
                res = _dot(xb_ref[rows, :], w_in_ref[:, OFF_XBC + c0:OFF_XBC + c0 + PIECE])
                xh = jnp.concatenate([xbc_ref[hf * half:hf * half + SUBLANES, cols], res], axis=0)
                xbc_ref[SUBLANES + hf * half:SUBLANES + (hf + 1) * half, cols] = res
                acc = xh * convw_ref[0:1, cols]
                for i in range(1, CONV_W):
                    acc = pltpu.roll(acc, 1, 0) + xh * convw_ref[i:i + 1, cols]
                act_ref[rows, cols] = _silu(acc[SUBLANES:] + convb_ref[:, cols])
            return run

        def dt():
            dtr_ref[rows, :] = _dot(xb_ref[rows, :], w_in_ref[:, OFF_DT:D_IN_PAD])

        return ([cast] + [main(c0) for c0 in range(0, OFF_XBC, PIECE)]
                + [xbc(c0) for c0 in range(0, CONV_DIM, PIECE)] + [dt])

    def out_pieces(hf):
        rows = slice(0, tl) if hf is None else slice(hf * half, (hf + 1) * half)

        def part(c0):
            def run():
                mo_ref[rows, c0:c0 + PIECE] = _dot(mix_ref[rows, :], w_out_ref[:, c0:c0 + PIECE])
            return run
        return [part(c0) for c0 in range(0, D_MODEL, PIECE)]

    def norm_pieces(hf):
        def piece(r0):
            def run():
                rows = slice(r0, r0 + CHUNK)
                x1_ref[0, rows, :] = _layer_norm(ALPHA * x_ref[0, rows, :] + mo_ref[rows, :],
                                                 ln1g_ref[...], ln1b_ref[...])
            return run
        return [piece(r0) for r0 in range(hf * half, (hf + 1) * half, CHUNK)]

    triu_b = jnp.where(row_i <= col_i, 1.0, 0.0).astype(BF16)
    a_col = -jnp.exp(alog_ref[...])
    zpad = jnp.zeros((CHUNK - SSM_HEADS, CHUNK), F32)
    lane256 = lax.broadcasted_iota(jnp.int32, (CHUNK, A_WIDTH), 1)
    r256 = lax.broadcasted_iota(jnp.int32, (2 * CHUNK, 2 * CHUNK), 0)
    c256 = lax.broadcasted_iota(jnp.int32, (2 * CHUNK, 2 * CHUNK), 1)
    t256 = jnp.bitwise_and(r256, CHUNK - 1)
    win_mask = (c256 > t256) & (c256 <= t256 + CHUNK)
    rcol = lax.broadcasted_iota(jnp.int32, (2 * CHUNK, 1), 0)
    sink_a = jnp.where(rcol < CHUNK, sinks_ref[layer, 0], sinks_ref[layer, 3]) * LOG2E
    sink_b = jnp.where(rcol < CHUNK, sinks_ref[layer, 1], sinks_ref[layer, 2]) * LOG2E

    def chunk_a(c):
        rows = slice(c * CHUNK, (c + 1) * CHUNK)
        u = proj_ref[rows, OFF_AU:OFF_AU + A_WIDTH]
        vn = proj_ref[rows, OFF_AV:OFF_AV + A_WIDTH]
        pa = _dot(wstk_ref[...], vn.astype(BF16))
        mix_a = jnp.where(
            lane256 < HEAD_DIM, pa[0:CHUNK],
            jnp.where(lane256 < 2 * HEAD_DIM, pa[CHUNK:2 * CHUNK],
                      jnp.where(lane256 < 3 * HEAD_DIM, pa[2 * CHUNK:3 * CHUNK], pa[3 * CHUNK:])))
        mix_ref[rows, 0:A_WIDTH] = (u * (mix_a + bs_ref[...])).astype(BF16)

    def chunk_b(c):
        rows = slice(c * CHUNK, (c + 1) * CHUNK)
        q01 = proj_ref[rows, OFF_Q:OFF_Q + LANES]
        q23 = proj_ref[rows, OFF_Q + LANES:OFF_Q + 2 * LANES]
        kc = proj_ref[rows, OFF_K:OFF_K + KV_WIDTH]
        vc = proj_ref[rows, OFF_V:OFF_V + KV_WIDTH]
        if c == 0:
            kp, vp = kprev_ref[...], vprev_ref[...]
        else:
            prev = slice((c - 1) * CHUNK, c * CHUNK)
            kp = proj_ref[prev, OFF_K:OFF_K + KV_WIDTH]
            vp = proj_ref[prev, OFF_V:OFF_V + KV_WIDTH]
        k2 = jnp.concatenate([kp, kc], axis=0)
        v2 = jnp.concatenate([vp, vc], axis=0)
        k2b = k2.astype(BF16)
        v2b = v2.astype(BF16)
        k2rb = pltpu.roll(k2, HEAD_DIM, 1).astype(BF16)
        v2rb = pltpu.roll(v2, HEAD_DIM, 1).astype(BF16)
        q_a = jnp.concatenate([jnp.where(lane_lo, q01, 0.0), jnp.where(lane_lo, 0.0, q23)], axis=0)
        q_b = jnp.concatenate([jnp.where(lane_lo, 0.0, q01), jnp.where(lane_lo, q23, 0.0)], axis=0)
        vis = win_mask & (c256 >= jnp.where(t == 0, CHUNK, 0)) if c == 0 else win_mask

        def attend(qm, kk, vv, sink2):
            s = jnp.where(vis, _dot_nt(qm.astype(BF16), kk), NEG)
            m = jnp.maximum(jnp.max(s, axis=-1, keepdims=True), sink2)
            p = jnp.exp2(s - m)
            den = jnp.sum(p, axis=-1, keepdims=True) + jnp.exp2(sink2 - m)
            return _dot(p.astype(BF16), vv) * (1.0 / den)

        o_a = attend(q_a, k2b, v2b, sink_a)
        o_b = attend(q_b, k2rb, v2rb, sink_b)
        mix_ref[rows, A_WIDTH:A_WIDTH + LANES] = jnp.where(
            lane_lo, o_a[0:CHUNK], o_b[0:CHUNK]).astype(BF16)
        mix_ref[rows, A_WIDTH + LANES:A_WIDTH + 2 * LANES] = jnp.where(
            lane_lo, o_b[CHUNK:], o_a[CHUNK:]).astype(BF16)

    def chunk_c(c):
        rows = slice(c * CHUNK, (c + 1) * CHUNK)
        xs = act_ref[rows, 0:C_WIDTH]
        bm = act_ref[rows, C_WIDTH:C_WIDTH + 2 * D_STATE]
        cm = act_ref[rows, C_WIDTH + 2 * D_STATE:CONV_DIM]
        gate = proj_ref[rows, OFF_Z:OFF_Z + C_WIDTH]
        dt_row = jax.nn.softplus(dtr_ref[rows, :].T[0:SSM_HEADS] + dtb_ref[...])
        cum_row = _dot_exact_rhs(dt_row * a_col, triu_b) * LOG2E
        cum = jnp.concatenate([cum_row, zpad], axis=0).T
        toend_row = jnp.exp2(cum_row[:, CHUNK - 1:CHUNK] - cum_row) * dt_row
        ecl = jnp.exp2(cum_row[:, CHUNK - 1:CHUNK])
        xt = xs.T
        yts = []
        for g in range(2):
            bg = bm[:, g * D_STATE:(g + 1) * D_STATE]
            cg = cm[:, g * D_STATE:(g + 1) * D_STATE]
            cb = _dot_nt(cg.astype(BF16), bg.astype(BF16))
            xw = []
            for hh in range(4):
                h = g * 4 + hh
                hs = slice(h * HEAD_DIM, (h + 1) * HEAD_DIM)
                cc = jnp.broadcast_to(cum[:, h:h + 1], (CHUNK, CHUNK))
                dec = jnp.exp2(jnp.where(causal, cc - cum_row[h:h + 1, :], NEG))
                m_h = cb * dec * dt_row[h:h + 1, :]
                ce_h = cg * jnp.exp2(cc)
                l_h = jnp.concatenate([m_h, ce_h], axis=1).astype(BF16)
                r_h = jnp.concatenate([xt[hs], h_ref[hs, :]], axis=1).astype(BF16)
                yts.append(_dot_nt(r_h, l_h))
                xw.append(xt[hs] * toend_row[h:h + 1, :])
            gs = slice(g * 4 * HEAD_DIM, (g + 1) * 4 * HEAD_DIM)
            upd = _dot(jnp.concatenate(xw, axis=0).astype(BF16), bg.astype(BF16))
            keep = jnp.concatenate(
                [jnp.broadcast_to(ecl[g * 4 + hh:g * 4 + hh + 1, :], (HEAD_DIM, D_STATE))
                 for hh in range(4)], axis=0)
            h_ref[gs, :] = h_ref[gs, :] * keep + upd
        y = jnp.concatenate(yts, axis=0).T
        out_c = _gated_rmsnorm(y, xs, gate, dskip_ref[...], gnw_ref[...])
        mix_ref[rows, A_WIDTH + B_WIDTH:D_MODEL] = out_c.astype(BF16)

    cph = half // CHUNK

    def chunks(hf):
        for c in range(hf * cph, (hf + 1) * cph):
            chunk_a(c)
            chunk_b(c)
            chunk_c(c)

    for hf in range(2):
        for piece in project_pieces(hf):
            piece()
    for hf in range(2):
        chunks(hf)
    for piece in out_pieces(None):
        piece()
    for hf in range(2):
        for piece in norm_pieces(hf):
            piece()

    last = slice(tl - CHUNK, tl)
    kprev_ref[...] = proj_ref[last, OFF_K:OFF_K + KV_WIDTH]
    vprev_ref[...] = proj_ref[last, OFF_V:OFF_V + KV_WIDTH]
    tail = xbc_ref[SUBLANES + tl - (CONV_W - 1):SUBLANES + tl, :]
    xbc_ref[SUBLANES - (CONV_W - 1):SUBLANES, :] = tail

    @pl.when(t == nt - 1)
    def _final():
        klast_ref[0] = kprev_ref[...]
        vlast_ref[0] = vprev_ref[...]
        convnew_ref[0] = tail
        ssm_ref[0] = h_ref[...]


def _full(shape):
    n = len(shape)
    return pl.BlockSpec(shape, lambda *_: (0,) * n)


def _layer(shape, l, single_buffer=False):
    n = len(shape)
    mode = pl.Buffered(1) if single_buffer else None
    return pl.BlockSpec((None,) + tuple(shape), lambda *_: (l,) + (0,) * n, pipeline_mode=mode)


def _mixer_call(l, x, sinks, w_in, lnvg, lnvb, ws, bs_exp, convw, convb, dtb, alog, dskip, gnw,
                w_out, ln1g, ln1b, *, tl):
    bsz, seq, _ = x.shape
    nt = seq // tl
    kern = functools.partial(_mixer_kernel, tl=tl, layer=l)
    in_specs = [
        pl.BlockSpec(memory_space=pltpu.SMEM),
        pl.BlockSpec((1, tl, D_MODEL), lambda b, t: (b, t, 0)),
        _layer((D_MODEL, D_IN_PAD), l, single_buffer=True),
        _layer((1, A_WIDTH), l), _layer((1, A_WIDTH), l),
        _layer((4, CHUNK, CHUNK), l), _layer((CHUNK, A_WIDTH), l),
        _layer((CONV_W, CONV_DIM), l), _layer((1, CONV_DIM), l),
        _layer((SSM_HEADS, 1), l), _layer((SSM_HEADS, 1), l),
        _layer((1, C_WIDTH), l), _layer((1, C_WIDTH), l),
        _layer((D_MODEL, D_MODEL), l, single_buffer=True),
        _layer((1, D_MODEL), l), _layer((1, D_MODEL), l),
    ]
    out_shape = (
        jax.ShapeDtypeStruct((bsz, seq, D_MODEL), F32),
        jax.ShapeDtypeStruct((bsz, CHUNK, KV_WIDTH), F32),
        jax.ShapeDtypeStruct((bsz, CHUNK, KV_WIDTH), F32),
        jax.ShapeDtypeStruct((bsz, CONV_W - 1, CONV_DIM), F32),
        jax.ShapeDtypeStruct((bsz, C_WIDTH, D_STATE), F32),
    )
    out_specs = (
        pl.BlockSpec((1, tl, D_MODEL), lambda b, t: (b, t, 0)),
        pl.BlockSpec((1, CHUNK, KV_WIDTH), lambda b, t: (b, 0, 0)),
        pl.BlockSpec((1, CHUNK, KV_WIDTH), lambda b, t: (b, 0, 0)),
        pl.BlockSpec((1, CONV_W - 1, CONV_DIM), lambda b, t: (b, 0, 0)),
        pl.BlockSpec((1, C_WIDTH, D_STATE), lambda b, t: (b, 0, 0)),
    )
    scratch = [
        pltpu.VMEM((tl, OFF_XBC), F32),
        pltpu.VMEM((SUBLANES + tl, CONV_DIM), F32),
        pltpu.VMEM((tl, CONV_DIM), F32),
        pltpu.VMEM((tl, LANES), F32),
        pltpu.VMEM((tl, D_MODEL), BF16),
        pltpu.VMEM((CHUNK, KV_WIDTH), F32),
        pltpu.VMEM((CHUNK, KV_WIDTH), F32),
        pltpu.VMEM((C_WIDTH, D_STATE), F32),
        pltpu.VMEM((4 * CHUNK, CHUNK), BF16),
        pltpu.VMEM((tl, D_MODEL), BF16),
        pltpu.VMEM((tl, D_MODEL), F32),
    ]
    return pl.pallas_call(
        kern, out_shape=out_shape, grid=(bsz, nt), in_specs=in_specs, out_specs=out_specs,
        scratch_shapes=scratch, name="prompt_mixer",
        compiler_params=pltpu.CompilerParams(
            dimension_semantics=("arbitrary", "arbitrary"), vmem_limit_bytes=VMEM_LIMIT),
    )(sinks, x, w_in, lnvg, lnvb, ws, bs_exp, convw, convb, dtb, alog, dskip, gnw, w_out, ln1g, ln1b)


FF_CHUNK = 1024


def _ffn(x, w1_ref, w2_ref, g_ref, b_ref):
    xb = x.astype(BF16)
    acc = jnp.zeros(x.shape, F32)
    for c in range(D_FF // FF_CHUNK):
        sl = slice(c * FF_CHUNK, (c + 1) * FF_CHUNK)
        h = jnp.maximum(_dot(xb, w1_ref[:, sl]), 0.0)
        acc = acc + _dot((h * h).astype(BF16), w2_ref[sl, :])
    return _layer_norm(ALPHA * x + acc, g_ref[...], b_ref[...])


def _ffn_kernel(x_ref, w1_ref, w2_ref, g_ref, b_ref, o_ref):
    o_ref[...] = _ffn(x_ref[...], w1_ref, w2_ref, g_ref, b_ref)


def _ffn_call(l, x2d, w1, w2, g, b, *, tm):
    rows = x2d.shape[0]
    return pl.pallas_call(
        _ffn_kernel, out_shape=jax.ShapeDtypeStruct((rows, D_MODEL), F32),
        grid=(rows // tm,),
        in_specs=[pl.BlockSpec((tm, D_MODEL), lambda i: (i, 0)),
                  _layer((D_MODEL, D_FF), l, single_buffer=True),
                  _layer((D_FF, D_MODEL), l, single_buffer=True),
                  _layer((1, D_MODEL), l), _layer((1, D_MODEL), l)],
        out_specs=pl.BlockSpec((tm, D_MODEL), lambda i: (i, 0)),
        name="ffn",
        compiler_params=pltpu.CompilerParams(
            dimension_semantics=("arbitrary",), vmem_limit_bytes=VMEM_LIMIT),
    )(x2d, w1, w2, g, b)


def _s1_kernel(x_ref, w_in_ref, lnvg_ref, lnvb_ref, ws0_ref, bs0_ref, convw_ref, convb_ref,
               cst_ref, dtb_ref, alog_ref, e_ref,
               vn_ref, outa_ref, q_ref, k_ref, v_ref, z_ref, cnew_ref, xs_ref, bm_ref,
               xT_ref, cT_ref, dec_ref):
    xb = x_ref[...].astype(BF16)
    proj = _dot(xb, w_in_ref[...])
    u = jax.nn.gelu(proj[:, OFF_AU:OFF_AU + A_WIDTH])
    vn = _layer_norm(jax.nn.gelu(proj[:, OFF_AV:OFF_AV + A_WIDTH]), lnvg_ref[...], lnvb_ref[...])
    vn_ref[...] = vn
    outa_ref[...] = u * (ws0_ref[...] * vn + bs0_ref[...])
    q_ref[...] = proj[:, OFF_Q:OFF_Q + B_WIDTH]
    k_ref[...] = proj[:, OFF_K:OFF_K + KV_WIDTH]
    v_ref[...] = proj[:, OFF_V:OFF_V + KV_WIDTH]
    z_ref[...] = proj[:, OFF_Z:OFF_Z + C_WIDTH]
    xbc = proj[:, OFF_XBC:OFF_DT]
    acc = convb_ref[...] + xbc * convw_ref[CONV_W - 1:CONV_W, :]
    for i in range(CONV_W - 1):
        acc = acc + cst_ref[i] * convw_ref[i:i + 1, :]
    cnew_ref[0] = cst_ref[1]
    cnew_ref[1] = cst_ref[2]
    cnew_ref[2] = xbc
    act = _silu(acc)
    xs = act[:, 0:C_WIDTH]
    xs_ref[...] = xs
    bm_ref[...] = act[:, C_WIDTH:C_WIDTH + 2 * D_STATE]
    cT_ref[...] = act[:, C_WIDTH + 2 * D_STATE:CONV_DIM].T
    dt = jax.nn.softplus(proj[:, OFF_DT:D_IN_PAD] + dtb_ref[...])
    dec_ref[...] = jnp.exp(dt * (-jnp.exp(alog_ref[...])))
    dt_exp = _dot_exact_rhs(dt, e_ref[...])
    xT_ref[...] = (xs * dt_exp).T


S2_TB = 2 * SUBLANES


def _s2_kernel(sinks_ref, q_ref, kn_ref, vn_ref, kst_ref, vst_ref, h_ref, xT_ref, bm_ref,
               cT_ref, dec_ref, ob_ref, hnew_ref, yT_ref, *, layer, step):
    i = step
    tb = S2_TB
    lane = lax.broadcasted_iota(jnp.int32, (tb, LANES), 1)
    lo = lane < HEAD_DIM

    q01 = q_ref[:, 0:LANES]
    q23 = q_ref[:, LANES:2 * LANES]
    qh = jnp.concatenate([
        jnp.where(lo, q01, 0.0),
        jnp.where(lo, pltpu.roll(q01, HEAD_DIM, 1), 0.0),
        jnp.where(lo, 0.0, pltpu.roll(q23, HEAD_DIM, 1)),
        jnp.where(lo, 0.0, q23)], axis=0)
    qhb = qh.astype(BF16)
    rowi = lax.broadcasted_iota(jnp.int32, (4 * tb, LANES), 0)
    coli = lax.broadcasted_iota(jnp.int32, (4 * tb, LANES), 1)
    tok = jnp.bitwise_and(rowi, tb - 1)
    s = jnp.zeros((4 * tb, LANES), F32)
    for j in range(tb):
        sj = _dot_nt(qhb, kst_ref[j])
        s = jnp.where(tok == j, sj, s)
    s = jnp.where(coli >= 1, s * ATT_SCALE, NEG)
    kn4 = jnp.concatenate([kn_ref[...]] * 4, axis=0)
    vn4 = jnp.concatenate([vn_ref[...]] * 4, axis=0)
    s_new = jnp.sum(qh * kn4, axis=-1, keepdims=True) * ATT_SCALE
    head = jnp.right_shift(rowi[:, 0:1], tb.bit_length() - 1)
    sink = jnp.where(head == 0, sinks_ref[layer, 0],
                     jnp.where(head == 1, sinks_ref[layer, 1],
                               jnp.where(head == 2, sinks_ref[layer, 2], sinks_ref[layer, 3])))
    m = jnp.maximum(jnp.maximum(jnp.max(s, axis=-1, keepdims=True), s_new), sink)
    p = jnp.exp(s - m)
    p_new = jnp.exp(s_new - m)
    inv = 1.0 / (jnp.sum(p, axis=-1, keepdims=True) + p_new + jnp.exp(sink - m))
    p = p * inv
    o = (p_new * inv) * vn4
    for j in range(tb):
        pj = jnp.where(tok == j, p, 0.0).astype(BF16)
        o = o + _dot(pj, vst_ref[j])
    ob_ref[:, 0:LANES] = jnp.where(lo, o[0:tb], pltpu.roll(o[tb:2 * tb], HEAD_DIM, 1))
    ob_ref[:, LANES:2 * LANES] = jnp.where(lo, pltpu.roll(o[2 * tb:3 * tb], HEAD_DIM, 1), o[3 * tb:])

    @pl.when(i == 0)
    def _():
        yT_ref[...] = jnp.zeros_like(yT_ref)

    r8 = lax.broadcasted_iota(jnp.int32, (tb, D_STATE), 0)
    c128 = lax.broadcasted_iota(jnp.int32, (LANES, LANES), 1)
    hp_g = C_WIDTH // 2
    b0 = pl.multiple_of(i * tb, tb)
    zrows = jnp.zeros((LANES - tb, tb * D_STATE), BF16)
    to_front = jnp.bitwise_and(LANES - b0, LANES - 1)
    for g in range(2):
        gs = slice(g * hp_g, (g + 1) * hp_g)
        ns = slice(g * D_STATE, (g + 1) * D_STATE)
        bblk = bm_ref[pl.ds(b0, tb), ns]
        bdiag = jnp.concatenate([jnp.where(r8 == j, bblk, 0.0) for j in range(tb)], axis=1)
        rhs = jnp.concatenate([bdiag.astype(BF16), zrows], axis=0)
        xg = pltpu.roll(xT_ref[gs, :], to_front, 1).astype(BF16)
        upd = _dot(xg, rhs)
        hns = []
        cms = []
        for j in range(tb):
            drow = dec_ref[pl.ds(b0 + j, 1), :]
            parts = []
            for hh in range(4):
                h = g * 4 + hh
                rs = slice(h * HEAD_DIM, (h + 1) * HEAD_DIM)
                hn = (h_ref[j, rs, :] * jnp.broadcast_to(drow[:, h:h + 1], (HEAD_DIM, D_STATE))
                      + upd[hh * HEAD_DIM:(hh + 1) * HEAD_DIM, j * D_STATE:(j + 1) * D_STATE])
                hnew_ref[j, rs, :] = hn
                parts.append(hn)
            hns.append(jnp.concatenate(parts, axis=0).astype(BF16))
            cms.append(jnp.where(c128 == b0 + j, cT_ref[ns, :], 0.0).astype(BF16))
        yT_ref[gs, :] += _dot(jnp.concatenate(hns, axis=1), jnp.concatenate(cms, axis=0))


def _s3_kernel(yT_ref, xs_ref, z_ref, dskip_ref, gnw_ref, outa_ref, outb_ref, x_ref, w_out_ref,
               g_ref, b_ref, x1_ref):
    out_c = _gated_rmsnorm(yT_ref[...].T, xs_ref[...], _silu(z_ref[...]), dskip_ref[...], gnw_ref[...])
    mix = jnp.concatenate([outa_ref[...], outb_ref[...], out_c], axis=1).astype(BF16)
    x1_ref[...] = _layer_norm(ALPHA * x_ref[...] + _dot(mix, w_out_ref[...]), g_ref[...], b_ref[...])


def _sample_kernel(sinks_ref, x_ref, w_in_ref, lnvg_ref, lnvb_ref, ws0_ref, bs0_ref, convw_ref,
                   convb_ref, cst_ref, dtb_ref, alog_ref, e_ref, kst_ref, vst_ref, h_ref,
                   dskip_ref, gnw_ref, w_out_ref, ln1g_ref, ln1b_ref, w1_ref, w2_ref, ln2g_ref, ln2b_ref,
                   ys_ref, vns_ref, ksm_ref, vsm_ref, csm_ref, hnew_ref,
                   res_scr, outa_scr, q_scr, z_scr, xs_scr, bm_scr, xT_scr, cT_scr, dec_scr,
                   outb_scr, yT_scr, x1_scr):
    l = pl.program_id(0)
    i = pl.program_id(1)
    last = pl.num_programs(1) - 1

    @pl.when(jnp.logical_and(l == 0, i == 0))
    def _():
        res_scr[...] = x_ref[...]

    @pl.when(i == 0)
    def _():
        _s1_kernel(res_scr, w_in_ref, lnvg_ref, lnvb_ref, ws0_ref, bs0_ref, convw_ref, convb_ref,
                   cst_ref, dtb_ref, alog_ref, e_ref,
                   vns_ref, outa_scr, q_scr, ksm_ref, vsm_ref, z_scr, csm_ref, xs_scr, bm_scr,
                   xT_scr, cT_scr, dec_scr)

    rows = pl.ds(pl.multiple_of(i * S2_TB, S2_TB), S2_TB)
    _s2_kernel(sinks_ref, q_scr.at[rows], ksm_ref.at[rows], vsm_ref.at[rows], kst_ref, vst_ref,
               h_ref, xT_scr, bm_scr, cT_scr, dec_scr, outb_scr.at[rows], hnew_ref, yT_scr,
               layer=l, step=i)

    @pl.when(i == last)
    def _():
        _s3_kernel(yT_scr, xs_scr, z_scr, dskip_ref, gnw_ref, outa_scr, outb_scr, res_scr,
                   w_out_ref, ln1g_ref, ln1b_ref, x1_scr)
        res_scr[...] = _ffn(x1_scr[...], w1_ref, w2_ref, ln2g_ref, ln2b_ref)
        ys_ref[...] = res_scr[...]


def _sample_call(x, sinks, w_in, lnvg, lnvb, ws0, bs0, convw, convb, cst, dtb, alog, emat,
                 kst, vst, hst, dskip, gnw, w_out, ln1g, ln1b, w1, w2, ln2g, ln2b):
    n = x.shape[0]
    tb = S2_TB

    def per_layer(shape, single_buffer=False):
        k = len(shape)
        mode = pl.Buffered(1) if single_buffer else None
        return pl.BlockSpec((None,) + tuple(shape), lambda l, i: (l,) + (0,) * k, pipeline_mode=mode)

    def per_block(shape):
        k = len(shape) - 1
        return pl.BlockSpec((None,) + tuple(shape), lambda l, i: (l, i) + (0,) * k)

    in_specs = [
        pl.BlockSpec(memory_space=pltpu.SMEM),
        _full((n, D_MODEL)),
        per_layer((D_MODEL, D_IN_PAD), True),
        per_layer((1, A_WIDTH)), per_layer((1, A_WIDTH)), per_layer((1, A_WIDTH)), per_layer((1, A_WIDTH)),
        per_layer((CONV_W, CONV_DIM)), per_layer((1, CONV_DIM)), per_layer((CONV_W - 1, n, CONV_DIM)),
        per_layer((1, LANES)), per_layer((1, LANES)), _full((LANES, C_WIDTH)),
        per_block((tb, CHUNK, KV_WIDTH)), per_block((tb, CHUNK, KV_WIDTH)),
        per_block((tb, C_WIDTH, D_STATE)),
        per_layer((1, C_WIDTH)), per_layer((1, C_WIDTH)),
        per_layer((D_MODEL, D_MODEL), True), per_layer((1, D_MODEL)), per_layer((1, D_MODEL)),
        per_layer((D_MODEL, D_FF), True), per_layer((D_FF, D_MODEL), True),
        per_layer((1, D_MODEL)), per_layer((1, D_MODEL)),
    ]
    out_shape = (
        jax.ShapeDtypeStruct((n, D_MODEL), F32),
        jax.ShapeDtypeStruct((DEPTH, n, A_WIDTH), F32),
        jax.ShapeDtypeStruct((DEPTH, n, KV_WIDTH), F32),
        jax.ShapeDtypeStruct((DEPTH, n, KV_WIDTH), F32),
        jax.ShapeDtypeStruct((DEPTH, CONV_W - 1, n, CONV_DIM), F32),
        jax.ShapeDtypeStruct((DEPTH, n, C_WIDTH, D_STATE), F32),
    )
    out_specs = (
        _full((n, D_MODEL)),
        per_layer((n, A_WIDTH)), per_layer((n, KV_WIDTH)), per_layer((n, KV_WIDTH)),
        per_layer((CONV_W - 1, n, CONV_DIM)),
        per_block((tb, C_WIDTH, D_STATE)),
    )
    scratch = [
        pltpu.VMEM((n, D_MODEL), F32),
        pltpu.VMEM((n, A_WIDTH), F32),
        pltpu.VMEM((n, B_WIDTH), F32),
        pltpu.VMEM((n, C_WIDTH), F32),
        pltpu.VMEM((n, C_WIDTH), F32),
        pltpu.VMEM((n, 2 * D_STATE), F32),
        pltpu.VMEM((C_WIDTH, n), F32),
        pltpu.VMEM((2 * D_STATE, n), F32),
        pltpu.VMEM((n, LANES), F32),
        pltpu.VMEM((n, B_WIDTH), F32),
        pltpu.VMEM((C_WIDTH, n), F32),
        pltpu.VMEM((n, D_MODEL), F32),
    ]
    return pl.pallas_call(
        _sample_kernel, out_shape=out_shape, grid=(DEPTH, n // tb), in_specs=in_specs,
        out_specs=out_specs, scratch_shapes=scratch, name="sample_step",
        compiler_params=pltpu.CompilerParams(
            dimension_semantics=("arbitrary", "arbitrary"), vmem_limit_bytes=VMEM_LIMIT),
    )(sinks, x, w_in, lnvg, lnvb, ws0, bs0, convw, convb, cst, dtb, alog, emat, kst, vst, hst,
      dskip, gnw, w_out, ln1g, ln1b, w1, w2, ln2g, ln2b)


PROMPT_TL = 1024
PROMPT_TM = 1024


def kernel(x_prompt, x_sample, state_attn_k, state_attn_v, state_conv, state_ssm, w_in, ln_v_g,
           ln_v_b, w_s, b_s, sinks, conv_w, conv_b, dt_bias, a_log, d_skip, gn_w, w_out, ln1_g,
           ln1_b, w1, w2, ln2_g, ln2_b):
    bsz, seq, _ = x_prompt.shape
    n_s = x_sample.shape[0]
    d_in = w_in.shape[-1]

    w_in_b = jnp.pad(w_in, ((0, 0), (0, 0), (0, D_IN_PAD - d_in))).astype(BF16)
    w_out_b = w_out.astype(BF16)
    w1_b = w1.astype(BF16)
    w2_b = w2.astype(BF16)
    pad_h = ((0, 0), (0, LANES - SSM_HEADS))
    dtb_p = jnp.pad(dt_bias, pad_h)[:, None, :]
    alog_p = jnp.pad(a_log, pad_h)[:, None, :]
    dskip_e = jnp.repeat(d_skip, HEAD_DIM, axis=-1)[:, None, :]
    bs_e = jnp.repeat(jnp.swapaxes(b_s, 1, 2), HEAD_DIM, axis=-1)
    ws0_e = jnp.repeat(w_s[:, :, 0, 0], HEAD_DIM, axis=-1)[:, None, :]
    bs0_e = jnp.repeat(b_s[:, :, 0], HEAD_DIM, axis=-1)[:, None, :]
    emat = (lax.broadcasted_iota(jnp.int32, (LANES, C_WIDTH), 0)
            == lax.broadcasted_iota(jnp.int32, (LANES, C_WIDTH), 1) // HEAD_DIM).astype(BF16)
    row = lambda a: a[:, None, :]
    lnvg, lnvb, convb, gnw = row(ln_v_g), row(ln_v_b), row(conv_b), row(gn_w)
    ln1g, ln1b, ln2g, ln2b = row(ln1_g), row(ln1_b), row(ln2_g), row(ln2_b)
    kst = state_attn_k.reshape(DEPTH, n_s, CHUNK, KV_WIDTH).astype(BF16)
    vst = state_attn_v.reshape(DEPTH, n_s, CHUNK, KV_WIDTH).astype(BF16)
    hst = state_ssm.reshape(DEPTH, n_s, C_WIDTH, D_STATE)
    cst = jnp.swapaxes(state_conv, 1, 2)

    ys, vns, ksm, vsm, csm, hsm = _sample_call(
        x_sample.reshape(n_s, D_MODEL), sinks, w_in_b, lnvg, lnvb, ws0_e, bs0_e, conv_w, convb, cst,
        dtb_p, alog_p, emat, kst, vst, hst, dskip_e, gnw, w_out_b, ln1g, ln1b, w1_b, w2_b, ln2g, ln2b)

    yp = x_prompt
    kp, vp, cp, hp = [], [], [], []
    for l in range(DEPTH):
        x1, k_l, v_l, c_l, h_l = _mixer_call(
            l, yp, sinks, w_in_b, lnvg, lnvb, w_s, bs_e, conv_w, convb,
            dt_bias[:, :, None], a_log[:, :, None], dskip_e, gnw, w_out_b, ln1g, ln1b, tl=PROMPT_TL)
        yp = _ffn_call(l, x1.reshape(bsz * seq, D_MODEL), w1_b, w2_b, ln2g, ln2b,
                       tm=PROMPT_TM).reshape(bsz, seq, D_MODEL)
        kp.append(k_l); vp.append(v_l); cp.append(c_l); hp.append(h_l)

    kv_p = (DEPTH, bsz, CHUNK, 2, HEAD_DIM)
    kv_s = (DEPTH, n_s, 1, 2, HEAD_DIM)
    ssm_shape = (SSM_HEADS, HEAD_DIM, D_STATE)
    return (yp, ys.reshape(n_s, 1, D_MODEL),
            jnp.stack(kp).reshape(kv_p), jnp.stack(vp).reshape(kv_p),
            jnp.stack(cp), jnp.stack(hp).reshape((DEPTH, bsz) + ssm_shape),
            ksm.reshape(kv_s), vsm.reshape(kv_s),
            jnp.swapaxes(csm, 1, 2),
            hsm.reshape((DEPTH, n_s) + ssm_shape),
            vns.reshape(DEPTH, n_s, 1, A_WIDTH))
```
